```python
import math
import jax
import jax.numpy as jnp
from jax import lax
import numpy as np

D_MODEL = 2048
BATCH = 2
SEQ = 8192
DEPTH = 4

GRID_W = 64
CTX_LEN = 256
MIX_W = D_MODEL // 2
N_BRANCH = 3

RWKV_HD = 64
RWKV_HEADS = MIX_W // RWKV_HD
RWKV_DECAY_RANK = 64
RWKV_AAA_RANK = 64
RWKV_GATE_RANK = 128
RWKV_SHIFT_W = 3 * MIX_W + RWKV_DECAY_RANK + RWKV_AAA_RANK + RWKV_GATE_RANK
RWKV_GN_EPS = 64e-5

RET_HEADS = 4
RET_V_HD = MIX_W // RET_HEADS
RET_QK_HD = RET_V_HD // 2
RET_QK_W = RET_HEADS * RET_QK_HD
RET_CHUNK = 128
RET_GN_EPS = 1e-6

NA_HD = 64
NA_HEADS = MIX_W // NA_HD
NA_WIN_ROWS = 8
NA_WIN_COLS = 16

N_EXPERTS = 32
TOP_K = 4
D_EXPERT = D_MODEL // 4
SWIGLU_LIMIT = 7.0
SWIGLU_ALPHA = 1.702

ROPE_BASE = 10000.0
LN_EPS = 1e-6
DN_ALPHA = (2 * DEPTH) ** 0.25
DN_BETA = (8 * DEPTH) ** -0.25
IN_GROUPS = (RWKV_SHIFT_W, RET_QK_W, RET_QK_W, MIX_W, MIX_W, MIX_W, MIX_W, MIX_W, N_BRANCH * D_MODEL)
D_IN_PROJ = sum(IN_GROUPS)

kernel_name = "hybrid_diffusion_trunk_rwkv7_retnet_natten_moe"


def _split(a, sizes):
    idx, acc = [], 0
    for s in sizes[:-1]:
        acc += s
        idx.append(acc)
    return jnp.split(a, idx, axis=-1)


def _layernorm(x, w=None, b=None, eps=LN_EPS):
    xf = x.astype(jnp.float32)
    mu = jnp.mean(xf, -1, keepdims=True)
    var = jnp.mean(jnp.square(xf - mu), -1, keepdims=True)
    y = (xf - mu) * lax.rsqrt(var + eps)
    if w is not None:
        y = y * w.astype(jnp.float32) + b.astype(jnp.float32)
    return y.astype(x.dtype)


def _head_norm(y, w, b, eps):
    B, T, H, hd = y.shape
    yf = y.astype(jnp.float32)
    mu = jnp.mean(yf, -1, keepdims=True)
    var = jnp.mean(jnp.square(yf - mu), -1, keepdims=True)
    yn = ((yf - mu) * lax.rsqrt(var + eps)).reshape(B, T, H * hd)
    return yn * w.astype(jnp.float32) + b.astype(jnp.float32)


def _adaln(cond, ada_w, ada_b):
    m = jax.nn.silu(cond) @ ada_w + ada_b
    return jnp.split(m, 6, axis=-1)


def _modulate(x, shift, scale):
    return _layernorm(x) * (1.0 + scale) + shift


def _centred_conv3(x, taps):
    xp = jnp.pad(x, ((0, 0), (1, 1), (0, 0)))
    return xp[:, :-2] * taps[0] + xp[:, 1:-1] * taps[1] + xp[:, 2:] * taps[2]


def _axial_rope(x, rows, cols):
    hd = x.shape[-1]
    half = hd // 2
    nf = half // 2
    inv = ROPE_BASE ** (-jnp.arange(nf, dtype=jnp.float32) / nf)
    xf = x.astype(jnp.float32)

    def rot(xa, pos):
        ang = pos.astype(jnp.float32)[:, None] * inv[None, :]
        cos = jnp.cos(ang)[None, :, None, :]
        sin = jnp.sin(ang)[None, :, None, :]
        x1, x2 = xa[..., :nf], xa[..., nf:]
        return jnp.concatenate([x1 * cos - x2 * sin, x1 * sin + x2 * cos], -1)

    return jnp.concatenate([rot(xf[..., :half], rows), rot(xf[..., half:], cols)], -1)


def _rwkv_prepare(p, shift, w0, wB, a0, aB, gB, k_k, k_a):
    B, T, _ = p.shape
    z = _centred_conv3(p, shift)
    r, k, v, wl, al, gl = _split(z, (MIX_W, MIX_W, MIX_W, RWKV_DECAY_RANK, RWKV_AAA_RANK, RWKV_GATE_RANK))
    heads = lambda a: a.astype(jnp.float32).reshape(B, T, RWKV_HEADS, RWKV_HD)
    hv = lambda a: a.astype(jnp.float32).reshape(RWKV_HEADS, RWKV_HD)
    g = jax.nn.sigmoid(gl) @ gB
    kk = heads(k) * hv(k_k)
    kk = kk / jnp.maximum(jnp.sqrt(jnp.sum(kk * kk, -1, keepdims=True)), 1e-12)
    kh = heads(k)
    dirs = []
    for d in range(2):
        w = jnp.exp(-math.exp(-0.5) * jax.nn.sigmoid(heads(w0[d] + jnp.tanh(wl) @ wB[d])))
        a = jax.nn.sigmoid(heads(a0[d] + al @ aB[d]))
        kd = kh * (1.0 + (a - 1.0) * hv(k_a))
        dirs.append((w, kk * a, kd))
    return heads(r), heads(v), g, kk, dirs


def _rwkv_scan(S0, r, v, kk, w, b, k, reverse, with_out):
    tm = lambda a: jnp.swapaxes(a, 0, 1)

    def step(S, inp):
        r_t, v_t, kk_t, w_t, b_t, k_t = inp
        sa = -jnp.einsum('bhvk,bhk->bhv', S, kk_t)
        S = S * w_t[:, :, None, :] + sa[..., None] * b_t[:, :, None, :] + v_t[..., None] * k_t[:, :, None, :]
        y = jnp.einsum('bhvk,bhk->bhv', S, r_t) if with_out else None
        return S, y

    S, ys = lax.scan(step, S0, (tm(r), tm(v), tm(kk), tm(w), tm(b), tm(k)), reverse=reverse)
    return S, (tm(ys) if with_out else None)


def _rwkv_finish(y, r, v, ks, g, r_k, gn_w, gn_b):
    B, T = y.shape[:2]
    rk = r_k.astype(jnp.float32).reshape(RWKV_HEADS, RWKV_HD)
    bonus = sum(jnp.sum(r * kd * rk, -1, keepdims=True) for kd in ks) * v
    out = _head_norm(y, gn_w, gn_b, RWKV_GN_EPS) + bonus.reshape(B, T, MIX_W)
    return out * g.astype(jnp.float32)


def _rwkv_mixer(p_ctx, p_lat, need_ctx, shift, w0, wB, a0, aB, gB, k_k, k_a, r_k, gn_w, gn_b):
    prm = (shift, w0, wB, a0, aB, gB, k_k, k_a)
    rc, vc, gc, kkc, dc = _rwkv_prepare(p_ctx, *prm)
    rl, vl, gl, kkl, dl = _rwkv_prepare(p_lat, *prm)
    B = p_lat.shape[0]
    y_c, y_l = 0.0, 0.0
    for d in range(2):
        S0 = jnp.zeros((B, RWKV_HEADS, RWKV_HD, RWKV_HD), jnp.float32)
        wc, bc, kc = dc[d]
        S_c, yc = _rwkv_scan(S0, rc, vc, kkc, wc, bc, kc, d == 1, need_ctx)
        wl_, bl_, kl_ = dl[d]
        _, yl = _rwkv_scan(S_c, rl, vl, kkl, wl_, bl_, kl_, d == 1, True)
        y_l = y_l + yl
        if need_ctx:
            y_c = y_c + yc
    out_l = _rwkv_finish(y_l, rl, vl, [dd[2] for dd in dl], gl, r_k, gn_w, gn_b).astype(p_lat.dtype)
    out_c = _rwkv_finish(y_c, rc, vc, [dd[2] for dd in dc], gc, r_k, gn_w, gn_b).astype(p_ctx.dtype) if need_ctx else None
    return out_l, out_c


def _ret_chunkwise(q, k, v, gamma, R0, with_out):
    B, H, T, _ = q.shape
    C = RET_CHUNK
    n = T // C
    lg = jnp.log(gamma)
    i = jnp.arange(C, dtype=jnp.float32)
    diff = i[:, None] - i[None, :]
    D = jnp.where(diff >= 0, jnp.exp(lg[:, None, None] * jnp.maximum(diff, 0.0)), 0.0)
    q_dec = jnp.exp(lg[:, None] * (i + 1.0))[None, :, :, None]
    k_dec = jnp.exp(lg[:, None] * (C - 1.0 - i))[None, :, :, None]
    c_dec = jnp.exp(lg * C)[None, :, None, None]
    chunks = lambda a: a.reshape(B, H, n, C, a.shape[-1]).transpose(2, 0, 1, 3, 4)

    def step(R, inp):
        qc, kc, vc = inp
        R_new = R * c_dec + jnp.einsum('bhcd,bhce->bhde', kc * k_dec, vc)
        if with_out:
            s = jnp.einsum('bhid,bhjd->bhij', qc, kc) * D[None]
            o = jnp.einsum('bhij,bhje->bhie', s, vc) + jnp.einsum('bhid,bhde->bhie', qc * q_dec, R)
        else:
            o = None
        return R_new, o

    R, o = lax.scan(step, R0, (chunks(q), chunks(k), chunks(v)))
    if with_out:
        o = o.transpose(1, 2, 0, 3, 4).reshape(B, H, T, -1)
    return R, o


def _ret_mixer(pc, pl, rows, cols, need_ctx, decay_logit, gn_w, gn_b):
    def heads(a, hd):
        B, T, _ = a.shape
        return a.astype(jnp.float32).reshape(B, T, RET_HEADS, hd).transpose(0, 2, 1, 3)

    def rope_heads(a):
        B, T, _ = a.shape
        return _axial_rope(a.reshape(B, T, RET_HEADS, RET_QK_HD), rows, cols).transpose(0, 2, 1, 3)

    k_scale = RET_QK_HD ** -0.5
    qc, kc, vc, gc = pc
    ql, kl, vl, gl = pl
    Qc, Kc, Vc = heads(qc, RET_QK_HD), heads(kc, RET_QK_HD) * k_scale, heads(vc, RET_V_HD)
    Ql, Kl, Vl = rope_heads(ql), rope_heads(kl) * k_scale, heads(vl, RET_V_HD)
    gammas = jax.nn.sigmoid(decay_logit.astype(jnp.float32))
    B = ql.shape[0]
    flip = lambda a: a[:, :, ::-1]
    ident = lambda a: a
    o_c, o_l = 0.0, 0.0
    for d in range(2):
        f = flip if d == 1 else ident
        R0 = jnp.zeros((B, RET_HEADS, RET_QK_HD, RET_V_HD), jnp.float32)
        R_c, oc = _ret_chunkwise(f(Qc), f(Kc), f(Vc), gammas[d], R0, need_ctx)
        _, ol = _ret_chunkwise(f(Ql), f(Kl), f(Vl), gammas[d], R_c, True)
        o_l = o_l + f(ol)
        if need_ctx:
            o_c = o_c + f(oc)

    def finish(o, g):
        y = _head_norm(o.transpose(0, 2, 1, 3), gn_w, gn_b, RET_GN_EPS) * jax.nn.silu(g.astype(jnp.float32))
        return y.astype(g.dtype)

    return finish(o_l, gl), (finish(o_c, gc) if need_ctx else None)


def _na_mixer(pc, pl, n_rows, need_ctx, rpb):
    q_c, k_c, v_c = pc
    q_l, k_l, v_l = pl
    B, T, _ = q_l.shape
    H, hd = NA_HEADS, NA_HD
    scale = hd ** -0.5
    kr = min(NA_WIN_ROWS, n_rows)
    kcw = NA_WIN_COLS
    grid = lambda a: a.reshape(B, n_rows, GRID_W, H, hd).transpose(0, 3, 1, 2, 4)
    heads = lambda a: a.reshape(B, a.shape[1], H, hd).transpose(0, 2, 1, 3)
    qg, kg, vg = grid(q_l), grid(k_l), grid(v_l)
    kch, vch = heads(k_c), heads(v_c)
    col = jnp.arange(GRID_W)
    cs = jnp.clip(col - kcw // 2, 0, GRID_W - kcw)
    col_ok = (col[None, :] >= cs[:, None]) & (col[None, :] < cs[:, None] + kcw)
    col_ok = jnp.tile(col_ok, (1, kr))
    col_idx = jnp.clip(col[None, :] - col[:, None] + (NA_WIN_COLS - 1), 0, 2 * NA_WIN_COLS - 2)
    n_loc = kr * GRID_W

    def row_block(r):
        rs = jnp.clip(r - kr // 2, 0, n_rows - kr)
        q_r = lax.dynamic_index_in_dim(qg, r, axis=2, keepdims=False)
        k_r = lax.dynamic_slice_in_dim(kg, rs, kr, axis=2).reshape(B, H, n_loc, hd)
        v_r = lax.dynamic_slice_in_dim(vg, rs, kr, axis=2).reshape(B, H, n_loc, hd)
        row_idx = rs + jnp.arange(kr) - r + (NA_WIN_ROWS - 1)
        bias = rpb[:, row_idx][:, :, col_idx]
        bias = bias.transpose(0, 2, 1, 3).reshape(H, GRID_W, n_loc).astype(jnp.float32)
        s_loc = jnp.einsum('bhqd,bhkd->bhqk', q_r, k_r).astype(jnp.float32) * scale + bias
        s_loc = jnp.where(col_ok, s_loc, -jnp.inf)
        s_ctx = jnp.einsum('bhqd,bhkd->bhqk', q_r, kch).astype(jnp.float32) * scale
        p = jax.nn.softmax(jnp.concatenate([s_loc, s_ctx], -1), axis=-1).astype(v_r.dtype)
        return jnp.einsum('bhqk,bhkd->bhqd', p[..., :n_loc], v_r) + jnp.einsum('bhqk,bhkd->bhqd', p[..., n_loc:], vch)

    o = lax.map(row_block, jnp.arange(n_rows))
    out_l = o.transpose(1, 0, 3, 2, 4).reshape(B, T, H * hd)
    out_c = None
    if need_ctx:
        s = jnp.einsum('bhqd,bhkd->bhqk', heads(q_c), kch).astype(jnp.float32) * scale
        p = jax.nn.softmax(s, axis=-1).astype(vch.dtype)
        out_c = jnp.einsum('bhqk,bhkd->bhqd', p, vch).transpose(0, 2, 1, 3).reshape(B, q_c.shape[1], H * hd)
    return out_l, out_c


def _merge(ys, gate_pre, w_branch, w_out):
    gates = jax.nn.sigmoid(gate_pre.astype(jnp.float32)).astype(gate_pre.dtype)
    merged = 0.0
    for i, y in enumerate(ys):
        merged = merged + gates[..., i * D_MODEL:(i + 1) * D_MODEL] * (y @ w_branch[i])
    return merged @ w_out


def _mixer_block(h_ctx, h_lat, rows, cols, n_rows, need_ctx, w_in, rwkv_shift, rwkv_w0, rwkv_wB, rwkv_a0, rwkv_aB,
                 rwkv_gB, rwkv_kk, rwkv_ka, rwkv_rk, rwkv_gn_w, rwkv_gn_b, ret_decay, ret_gn_w, ret_gn_b, na_rpb,
                 w_branch, w_out):
    pl = _split(h_lat @ w_in, IN_GROUPS)
    pc = _split(h_ctx @ w_in, IN_GROUPS)
    a_l, a_c = _rwkv_mixer(pc[0], pl[0], need_ctx, rwkv_shift, rwkv_w0, rwkv_wB, rwkv_a0, rwkv_aB, rwkv_gB,
                           rwkv_kk, rwkv_ka, rwkv_rk, rwkv_gn_w, rwkv_gn_b)
    b_l, b_c = _ret_mixer(pc[1:5], pl[1:5], rows, cols, need_ctx, ret_decay, ret_gn_w, ret_gn_b)
    n_l, n_c = _na_mixer(pc[5:8], pl[5:8], n_rows, need_ctx, na_rpb)
    out_l = _merge((a_l, b_l, n_l), pl[8], w_branch, w_out)
    out_c = _merge((a_c, b_c, n_c), pc[8], w_branch, w_out) if need_ctx else None
    return out_l, out_c


def _moe(h, router_w, router_b, w1, b1, w2, b2):
    logits = (h @ router_w + router_b).astype(jnp.float32)
    top_v, top_i = lax.top_k(logits, TOP_K)
    probs = jax.nn.softmax(top_v, axis=-1)
    gates = jnp.einsum('nk,nke->ne', probs, jax.nn.one_hot(top_i, N_EXPERTS, dtype=jnp.float32)).astype(h.dtype)
    out = jnp.zeros_like(h)
    for e in range(N_EXPERTS):
        hu = h @ w1[e] + b1[e]
        g = jnp.minimum(hu[:, :D_EXPERT], SWIGLU_LIMIT)
        u = jnp.clip(hu[:, D_EXPERT:], -SWIGLU_LIMIT, SWIGLU_LIMIT)
        act = g * jax.nn.sigmoid(SWIGLU_ALPHA * g) * (u + 1.0)
        out = out + gates[:, e:e + 1] * (act @ w2[e] + b2[e])
    return out


def setup_inputs(seed: int = 0) -> dict:
    key = jax.random.key(seed)
    ks = iter(jax.random.split(key, 40))
    nrm = lambda shape, s: jax.random.normal(next(ks), shape, jnp.float32) * s
    L, D, E, F = DEPTH, D_MODEL, N_EXPERTS, D_EXPERT
    gain = lambda shape: 1.0 + nrm(shape, 0.02)
    ret_base = jnp.log(2.0 ** (5.0 + jnp.arange(RET_HEADS, dtype=jnp.float32)) - 1.0)
    shift_base = jnp.array([0.0, 1.0, 0.0], jnp.float32)[:, None]
    return {
        "x": nrm((BATCH, SEQ, D), 1.0),
        "c": nrm((BATCH, D), 1.0),
        "ctx": nrm((BATCH, CTX_LEN, D), 1.0),
        "c_ctx": nrm((D,), 1.0),
        "ada_w": nrm((L, D, 6 * D), 0.5 * D ** -0.5),
        "ada_b": nrm((L, 6 * D), 0.02),
        "w_in": nrm((L, D, D_IN_PROJ), D ** -0.5),
        "rwkv_shift": shift_base + nrm((L, 3, RWKV_SHIFT_W), 0.1),
        "rwkv_w0": nrm((L, 2, MIX_W), 0.5),
        "rwkv_wB": nrm((L, 2, RWKV_DECAY_RANK, MIX_W), 0.3 * RWKV_DECAY_RANK ** -0.5),
        "rwkv_a0": nrm((L, 2, MIX_W), 0.5),
        "rwkv_aB": nrm((L, 2, RWKV_AAA_RANK, MIX_W), 0.3 * RWKV_AAA_RANK ** -0.5),
        "rwkv_gB": nrm((L, RWKV_GATE_RANK, MIX_W), RWKV_GATE_RANK ** -0.5),
        "rwkv_kk": 0.85 + nrm((L, MIX_W), 0.05),
        "rwkv_ka": 1.0 + nrm((L, MIX_W), 0.1),
        "rwkv_rk": nrm((L, MIX_W), 0.1),
        "rwkv_gn_w": gain((L, MIX_W)),
        "rwkv_gn_b": nrm((L, MIX_W), 0.02),
        "ret_decay": ret_base[None, None, :] + nrm((L, 2, RET_HEADS), 0.1),
        "ret_gn_w": gain((L, MIX_W)),
        "ret_gn_b": nrm((L, MIX_W), 0.02),
        "na_rpb": nrm((L, NA_HEADS, 2 * NA_WIN_ROWS - 1, 2 * NA_WIN_COLS - 1), 0.1),
        "w_branch": nrm((L, N_BRANCH, MIX_W, D), DN_BETA * MIX_W ** -0.5),
        "w_out": nrm((L, D, D), DN_BETA * D ** -0.5),
        "ln1_w": gain((L, D)),
        "ln1_b": nrm((L, D), 0.02),
        "ln2_w": gain((L, D)),
        "ln2_b": nrm((L, D), 0.02),
        "router_w": nrm((L, D, E), D ** -0.5),
        "router_b": nrm((L, E), 0.01),
        "exp_w1": nrm((L, E, D, 2 * F), D ** -0.5),
        "exp_b1": nrm((L, E, 2 * F), 0.01),
        "exp_w2": nrm((L, E, F, D), DN_BETA * F ** -0.5),
        "exp_b2": nrm((L, E, D), 0.01),
    }


def reference(x, c, ctx, c_ctx, ada_w, ada_b, w_in, rwkv_shift, rwkv_w0, rwkv_wB, rwkv_a0, rwkv_aB, rwkv_gB,
              rwkv_kk, rwkv_ka, rwkv_rk, rwkv_gn_w, rwkv_gn_b, ret_decay, ret_gn_w, ret_gn_b, na_rpb, w_branch,
              w_out, ln1_w, ln1_b, ln2_w, ln2_b, router_w, router_b, exp_w1, exp_b1, exp_w2, exp_b2):
    B, T, _ = x.shape
    n_rows = T // GRID_W
    t = jnp.arange(T, dtype=jnp.int32)
    rows, cols = t // GRID_W, t % GRID_W
    xc = ctx
    for l in range(DEPTH):
        need_ctx = l < DEPTH - 1
        sh_a, sc_a, g_a, sh_f, sc_f, g_f = _adaln(c[:, None, :], ada_w[l], ada_b[l])
        csh_a, csc_a, cg_a, csh_f, csc_f, cg_f = _adaln(c_ctx, ada_w[l], ada_b[l])
        y_l, y_c = _mixer_block(_modulate(xc, csh_a, csc_a), _modulate(x, sh_a, sc_a), rows, cols, n_rows, need_ctx,
                                w_in[l], rwkv_shift[l], rwkv_w0[l], rwkv_wB[l], rwkv_a0[l], rwkv_aB[l], rwkv_gB[l],
                                rwkv_kk[l], rwkv_ka[l], rwkv_rk[l], rwkv_gn_w[l], rwkv_gn_b[l], ret_decay[l],
                                ret_gn_w[l], ret_gn_b[l], na_rpb[l], w_branch[l], w_out[l])
        x = _layernorm(DN_ALPHA * x + g_a * y_l, ln1_w[l], ln1_b[l])
        h = _modulate(x, sh_f, sc_f).reshape(B * T, D_MODEL)
        moe_p = (router_w[l], router_b[l], exp_w1[l], exp_b1[l], exp_w2[l], exp_b2[l])
        if need_ctx:
            xc = _layernorm(DN_ALPHA * xc + cg_a * y_c, ln1_w[l], ln1_b[l])
            hc = _modulate(xc, csh_f, csc_f).reshape(-1, D_MODEL)
            f = _moe(jnp.concatenate([h, hc], axis=0), *moe_p)
            f_l, f_c = f[:B * T], f[B * T:]
            xc = _layernorm(DN_ALPHA * xc + cg_f * f_c.reshape(xc.shape), ln2_w[l], ln2_b[l])
        else:
            f_l = _moe(h, *moe_p)
        x = _layernorm(DN_ALPHA * x + g_f * f_l.reshape(x.shape), ln2_w[l], ln2_b[l])
    return x
```

```python
import functools
import math

import jax
import jax.numpy as jnp
from jax import lax
from jax.experimental import pallas as pl
from jax.experimental.pallas import tpu as pltpu

F32 = jnp.float32
BF16 = jnp.bfloat16
HI = lax.Precision.HIGHEST

D_MODEL = 2048
GRID_W = 64
MIX_W = D_MODEL // 2
N_BRANCH = 3

RWKV_HD = 64
RWKV_HEADS = MIX_W // RWKV_HD
RWKV_DECAY_RANK = 64
RWKV_AAA_RANK = 64
RWKV_GATE_RANK = 128
RWKV_SHIFT_W = 3 * MIX_W + RWKV_DECAY_RANK + RWKV_AAA_RANK + RWKV_GATE_RANK
RWKV_GN_EPS = 64e-5
RWKV_CHUNK = 64

RET_HEADS = 4
RET_V_HD = MIX_W // RET_HEADS
RET_QK_HD = RET_V_HD // 2
RET_QK_W = RET_HEADS * RET_QK_HD
RET_CHUNK = 128
RET_GN_EPS = 1e-6

NA_HD = 64
NA_HEADS = MIX_W // NA_HD
NA_WIN_ROWS = 8
NA_WIN_COLS = 16
NA_ROW_BLOCK = 8

N_EXPERTS = 32
TOP_K = 4
D_EXPERT = D_MODEL // 4
SWIGLU_LIMIT = 7.0
SWIGLU_ALPHA = 1.702

ROPE_BASE = 10000.0
LN_EPS = 1e-6

OFF_RWKV = 0
OFF_RET_Q = OFF_RWKV + RWKV_SHIFT_W
OFF_RET_K = OFF_RET_Q + RET_QK_W
OFF_RET_V = OFF_RET_K + RET_QK_W
OFF_RET_G = OFF_RET_V + MIX_W
OFF_NA_Q = OFF_RET_G + MIX_W
OFF_NA_K = OFF_NA_Q + MIX_W
OFF_NA_V = OFF_NA_K + MIX_W
OFF_GATE = OFF_NA_V + MIX_W
D_IN_PROJ = OFF_GATE + N_BRANCH * D_MODEL
IN_PROJ_TN = 512
D_IN_PROJ_PAD = -(-D_IN_PROJ // IN_PROJ_TN) * IN_PROJ_TN

ROW_TILE = 256
LANES = 128
VMEM_LIMIT = 56 * 1024 * 1024


def _cparams(*sem):
    return pltpu.CompilerParams(dimension_semantics=sem, vmem_limit_bytes=VMEM_LIMIT)


def _adaln_kernel(ct_ref, w_ref, b_ref, o_ref, *, n_cond):
    ct = ct_ref[...]
    st = ct * jax.nn.sigmoid(ct)
    w = w_ref[0]
    o_ref[0] = jnp.zeros(o_ref.shape[1:], F32)
    for m in range(n_cond):
        o_ref[0, m:m + 1, :] = jnp.sum(st[:, m:m + 1] * w, axis=0, keepdims=True) + b_ref[0]


def _adaln(cond_t, ada_w, ada_b, n_cond):
    n_layers, d, d6 = ada_w.shape
    tn = 512
    return pl.pallas_call(
        functools.partial(_adaln_kernel, n_cond=n_cond),
        grid=(n_layers, d6 // tn),
        in_specs=[pl.BlockSpec((d, 8), lambda l, j: (0, 0)),
                  pl.BlockSpec((1, d, tn), lambda l, j: (l, 0, j)),
                  pl.BlockSpec((1, 1, tn), lambda l, j: (l, 0, j))],
        out_specs=pl.BlockSpec((1, 8, tn), lambda l, j: (l, 0, j)),
        out_shape=jax.ShapeDtypeStruct((n_layers, 8, d6), F32),
        compiler_params=_cparams("parallel", "parallel"),
        name="adaln",
    )(cond_t, ada_w, ada_b.reshape(n_layers, 1, d6))


def _norm(x, eps):
    mu = jnp.mean(x, axis=-1, keepdims=True)
    xc = x - mu
    var = jnp.mean(xc * xc, axis=-1, keepdims=True)
    return xc * lax.rsqrt(var + eps)


def _top4_gates(logits):
    lane = lax.broadcasted_iota(jnp.int32, logits.shape, 1).astype(F32)
    l = logits
    sel = jnp.zeros(logits.shape, F32)
    m0 = None
    for k in range(TOP_K):
        m = jnp.max(l, axis=1, keepdims=True)
        if k == 0:
            m0 = m
        idx = jnp.min(jnp.where(l == m, lane, float(LANES)), axis=1, keepdims=True)
        hit = lane == idx
        sel = jnp.where(hit, 1.0, sel)
        l = jnp.where(hit, -jnp.inf, l)
    e = jnp.where(sel > 0.0, jnp.exp(logits - m0), 0.0)
    return e / jnp.sum(e, axis=1, keepdims=True)


def _mod_kernel(x_ref, mod_ref, h_ref):
    md = mod_ref[0, 0]
    h = _norm(x_ref[...], LN_EPS) * (1.0 + md[2:3, :]) + md[1:2, :]
    h_ref[...] = h.astype(BF16)


def _resid_kernel(*refs, alpha, router):
    if router:
        x_ref, y_ref, mod_ref, lnw_ref, lnb_ref, rw_ref, rb_ref, xo_ref, h_ref, g_ref = refs
    else:
        x_ref, y_ref, mod_ref, lnw_ref, lnb_ref, xo_ref, h_ref = refs
    md = mod_ref[0, 0]
    z = alpha * x_ref[...] + md[0:1, :] * y_ref[...]
    x1 = _norm(z, LN_EPS) * lnw_ref[...] + lnb_ref[...]
    xo_ref[...] = x1
    h = _norm(x1, LN_EPS) * (1.0 + md[2:3, :]) + md[1:2, :]
    h_ref[...] = h.astype(BF16)
    if router:
        logits = jnp.dot(h, rw_ref[...], preferred_element_type=F32, precision=HI) + rb_ref[...]
        g_ref[...] = _top4_gates(logits)


def _row_specs(n_rows, tiles_per_batch, d):
    row = pl.BlockSpec((ROW_TILE, d), lambda i: (i, 0))
    mod = pl.BlockSpec((1, 1, 8, d), lambda i: (i // tiles_per_batch, jnp.minimum(i % tiles_per_batch, 1), 0, 0))
    vec = pl.BlockSpec((1, d), lambda i: (0, 0))
    return row, mod, vec


def _modulate(x, mod, tiles_per_batch):
    n, d = x.shape
    row, modspec, _ = _row_specs(n, tiles_per_batch, d)
    return pl.pallas_call(
        _mod_kernel, grid=(n // ROW_TILE,),
        in_specs=[row, modspec], out_specs=row,
        out_shape=jax.ShapeDtypeStruct((n, d), BF16),
        compiler_params=_cparams("parallel"), name="modulate",
    )(x, mod)


def _resid_ln(x, y, mod, lnw, lnb, tiles_per_batch, alpha, router_w=None, router_b=None):
    n, d = x.shape
    row, modspec, vec = _row_specs(n, tiles_per_batch, d)
    router = router_w is not None
    in_specs = [row, row, modspec, vec, vec]
    args = [x, y, mod, lnw.reshape(1, d), lnb.reshape(1, d)]
    out_specs = [row, row]
    out_shape = [jax.ShapeDtypeStruct((n, d), F32), jax.ShapeDtypeStruct((n, d), BF16)]
    if router:
        in_specs += [pl.BlockSpec((d, LANES), lambda i: (0, 0)), pl.BlockSpec((1, LANES), lambda i: (0, 0))]
        args += [router_w, router_b]
        out_specs.append(pl.BlockSpec((ROW_TILE, LANES), lambda i: (i, 0)))
        out_shape.append(jax.ShapeDtypeStruct((n, LANES), F32))
    return pl.pallas_call(
        functools.partial(_resid_kernel, alpha=alpha, router=router),
        grid=(n // ROW_TILE,), in_specs=in_specs, out_specs=out_specs, out_shape=out_shape,
        compiler_params=_cparams("parallel"), name="resid_ln",
    )(*args)


def _matmul_kernel(x_ref, w_ref, o_ref):
    o_ref[...] = jnp.dot(x_ref[...], w_ref[...], preferred_element_type=F32).astype(o_ref.dtype)


def _pick_tm(n, cap):
    tm = cap
    while n % tm:
        tm -= 16
    return tm


def _matmul(x, w, tn, out_dtype=F32, tm_cap=1536):
    n, k = x.shape
    _, m = w.shape
    tm = _pick_tm(n, tm_cap)
    return pl.pallas_call(
        _matmul_kernel, grid=(n // tm, m // tn),
        in_specs=[pl.BlockSpec((tm, k), lambda i, j: (i, 0)), pl.BlockSpec((k, tn), lambda i, j: (0, j))],
        out_specs=pl.BlockSpec((tm, tn), lambda i, j: (i, j)),
        out_shape=jax.ShapeDtypeStruct((n, m), out_dtype),
        compiler_params=_cparams("parallel", "parallel"), name="matmul",
    )(x, w)


def _merge_kernel(ya_ref, yb_ref, yn_ref, g0_ref, g1_ref, g2_ref, wb_ref, o_ref):
    acc = None
    for i, (y_ref, g_ref) in enumerate(((ya_ref, g0_ref), (yb_ref, g1_ref), (yn_ref, g2_ref))):
        z = jnp.dot(y_ref[...], wb_ref[i], preferred_element_type=F32)
        t = jax.nn.sigmoid(g_ref[...]) * z
        acc = t if acc is None else acc + t
    o_ref[...] = acc.astype(o_ref.dtype)


def _merge(ya, yb, yn, p, w_branch):
    n, mw = ya.shape
    d = w_branch.shape[-1]
    tn = 256
    tm = _pick_tm(n, 768)
    gate_blk = OFF_GATE // tn
    y_spec = pl.BlockSpec((tm, mw), lambda i, j: (i, 0))
    g_specs = [pl.BlockSpec((tm, tn), functools.partial(lambda i, j, b: (i, gate_blk + b * (d // tn) + j), b=b))
               for b in range(N_BRANCH)]
    return pl.pallas_call(
        _merge_kernel, grid=(n // tm, d // tn),
        in_specs=[y_spec, y_spec, y_spec, *g_specs, pl.BlockSpec((N_BRANCH, mw, tn), lambda i, j: (0, 0, j))],
        out_specs=pl.BlockSpec((tm, tn), lambda i, j: (i, j)),
        out_shape=jax.ShapeDtypeStruct((n, d), BF16),
        compiler_params=_cparams("parallel", "parallel"), name="merge",
    )(ya, yb, yn, p, p, p, w_branch)


def _moe_kernel(h_ref, g_ref, w1_ref, b1_ref, w2_ref, b2_ref, o_ref):
    e = pl.program_id(1)

    @pl.when(e == 0)
    def _():
        o_ref[...] = jnp.zeros_like(o_ref)

    hu = jnp.dot(h_ref[...], w1_ref[0], preferred_element_type=F32) + b1_ref[0]
    f = hu.shape[1] // 2
    g = jnp.minimum(hu[:, :f], SWIGLU_LIMIT)
    u = jnp.clip(hu[:, f:], -SWIGLU_LIMIT, SWIGLU_LIMIT)
    act = g * jax.nn.sigmoid(SWIGLU_ALPHA * g) * (u + 1.0)
    y = jnp.dot(act.astype(BF16), w2_ref[0], preferred_element_type=F32) + b2_ref[0]
    gates = g_ref[...]
    lane = lax.broadcasted_iota(jnp.int32, gates.shape, 1)
    gate = jnp.sum(jnp.where(lane == e, gates, 0.0), axis=1, keepdims=True)
    o_ref[...] += gate * y


def _moe(h, gates, w1, b1, w2, b2):
    n, d = h.shape
    n_exp, _, f2 = w1.shape
    tm = _pick_tm(n, 768)
    return pl.pallas_call(
        _moe_kernel, grid=(n // tm, n_exp),
        in_specs=[pl.BlockSpec((tm, d), lambda i, e: (i, 0)),
                  pl.BlockSpec((tm, LANES), lambda i, e: (i, 0)),
                  pl.BlockSpec((1, d, f2), lambda i, e: (e, 0, 0)),
                  pl.BlockSpec((1, 1, f2), lambda i, e: (e, 0, 0)),
                  pl.BlockSpec((1, f2 // 2, d), lambda i, e: (e, 0, 0)),
                  pl.BlockSpec((1, 1, d), lambda i, e: (e, 0, 0))],
        out_specs=pl.BlockSpec((tm, d), lambda i, e: (i, 0)),
        out_shape=jax.ShapeDtypeStruct((n, d), F32),
        compiler_params=_cparams("parallel", "arbitrary"), name="moe",
    )(h, gates, w1, b1.reshape(n_exp, 1, f2), w2, b2.reshape(n_exp, 1, d))


def _dot_nt(a, b, precision=None):
    return lax.dot_general(a, b, (((1,), (1,)), ((), ())), preferred_element_type=F32, precision=precision)


def _dot_tn(a, b, precision=None):
    return lax.dot_general(a, b, (((0,), (0,)), ((), ())), preferred_element_type=F32, precision=precision)


def _scan_chunk(d, i, n_chunks, n_ctx_chunks):
    rev = jnp.where(i < n_ctx_chunks, n_ctx_chunks - 1 - i, n_chunks - 1 + n_ctx_chunks - i)
    return jnp.where(d == 0, i, rev)


def _head_sum_mats():
    ch = jnp.arange(MIX_W) // RWKV_HD
    e = (ch[:, None] == jnp.arange(LANES)[None, :]).astype(F32)
    return e, e.T


def _rwkv_prep_kernel(p_ref, pp_ref, pn_ref, shift_ref, w0_ref, wb_ref, a0_ref, ab_ref, gb_ref, kk_ref, ka_ref,
                      rk_ref, e_ref, et_ref, r_out, v_out, kk_out, g_out, bonus_out, lw_out, b_out, kd_out,
                      *, tiles_per_batch):
    i = pl.program_id(1)
    x = p_ref[0]
    rows = lax.broadcasted_iota(jnp.int32, (ROW_TILE, 1), 0)
    has_prev = jnp.where(i > 1, 1.0, 0.0)
    has_next = jnp.where((i > 0) & (i < tiles_per_batch - 1), 1.0, 0.0)
    prev_row = pp_ref[0, 7:8, :] * has_prev
    next_row = pn_ref[0, 0:1, :] * has_next
    xp = jnp.where(rows == 0, prev_row, pltpu.roll(x, 1, 0))
    xn = jnp.where(rows == ROW_TILE - 1, next_row, pltpu.roll(x, ROW_TILE - 1, 0))
    z = xp * shift_ref[0:1, :] + x * shift_ref[1:2, :] + xn * shift_ref[2:3, :]
    m = MIX_W
    r, k, v = z[:, 0:m], z[:, m:2 * m], z[:, 2 * m:3 * m]
    o = 3 * m
    wl = z[:, o:o + RWKV_DECAY_RANK]
    al = z[:, o + RWKV_DECAY_RANK:o + RWKV_DECAY_RANK + RWKV_AAA_RANK]
    gl = z[:, o + RWKV_DECAY_RANK + RWKV_AAA_RANK:]
    dot = functools.partial(jnp.dot, preferred_element_type=F32, precision=HI)
    e, et = e_ref[...], et_ref[...]
    g_out[0] = dot(jax.nn.sigmoid(gl), gb_ref[...])
    kk0 = k * kk_ref[...]
    nrm = jnp.maximum(jnp.sqrt(dot(kk0 * kk0, e)), 1e-12)
    kk = kk0 * dot(1.0 / nrm, et)
    tw = jnp.tanh(wl)
    bsum = None
    for d in range(2):
        a =jax.nn.sigmoid(a0_ref[d:d + 1, :] + dot(al, ab_ref[d]))
        kd = k * (1.0 + (a - 1.0) * ka_ref[...])
        lw = (-math.exp(-0.5)) * jax.nn.sigmoid(w0_ref[d:d + 1, :] + dot(tw, wb_ref[d]))
        lw_out[d, 0] = lw
        b_out[d, 0] = kk * a
        kd_out[d, 0] = kd
        t = dot(r * kd * rk_ref[...], e)
        bsum = t if bsum is None else bsum + t
    r_out[0] = r
    v_out[0] = v
    kk_out[0] = kk
    bonus_out[0] = dot(bsum, et) * v


def _rwkv_prep(p3, shift, w0, wb, a0, ab, gb, k_k, k_a, r_k, ctx_len):
    b, tt, _ = p3.shape
    tpb = tt // ROW_TILE
    assert ctx_len == ROW_TILE
    w = RWKV_SHIFT_W
    e, et = _head_sum_mats()
    hb = ROW_TILE // 8
    full = lambda shape: pl.BlockSpec(shape, lambda bb, i: (0,) * len(shape))
    row1 = pl.BlockSpec((1, ROW_TILE, MIX_W), lambda bb, i: (bb, i, 0))
    row2 = pl.BlockSpec((2, 1, ROW_TILE, MIX_W), lambda bb, i: (0, bb, i, 0))
    s1 = jax.ShapeDtypeStruct((b, tt, MIX_W), F32)
    s2 = jax.ShapeDtypeStruct((2, b, tt, MIX_W), F32)
    vec = lambda a: a.reshape(1, MIX_W)
    return pl.pallas_call(
        functools.partial(_rwkv_prep_kernel, tiles_per_batch=tpb), grid=(b, tpb),
        in_specs=[pl.BlockSpec((1, ROW_TILE, w), lambda bb, i: (bb, i, 0)),
                  pl.BlockSpec((1, 8, w), lambda bb, i: (bb, jnp.maximum(i * hb - 1, 0), 0)),
                  pl.BlockSpec((1, 8, w), lambda bb, i: (bb, jnp.minimum((i + 1) * hb, tpb * hb - 1), 0)),
                  full((3, w)), full((2, MIX_W)), full((2, RWKV_DECAY_RANK, MIX_W)), full((2, MIX_W)),
                  full((2, RWKV_AAA_RANK, MIX_W)), full((RWKV_GATE_RANK, MIX_W)),
                  full((1, MIX_W)), full((1, MIX_W)), full((1, MIX_W)), full((MIX_W, LANES)), full((LANES, MIX_W))],
        out_specs=[row1, row1, row1, row1, row1, row2, row2, row2],
        out_shape=[s1, s1, s1, s1, s1, s2, s2, s2],
        compiler_params=_cparams("parallel", "parallel"), name="rwkv_prep",
    )(p3, p3, p3, shift, w0, wb, a0, ab, gb, vec(k_k), vec(k_a), vec(r_k), e, et)


def _rwkv_scan_kernel(r_ref, v_ref, kk_ref, lw_ref, b_ref, kd_ref, y_ref, s_scr):
    d = pl.program_id(0)
    i = pl.program_id(2)
    c = RWKV_CHUNK

    @pl.when(i == 0)
    def _():
        s_scr[...] = jnp.zeros_like(s_scr)

    dot = functools.partial(jnp.dot, preferred_element_type=F32, precision=HI)
    sgn = 1 - 2 * d
    ti = lax.broadcasted_iota(jnp.int32, (c, c), 0)
    si = lax.broadcasted_iota(jnp.int32, (c, c), 1)
    tri = jnp.where((ti - si) * sgn >= 0, 1.0, 0.0)
    lw = lw_ref[0, 0]
    cum = dot(tri, lw)
    e_excl = jnp.exp(cum - lw)
    e_neg = jnp.exp(-cum)
    e_pos = jnp.exp(cum)
    g_tot = jnp.exp(jnp.sum(lw, axis=0, keepdims=True))
    a_t = -kk_ref[0] * e_excl
    b_t = b_ref[0, 0] * e_neg
    k_t = kd_ref[0, 0] * e_neg
    r_t = r_ref[0] * e_pos
    v_all = v_ref[0]

    ri = lax.broadcasted_iota(jnp.int32, (LANES, LANES), 0)
    ci = lax.broadcasted_iota(jnp.int32, (LANES, LANES), 1)
    same = (ri // c) == (ci // c)
    strict = same & ((ri - ci) * sgn > 0)
    incl = same & ((ri - ci) * sgn >= 0)
    lane = lax.broadcasted_iota(jnp.int32, (1, LANES), 1)
    m0 = (lane < RWKV_HD).astype(F32)
    m1 = 1.0 - m0
    stack = lambda x: jnp.concatenate([x * m0, x * m1], axis=0)
    unstack = lambda x: x[:c] + x[c:]

    for p in range(RWKV_HEADS // 2):
        sl = slice(p * LANES, (p + 1) * LANES)
        a_p, b_p, k_p, r_p, v_p = a_t[:, sl], b_t[:, sl], k_t[:, sl], r_t[:, sl], v_all[:, sl]
        a_st, r_st, v_st = stack(a_p), stack(r_p), stack(v_p)
        lhs = jnp.concatenate([a_st, r_st], axis=0)
        rhs = jnp.concatenate([b_p, b_p, k_p, k_p], axis=0)
        gm = _dot_nt(lhs, rhs, HI)
        l_bd = jnp.where(strict, gm[:LANES, :LANES], 0.0)
        m_ak = jnp.where(strict, gm[:LANES, LANES:], 0.0)
        m_rb = jnp.where(incl, gm[LANES:, :LANES], 0.0)
        m_rk = jnp.where(incl, gm[LANES:, LANES:], 0.0)
        x = jnp.concatenate([a_st, dot(m_ak, v_st)], axis=1)
        pw = l_bd
        x = x + dot(pw, x)
        for _ in range(5):
            pw = dot(pw, pw)
            x = x + dot(pw, x)
        w_st, u0_st = x[:, :LANES], x[:, LANES:]
        s_bd = s_scr[p]
        u_st = _dot_nt(w_st, s_bd, HI) + u0_st
        y_st = _dot_nt(r_st, s_bd, HI) + dot(m_rb, u_st) + dot(m_rk, v_st)
        y_ref[0, 0, :, sl] = unstack(y_st)
        uv = jnp.concatenate([unstack(u_st), v_p], axis=0)
        bk = jnp.concatenate([b_p, k_p], axis=0)
        upd = _dot_tn(uv, bk, HI)
        s_scr[p] = g_tot[:, sl] * (s_bd + jnp.where(same, upd, 0.0))


def _rwkv_scan(r, v, kk, lw, bb, kd, ctx_len):
    b, tt, _ = r.shape
    c = RWKV_CHUNK
    n = tt // c
    nc = ctx_len // c
    blk1 = pl.BlockSpec((1, c, MIX_W), lambda d, b_, i: (b_, _scan_chunk(d, i, n, nc), 0))
    blk2 = pl.BlockSpec((1, 1, c, MIX_W), lambda d, b_, i: (d, b_, _scan_chunk(d, i, n, nc), 0))
    return pl.pallas_call(
        _rwkv_scan_kernel, grid=(2, b, n),
        in_specs=[blk1, blk1, blk1, blk2, blk2, blk2], out_specs=blk2,
        out_shape=jax.ShapeDtypeStruct((2, b, tt, MIX_W), F32),
        scratch_shapes=[pltpu.VMEM((RWKV_HEADS // 2, LANES, LANES), F32)],
        compiler_params=_cparams("parallel", "parallel", "arbitrary"), name="rwkv_scan",
    )(r, v, kk, lw, bb, kd)


def _rwkv_finish_kernel(y_ref, bonus_ref, g_ref, w_ref, b_ref, e_ref, et_ref, o_ref):
    dot = functools.partial(jnp.dot, preferred_element_type=F32, precision=HI)
    e, et = e_ref[...], et_ref[...]
    y = y_ref[0, 0] + y_ref[1, 0]
    mu = dot(dot(y, e), et) * (1.0 / RWKV_HD)
    yc = y - mu
    var = dot(dot(yc * yc, e), et) * (1.0 / RWKV_HD)
    yn = yc * lax.rsqrt(var + RWKV_GN_EPS) * w_ref[...] + b_ref[...]
    o_ref[0] = ((yn + bonus_ref[0]) * g_ref[0]).astype(o_ref.dtype)


def _rwkv_finish(y, bonus, g, gn_w, gn_b):
    _, b, tt, _ = y.shape
    e, et = _head_sum_mats()
    row = pl.BlockSpec((1, ROW_TILE, MIX_W), lambda bb, i: (bb, i, 0))
    full = lambda shape: pl.BlockSpec(shape, lambda bb, i: (0,) * len(shape))
    return pl.pallas_call(
        _rwkv_finish_kernel, grid=(b, tt // ROW_TILE),
        in_specs=[pl.BlockSpec((2, 1, ROW_TILE, MIX_W), lambda bb, i: (0, bb, i, 0)), row, row,
                  full((1, MIX_W)), full((1, MIX_W)), full((MIX_W, LANES)), full((LANES, MIX_W))],
        out_specs=row, out_shape=jax.ShapeDtypeStruct((b, tt, MIX_W), BF16),
        compiler_params=_cparams("parallel", "parallel"), name="rwkv_finish",
    )(y, bonus, g, gn_w.reshape(1, MIX_W), gn_b.reshape(1, MIX_W), e, et)


def _rope_swap(x):
    lane = lax.broadcasted_iota(jnp.int32, x.shape, 1)
    return jnp.where((lane % 64) < 32, pltpu.roll(x, 96, 1), pltpu.roll(x, 32, 1))


def _rope_tables(t, ctx_len):
    pos = jnp.arange(t, dtype=jnp.int32)
    nf = RET_QK_HD // 4
    inv = ROPE_BASE ** (-jnp.arange(nf, dtype=F32) / nf)
    ang_r = (pos // GRID_W).astype(F32)[:, None] * inv[None, :]
    ang_c = (pos % GRID_W).astype(F32)[:, None] * inv[None, :]
    cos = jnp.concatenate([jnp.cos(ang_r)] * 2 + [jnp.cos(ang_c)] * 2, axis=-1)
    sin = jnp.concatenate([-jnp.sin(ang_r), jnp.sin(ang_r), -jnp.sin(ang_c), jnp.sin(ang_c)], axis=-1)
    cos = jnp.concatenate([jnp.ones((ctx_len, RET_QK_HD), F32), cos], axis=0)
    sin = jnp.concatenate([jnp.zeros((ctx_len, RET_QK_HD), F32), sin], axis=0)
    return cos, sin


def _ret_kernel(dec_ref, q_ref, k_ref, v_ref, cos_ref, sin_ref, o_ref, r_scr):
    d = pl.program_id(0)
    h = pl.program_id(2)
    i = pl.program_id(3)
    c = RET_CHUNK

    @pl.when(i == 0)
    def _():
        r_scr[...] = jnp.zeros_like(r_scr)

    lg = jnp.log(jax.nn.sigmoid(jnp.full((c, c), dec_ref[d, h], F32)))
    cos, sin = cos_ref[...], sin_ref[...]
    q = q_ref[0]
    k = k_ref[0]
    q = q * cos + _rope_swap(q) * sin
    k = (k * cos + _rope_swap(k) * sin) * (RET_QK_HD ** -0.5)
    v = v_ref[0]
    sign = (1 - 2 * d).astype(F32)
    ii = lax.broadcasted_iota(jnp.int32, (c, c), 0).astype(F32)
    jj = lax.broadcasted_iota(jnp.int32, (c, c), 1).astype(F32)
    pos = jnp.where(d == 0, ii, c - 1.0 - ii)
    diff = (ii - jj) * sign
    dmat = jnp.where(diff >= 0, jnp.exp(lg * jnp.maximum(diff, 0.0)), 0.0)
    q_dec = jnp.exp(lg * (pos + 1.0))
    k_dec = jnp.exp(lg * (c - 1.0 - pos))
    c_dec = jnp.exp(lg[:1, :1] * float(c))
    r = r_scr[...]
    s = _dot_nt(q, k, HI) * dmat
    o = jnp.dot(s, v, preferred_element_type=F32, precision=HI)
    o = o + jnp.dot(q * q_dec, r, preferred_element_type=F32, precision=HI)
    o_ref[0, 0] = o
    r_scr[...] = r * c_dec + _dot_tn(k * k_dec, v, HI)


def _retention(p3, cos, sin, decay, ctx_len):
    b, tt, _ = p3.shape
    c = RET_CHUNK
    n = tt // c
    nc = ctx_len // c
    qb, kb, vb = OFF_RET_Q // RET_QK_HD, OFF_RET_K // RET_QK_HD, OFF_RET_V // RET_V_HD

    def tok(d, bb, h, i, dec):
        return _scan_chunk(d, i, n, nc)

    grid_spec = pltpu.PrefetchScalarGridSpec(
        num_scalar_prefetch=1, grid=(2, b, RET_HEADS, n),
        in_specs=[pl.BlockSpec((1, c, RET_QK_HD), lambda d, bb, h, i, dec: (bb, tok(d, bb, h, i, dec), qb + h)),
                  pl.BlockSpec((1, c, RET_QK_HD), lambda d, bb, h, i, dec: (bb, tok(d, bb, h, i, dec), kb + h)),
                  pl.BlockSpec((1, c, RET_V_HD), lambda d, bb, h, i, dec: (bb, tok(d, bb, h, i, dec), vb + h)),
                  pl.BlockSpec((c, RET_QK_HD), lambda d, bb, h, i, dec: (tok(d, bb, h, i, dec), 0)),
                  pl.BlockSpec((c, RET_QK_HD), lambda d, bb, h, i, dec: (tok(d, bb, h, i, dec), 0))],
        out_specs=pl.BlockSpec((1, 1, c, RET_V_HD), lambda d, bb, h, i, dec: (d, bb, tok(d, bb, h, i, dec), h)),
        scratch_shapes=[pltpu.VMEM((RET_QK_HD, RET_V_HD), F32)])
    return pl.pallas_call(
        _ret_kernel, grid_spec=grid_spec,
        out_shape=jax.ShapeDtypeStruct((2, b, tt, MIX_W), F32),
        compiler_params=_cparams("parallel", "parallel", "parallel", "arbitrary"), name="retention",
    )(decay, p3, p3, p3, cos, sin)


def _ret_finish_kernel(o_ref, g_ref, w_ref, b_ref, y_ref):
    o = o_ref[0, 0] + o_ref[1, 0]
    y = _norm(o, RET_GN_EPS) * w_ref[...] + b_ref[...]
    g = g_ref[0]
    y_ref[0] = (y * (g * jax.nn.sigmoid(g))).astype(y_ref.dtype)


def _ret_finish(o, p3, gn_w, gn_b):
    _, b, tt, _ = o.shape
    gb = OFF_RET_G // RET_V_HD
    return pl.pallas_call(
        _ret_finish_kernel, grid=(b, tt // ROW_TILE, RET_HEADS),
        in_specs=[pl.BlockSpec((2, 1, ROW_TILE, RET_V_HD), lambda bb, i, h: (0, bb, i, h)),
                  pl.BlockSpec((1, ROW_TILE, RET_V_HD), lambda bb, i, h: (bb, i, gb + h)),
                  pl.BlockSpec((1, RET_V_HD), lambda bb, i, h: (0, h)),
                  pl.BlockSpec((1, RET_V_HD), lambda bb, i, h: (0, h))],
        out_specs=pl.BlockSpec((1, ROW_TILE, RET_V_HD), lambda bb, i, h: (bb, i, h)),
        out_shape=jax.ShapeDtypeStruct((b, tt, MIX_W), BF16),
        compiler_params=_cparams("parallel", "parallel", "parallel"), name="ret_finish",
    )(o, p3, gn_w.reshape(1, MIX_W), gn_b.reshape(1, MIX_W))


NA_BLOCK_ROWS = 4
NA_BLOCK_TOK = NA_BLOCK_ROWS * GRID_W
NA_WIN_TOK = NA_WIN_ROWS * GRID_W


def _softmax_pv(s_parts, v_parts):
    m = None
    for s in s_parts:
        mm = jnp.max(s, axis=1, keepdims=True)
        m = mm if m is None else jnp.maximum(m, mm)
    den, acc = None, None
    for s, v in zip(s_parts, v_parts):
        e = jnp.exp(s - m)
        dd = jnp.sum(e, axis=1, keepdims=True)
        pv = jnp.dot(e.astype(BF16), v, preferred_element_type=F32)
        den = dd if den is None else den + dd
        acc = pv if acc is None else acc + pv
    return acc / den


def _na_kernel(q_ref, kp_ref, kc_ref, kn_ref, vp_ref, vc_ref, vn_ref, kx_ref, vx_ref, bias_ref, o_ref,
               k_scr, v_scr, *, n_blocks):
    rb = pl.program_id(2)
    lane = lax.broadcasted_iota(jnp.int32, (1, LANES), 1)
    head_masks = [(lane < NA_HD).astype(F32), (lane >= NA_HD).astype(F32)]
    kx = kx_ref[0].astype(BF16)
    vx = vx_ref[0].astype(BF16)
    scale = NA_HD ** -0.5

    @pl.when(rb == 0)
    def _():
        q = q_ref[0] * scale
        out = jnp.zeros((NA_BLOCK_TOK, LANES), F32)
        for hm in head_masks:
            s = _dot_nt((q * hm).astype(BF16), kx)
            out = out + _softmax_pv([s], [vx]) * hm
        o_ref[0] = out.astype(o_ref.dtype)

    @pl.when(rb > 0)
    def _():
        t = NA_BLOCK_TOK
        k_scr[0:t, :] = kp_ref[0].astype(BF16)
        k_scr[t:2 * t, :] = kc_ref[0].astype(BF16)
        k_scr[2 * t:3 * t, :] = kn_ref[0].astype(BF16)
        v_scr[0:t, :] = vp_ref[0].astype(BF16)
        v_scr[t:2 * t, :] = vc_ref[0].astype(BF16)
        v_scr[2 * t:3 * t, :] = vn_ref[0].astype(BF16)
        first = rb == 1
        last = rb == n_blocks
        for j in range(NA_BLOCK_ROWS):
            off = jnp.where(last, 0, jnp.where(first, NA_BLOCK_ROWS, j))
            oi = jnp.where(last, NA_BLOCK_ROWS + j, jnp.where(first, j, NA_WIN_ROWS // 2))
            start = pl.multiple_of(off * GRID_W, GRID_W)
            kw = k_scr[pl.ds(start, NA_WIN_TOK), :]
            vw = v_scr[pl.ds(start, NA_WIN_TOK), :]
            q = q_ref[0, j * GRID_W:(j + 1) * GRID_W, :] * scale
            out = jnp.zeros((GRID_W, LANES), F32)
            for hh, hm in enumerate(head_masks):
                qh = (q * hm).astype(BF16)
                s_loc = _dot_nt(qh, kw) + bias_ref[oi, hh]
                s_ctx = _dot_nt(qh, kx)
                out = out + _softmax_pv([s_loc, s_ctx], [vw, vx]) * hm
            o_ref[0, j * GRID_W:(j + 1) * GRID_W, :] = out.astype(o_ref.dtype)


def _na_bias_table(rpb):
    col = jnp.arange(GRID_W)
    cs = jnp.clip(col - NA_WIN_COLS // 2, 0, GRID_W - NA_WIN_COLS)
    col_ok = (col[None, :] >= cs[:, None]) & (col[None, :] < cs[:, None] + NA_WIN_COLS)
    col_idx = jnp.clip(col[None, :] - col[:, None] + (NA_WIN_COLS - 1), 0, 2 * NA_WIN_COLS - 2)
    tabs = []
    for oi in range(NA_WIN_ROWS):
        row_idx = jnp.arange(NA_WIN_ROWS) - oi + (NA_WIN_ROWS - 1)
        bias = rpb[:, row_idx][:, :, col_idx]
        bias = jnp.where(col_ok[None, None], bias, -1e30)
        tabs.append(bias.transpose(0, 2, 1, 3).reshape(NA_HEADS, GRID_W, NA_WIN_TOK))
    return jnp.stack(tabs, 0).astype(F32)


def _na(p3, bias_tab, ctx_len):
    b, tt, _ = p3.shape
    assert ctx_len == NA_BLOCK_TOK
    t = tt - ctx_len
    n_rows = t // GRID_W
    assert n_rows % NA_BLOCK_ROWS == 0 and n_rows >= NA_WIN_ROWS
    nb = n_rows // NA_BLOCK_ROWS
    qb, kb, vb = OFF_NA_Q // LANES, OFF_NA_K // LANES, OFF_NA_V // LANES
    blk = (1, NA_BLOCK_TOK, LANES)

    def spec(colb, shift):
        return pl.BlockSpec(blk, lambda pr, bb, rb: (bb, jnp.clip(rb + shift, 1, nb), colb + pr))

    return pl.pallas_call(
        functools.partial(_na_kernel, n_blocks=nb), grid=(NA_HEADS // 2, b, nb + 1),
        in_specs=[pl.BlockSpec(blk, lambda pr, bb, rb: (bb, rb, qb + pr)),
                  spec(kb, -1), spec(kb, 0), spec(kb, 1), spec(vb, -1), spec(vb, 0), spec(vb, 1),
                  pl.BlockSpec(blk, lambda pr, bb, rb: (bb, 0, kb + pr)),
                  pl.BlockSpec(blk, lambda pr, bb, rb: (bb, 0, vb + pr)),
                  pl.BlockSpec((NA_WIN_ROWS, 2, GRID_W, NA_WIN_TOK), lambda pr, bb, rb: (0, pr, 0, 0))],
        out_specs=pl.BlockSpec(blk, lambda pr, bb, rb: (bb, rb, pr)),
        out_shape=jax.ShapeDtypeStruct((b, tt, MIX_W), BF16),
        scratch_shapes=[pltpu.VMEM((3 * NA_BLOCK_TOK, LANES), BF16), pltpu.VMEM((3 * NA_BLOCK_TOK, LANES), BF16)],
        compiler_params=_cparams("parallel", "parallel", "parallel"), name="na",
    )(p3, p3, p3, p3, p3, p3, p3, p3, p3, bias_tab)


def _build_mod(gate, shift, scale, b):
    rows = jnp.stack([gate, shift, scale], axis=1)
    lat = rows[:b]
    ctx = jnp.broadcast_to(rows[b][None], lat.shape)
    mod = jnp.stack([ctx, lat], axis=1)
    return jnp.pad(mod, ((0, 0), (0, 0), (0, 5), (0, 0)))


def kernel(x, c, ctx, c_ctx, ada_w, ada_b, w_in, rwkv_shift, rwkv_w0, rwkv_wB, rwkv_a0, rwkv_aB, rwkv_gB, rwkv_kk, rwkv_ka, rwkv_rk, rwkv_gn_w, rwkv_gn_b, ret_decay, ret_gn_w, ret_gn_b, na_rpb, w_branch, w_out, ln1_w, ln1_b, ln2_w, ln2_b, router_w, router_b, exp_w1, exp_b1, exp_w2, exp_b2):
    b, t, d = x.shape
    ctx_len = ctx.shape[1]
    tt = ctx_len + t
    n = b * tt
    tpb = tt // ROW_TILE
    depth = ada_w.shape[0]
    alpha = (2 * depth) ** 0.25
    assert d == D_MODEL and b + 1 <= 8 and ctx_len == ROW_TILE and t % ROW_TILE == 0

    xa = jnp.concatenate([ctx, x], axis=1).reshape(n, d)
    cond = jnp.concatenate([c, c_ctx[None], jnp.zeros((8 - b - 1, d), F32)], axis=0)
    ada = _adaln(cond.T, ada_w, ada_b, b + 1).reshape(depth, 8, 6, d)
    cos, sin = _rope_tables(t, ctx_len)
    n_exp = router_w.shape[-1]
    rw_pad = jnp.pad(router_w, ((0, 0), (0, 0), (0, LANES - n_exp)))
    rb_pad = jnp.pad(router_b, ((0, 0), (0, LANES - n_exp)), constant_values=-1e30)
    zero = jnp.zeros((8, d), F32)

    h = _modulate(xa, _build_mod(zero, ada[0, :, 0], ada[0, :, 1], b), tpb)
    for l in range(depth):
        w_in_l = jnp.pad(w_in[l].astype(BF16), ((0, 0), (0, D_IN_PROJ_PAD - D_IN_PROJ)))
        p = _matmul(h, w_in_l, IN_PROJ_TN)
        p3 = p.reshape(b, tt, D_IN_PROJ_PAD)

        r, v, kk, g, bonus, lw, bb, kd = _rwkv_prep(p3, rwkv_shift[l], rwkv_w0[l], rwkv_wB[l], rwkv_a0[l], rwkv_aB[l],
                                                    rwkv_gB[l], rwkv_kk[l], rwkv_ka[l], rwkv_rk[l], ctx_len)
        y_scan = _rwkv_scan(r, v, kk, lw, bb, kd, ctx_len)
        ya = _rwkv_finish(y_scan, bonus, g, rwkv_gn_w[l], rwkv_gn_b[l])

        o_ret = _retention(p3, cos, sin, ret_decay[l], ctx_len)
        yb = _ret_finish(o_ret, p3, ret_gn_w[l], ret_gn_b[l])

        yn = _na(p3, _na_bias_table(na_rpb[l]), ctx_len)

        merged = _merge(ya.reshape(n, MIX_W), yb.reshape(n, MIX_W), yn.reshape(n, MIX_W), p, w_branch[l].astype(BF16))
        y = _matmul(merged, w_out[l].astype(BF16), 512)

        mod_f = _build_mod(ada[l, :, 2], ada[l, :, 3], ada[l, :, 4], b)
        x1, h2, gates = _resid_ln(xa, y, mod_f, ln1_w[l], ln1_b[l], tpb, alpha, rw_pad[l], rb_pad[l][None])
        f = _moe(h2, gates, exp_w1[l].astype(BF16), exp_b1[l], exp_w2[l].astype(BF16), exp_b2[l])
        nxt = min(l + 1, depth - 1)
        mod_a = _build_mod(ada[l, :, 5], ada[nxt, :, 0], ada[nxt, :, 1], b)
        xa, h = _resid_ln(x1, f, mod_a, ln2_w[l], ln2_b[l], tpb, alpha)
    return xa.reshape(b, tt, d)[:, ctx_len:]
```

```python
import functools
import math

import jax
import jax.numpy as jnp
from jax import lax
from jax.experimental import pallas as pl
from jax.experimental.pallas import tpu as pltpu

F32 = jnp.float32
BF16 = jnp.bfloat16
HI = lax.Precision.HIGHEST

D_MODEL = 2048
GRID_W = 64
MIX_W = D_MODEL // 2
N_BRANCH = 3

RWKV_HD = 64
RWKV_HEADS = MIX_W // RWKV_HD
RWKV_DECAY_RANK = 64
RWKV_AAA_RANK = 64
RWKV_GATE_RANK = 128
RWKV_SHIFT_W = 3 * MIX_W + RWKV_DECAY_RANK + RWKV_AAA_RANK + RWKV_GATE_RANK
RWKV_GN_EPS = 64e-5
RWKV_CHUNK = 64

RET_HEADS = 4
RET_V_HD = MIX_W // RET_HEADS
RET_QK_HD = RET_V_HD // 2
RET_QK_W = RET_HEADS * RET_QK_HD
RET_CHUNK = 128
RET_GN_EPS = 1e-6

NA_HD = 64
NA_HEADS = MIX_W // NA_HD
NA_WIN_ROWS = 8
NA_WIN_COLS = 16

N_EXPERTS = 32
TOP_K = 4
D_EXPERT = D_MODEL // 4
SWIGLU_LIMIT = 7.0
SWIGLU_ALPHA = 1.702

ROPE_BASE = 10000.0
LN_EPS = 1e-6

OFF_RWKV = 0
RWKV_PAD_W = 4096
OFF_RET_Q = RWKV_PAD_W
OFF_RET_K = OFF_RET_Q + RET_QK_W
OFF_RET_V = OFF_RET_K + RET_QK_W
OFF_RET_G = OFF_RET_V + MIX_W
OFF_NA_Q = OFF_RET_G + MIX_W
OFF_NA_K = OFF_NA_Q + MIX_W
OFF_NA_V = OFF_NA_K + MIX_W
OFF_GATE = OFF_NA_V + MIX_W
D_IN_PROJ_PAD = OFF_GATE + N_BRANCH * D_MODEL
IN_PROJ_TN = 512

ROW_TILE = 256
LANES = 128
VMEM_LIMIT = 56 * 1024 * 1024


def _cparams(*sem):
    return pltpu.CompilerParams(dimension_semantics=sem, vmem_limit_bytes=VMEM_LIMIT)


def _adaln_kernel(ct_ref, w_ref, b_ref, o_ref, *, n_cond):
    ct = ct_ref[...]
    st = ct * jax.nn.sigmoid(ct)
    w = w_ref[0]
    o_ref[0] = jnp.zeros(o_ref.shape[1:], F32)
    for m in range(n_cond):
        o_ref[0, m:m + 1, :] = jnp.sum(st[:, m:m + 1] * w, axis=0, keepdims=True) + b_ref[0]


def _adaln(cond_t, ada_w, ada_b, n_cond):
    n_layers, d, d6 = ada_w.shape
    tn = 512
    return pl.pallas_call(
        functools.partial(_adaln_kernel, n_cond=n_cond),
        grid=(n_layers, d6 // tn),
        in_specs=[pl.BlockSpec((d, 8), lambda l, j: (0, 0)),
                  pl.BlockSpec((1, d, tn), lambda l, j: (l, 0, j)),
                  pl.BlockSpec((1, 1, tn), lambda l, j: (l, 0, j))],
        out_specs=pl.BlockSpec((1, 8, tn), lambda l, j: (l, 0, j)),
        out_shape=jax.ShapeDtypeStruct((n_layers, 8, d6), F32),
        compiler_params=_cparams("parallel", "parallel"),
        name="adaln",
    )(cond_t, ada_w, ada_b.reshape(n_layers, 1, d6))


def _norm(x, eps):
    mu = jnp.mean(x, axis=-1, keepdims=True)
    xc = x - mu
    var = jnp.mean(xc * xc, axis=-1, keepdims=True)
    return xc * lax.rsqrt(var + eps)


def _top4_route(logits):
    lane = lax.broadcasted_iota(jnp.int32, logits.shape, 1).astype(F32)
    l = logits
    idxs, vals = [], []
    for k in range(TOP_K):
        m = jnp.max(l, axis=1, keepdims=True)
        idx = jnp.min(jnp.where(l == m, lane, float(LANES)), axis=1, keepdims=True)
        l = jnp.where(lane == idx, -jnp.inf, l)
        idxs.append(idx)
        vals.append(m)
    es = [jnp.exp(v - vals[0]) for v in vals]
    den = es[0] + es[1] + es[2] + es[3]
    out = jnp.zeros(logits.shape, F32)
    for k in range(TOP_K):
        out = jnp.where(lane == float(k), idxs[k], out)
        out = jnp.where(lane == float(TOP_K + k), es[k] / den, out)
    return out


def _mod_kernel(x_ref, mod_ref, h_ref):
    md = mod_ref[0, 0]
    h = _norm(x_ref[...], LN_EPS) * (1.0 + md[2:3, :]) + md[1:2, :]
    h_ref[...] = h.astype(BF16)


def _resid_kernel(*refs, alpha, router, combine):
    refs = list(refs)
    x_ref, y_ref, mod_ref, lnw_ref, lnb_ref = refs[:5]
    rest = refs[5:]
    if combine:
        route_ref, rest = rest[0], rest[1:]
    if router:
        rw_ref, rb_ref, rest = rest[0], rest[1], rest[2:]
    xo_ref, h_ref = rest[:2]
    md = mod_ref[0, 0]
    if combine:
        d = x_ref.shape[1]
        route = route_ref[...]
        y = None
        for k in range(TOP_K):
            t = route[:, TOP_K + k:TOP_K + k + 1] * y_ref[:, k * d:(k + 1) * d].astype(F32)
            y = t if y is None else y + t
    else:
        y = y_ref[...]
    z = alpha * x_ref[...] + md[0:1, :] * y
    x1 = _norm(z, LN_EPS) * lnw_ref[...] + lnb_ref[...]
    xo_ref[...] = x1
    h = _norm(x1, LN_EPS) * (1.0 + md[2:3, :]) + md[1:2, :]
    h_ref[...] = h.astype(BF16)
    if router:
        logits = jnp.dot(h, rw_ref[...], preferred_element_type=F32, precision=HI) + rb_ref[...]
        rest[2][...] = _top4_route(logits)


def _row_specs(n_rows, tiles_per_batch, d):
    row = pl.BlockSpec((ROW_TILE, d), lambda i: (i, 0))
    mod = pl.BlockSpec((1, 1, 8, d), lambda i: (i // tiles_per_batch, jnp.minimum(i % tiles_per_batch, 1), 0, 0))
    vec = pl.BlockSpec((1, d), lambda i: (0, 0))
    return row, mod, vec


def _modulate(x, mod, tiles_per_batch):
    n, d = x.shape
    row, modspec, _ = _row_specs(n, tiles_per_batch, d)
    return pl.pallas_call(
        _mod_kernel, grid=(n // ROW_TILE,),
        in_specs=[row, modspec], out_specs=row,
        out_shape=jax.ShapeDtypeStruct((n, d), BF16),
        compiler_params=_cparams("parallel"), name="modulate",
    )(x, mod)


def _resid_ln(x, y, mod, lnw, lnb, tiles_per_batch, alpha, router_w=None, router_b=None, route=None):
    n, d = x.shape
    row, modspec, vec = _row_specs(n, tiles_per_batch, d)
    lanes = pl.BlockSpec((ROW_TILE, LANES), lambda i: (i, 0))
    router = router_w is not None
    combine = route is not None
    in_specs = [row, pl.BlockSpec((ROW_TILE, y.shape[1]), lambda i: (i, 0)), modspec, vec, vec]
    args = [x, y, mod, lnw.reshape(1, d), lnb.reshape(1, d)]
    out_specs = [row, row]
    out_shape = [jax.ShapeDtypeStruct((n, d), F32), jax.ShapeDtypeStruct((n, d), BF16)]
    if combine:
        in_specs.append(lanes)
        args.append(route)
    if router:
        in_specs += [pl.BlockSpec((d, LANES), lambda i: (0, 0)), pl.BlockSpec((1, LANES), lambda i: (0, 0))]
        args += [router_w, router_b]
        out_specs.append(lanes)
        out_shape.append(jax.ShapeDtypeStruct((n, LANES), F32))
    return pl.pallas_call(
        functools.partial(_resid_kernel, alpha=alpha, router=router, combine=combine),
        grid=(n // ROW_TILE,), in_specs=in_specs, out_specs=out_specs, out_shape=out_shape,
        compiler_params=_cparams("parallel"), name="resid_ln",
    )(*args)


def _matmul_kernel(x_ref, w_ref, o_ref):
    o_ref[...] = jnp.dot(x_ref[...], w_ref[...], preferred_element_type=F32).astype(o_ref.dtype)


def _pick_tm(n, cap):
    tm = cap
    while n % tm:
        tm -= 16
    return tm


def _matmul(x, w, tn, out_dtype=F32, tm_cap=1536):
    n, k = x.shape
    _, m = w.shape
    tm = _pick_tm(n, tm_cap)
    return pl.pallas_call(
        _matmul_kernel, grid=(n // tm, m // tn),
        in_specs=[pl.BlockSpec((tm, k), lambda i, j: (i, 0)), pl.BlockSpec((k, tn), lambda i, j: (0, j))],
        out_specs=pl.BlockSpec((tm, tn), lambda i, j: (i, j)),
        out_shape=jax.ShapeDtypeStruct((n, m), out_dtype),
        compiler_params=_cparams("parallel", "parallel"), name="matmul",
    )(x, w)


def _merge_kernel(ya_ref, yb_ref, yn_ref, g0_ref, g1_ref, g2_ref, wb_ref, o_ref):
    acc = None
    for i, (y_ref, g_ref) in enumerate(((ya_ref, g0_ref), (yb_ref, g1_ref), (yn_ref, g2_ref))):
        z = jnp.dot(y_ref[...], wb_ref[i], preferred_element_type=F32)
        t = jax.nn.sigmoid(g_ref[...]) * z
        acc = t if acc is None else acc + t
    o_ref[...] = acc.astype(o_ref.dtype)


def _merge(ya, yb, yn, p, w_branch):
    n, mw = ya.shape
    d = w_branch.shape[-1]
    tn = 256
    tm = _pick_tm(n, 768)
    gate_blk = OFF_GATE // tn
    y_spec = pl.BlockSpec((tm, mw), lambda i, j: (i, 0))
    g_specs = [pl.BlockSpec((tm, tn), functools.partial(lambda i, j, b: (i, gate_blk + b * (d // tn) + j), b=b))
               for b in range(N_BRANCH)]
    return pl.pallas_call(
        _merge_kernel, grid=(n // tm, d // tn),
        in_specs=[y_spec, y_spec, y_spec, *g_specs, pl.BlockSpec((N_BRANCH, mw, tn), lambda i, j: (0, 0, j))],
        out_specs=pl.BlockSpec((tm, tn), lambda i, j: (i, j)),
        out_shape=jax.ShapeDtypeStruct((n, d), BF16),
        compiler_params=_cparams("parallel", "parallel"), name="merge",
    )(ya, yb, yn, p, p, p, w_branch)


MOE_TM = 512


def _moe_dispatch(route, n_exp):
    n = route.shape[0]
    n_slots = n * TOP_K
    n_tiles = (n_slots + n_exp * (MOE_TM - 1)) // MOE_TM + 1
    flat = route[:, :TOP_K].astype(jnp.int32).reshape(n_slots)
    order = jnp.argsort(flat, stable=True).astype(jnp.int32)
    e_sorted = flat[order]
    counts = jnp.sum((flat[:, None] == jnp.arange(n_exp, dtype=jnp.int32)[None, :]).astype(jnp.int32), axis=0)
    padded = ((counts + MOE_TM - 1) // MOE_TM) * MOE_TM
    pad_end = jnp.cumsum(padded)
    pad_start = pad_end - padded
    start = jnp.cumsum(counts) - counts
    pos_sorted = pad_start[e_sorted] + (jnp.arange(n_slots, dtype=jnp.int32) - start[e_sorted])
    pos_of_slot = jnp.zeros((n_slots,), jnp.int32).at[order].set(pos_sorted)
    tok_of_pos = jnp.zeros((n_tiles * MOE_TM,), jnp.int32).at[pos_sorted].set(order // TOP_K)
    n_used = (pad_end[-1] // MOE_TM).astype(jnp.int32)
    tile_start = jnp.minimum(jnp.arange(n_tiles, dtype=jnp.int32), n_used - 1) * MOE_TM
    tile_expert = jnp.minimum(jnp.searchsorted(pad_end, tile_start, side="right"), n_exp - 1).astype(jnp.int32)
    return tok_of_pos, pos_of_slot, tile_expert, n_used.reshape(1)


def _moe_kernel(te_ref, nu_ref, x_ref, w1_ref, b1_ref, w2_ref, b2_ref, o_ref):
    t = pl.program_id(0)

    @pl.when(t < nu_ref[0])
    def _():
        hu = jnp.dot(x_ref[...], w1_ref[0], preferred_element_type=F32) + b1_ref[0]
        f = hu.shape[1] // 2
        g = jnp.minimum(hu[:, :f], SWIGLU_LIMIT)
        u = jnp.clip(hu[:, f:], -SWIGLU_LIMIT, SWIGLU_LIMIT)
        act = g * jax.nn.sigmoid(SWIGLU_ALPHA * g) * (u + 1.0)
        y = jnp.dot(act.astype(BF16), w2_ref[0], preferred_element_type=F32) + b2_ref[0]
        o_ref[...] = y.astype(o_ref.dtype)

    @pl.when(t >= nu_ref[0])
    def _():
        o_ref[...] = jnp.zeros_like(o_ref)


def _moe(xs, tile_expert, n_used, w1, b1, w2, b2):
    p, d = xs.shape
    n_exp, _, f2 = w1.shape
    grid_spec = pltpu.PrefetchScalarGridSpec(
        num_scalar_prefetch=2, grid=(p // MOE_TM,),
        in_specs=[pl.BlockSpec((MOE_TM, d), lambda t, te, nu: (t, 0)),
                  pl.BlockSpec((1, d, f2), lambda t, te, nu: (te[t], 0, 0)),
                  pl.BlockSpec((1, 1, f2), lambda t, te, nu: (te[t], 0, 0)),
                  pl.BlockSpec((1, f2 // 2, d), lambda t, te, nu: (te[t], 0, 0)),
                  pl.BlockSpec((1, 1, d), lambda t, te, nu: (te[t], 0, 0))],
        out_specs=pl.BlockSpec((MOE_TM, d), lambda t, te, nu: (t, 0)))
    return pl.pallas_call(
        _moe_kernel, grid_spec=grid_spec,
        out_shape=jax.ShapeDtypeStruct((p, d), BF16),
        compiler_params=_cparams("arbitrary"), name="moe",
    )(tile_expert, n_used, xs, w1, b1.reshape(n_exp, 1, f2), w2, b2.reshape(n_exp, 1, d))


def _dot_nt(a, b, precision=None):
    return lax.dot_general(a, b, (((1,), (1,)), ((), ())), preferred_element_type=F32, precision=precision)


def _dot_tn(a, b, precision=None):
    return lax.dot_general(a, b, (((0,), (0,)), ((), ())), preferred_element_type=F32, precision=precision)


def _scan_chunk(d, i, n_chunks, n_ctx_chunks):
    rev = jnp.where(i < n_ctx_chunks, n_ctx_chunks - 1 - i, n_chunks - 1 + n_ctx_chunks - i)
    return jnp.where(d == 0, i, rev)


def _head_sum_mats():
    ch = jnp.arange(MIX_W) // RWKV_HD
    e = (ch[:, None] == jnp.arange(LANES)[None, :]).astype(BF16)
    return e, e.T


def _dot_split(x, m_bf16):
    hi = x.astype(BF16)
    lo = (x - hi.astype(F32)).astype(BF16)
    return (jnp.dot(hi, m_bf16, preferred_element_type=F32) + jnp.dot(lo, m_bf16, preferred_element_type=F32))


def _rwkv_prep_kernel(p_ref, pp_ref, pn_ref, shift_ref, w0_ref, wb_ref, a0_ref, ab_ref, gb_ref, kk_ref, ka_ref,
                      rk_ref, e_ref, et_ref, r_out, v_out, kk_out, g_out, bonus_out, lw_out, b_out, kd_out,
                      *, tiles_per_batch):
    i = pl.program_id(1)
    x = p_ref[0]
    rows = lax.broadcasted_iota(jnp.int32, (ROW_TILE, 1), 0)
    has_prev = jnp.where(i > 1, 1.0, 0.0)
    has_next = jnp.where((i > 0) & (i < tiles_per_batch - 1), 1.0, 0.0)
    prev_row = pp_ref[0, 7:8, :] * has_prev
    next_row = pn_ref[0, 0:1, :] * has_next
    xp = jnp.where(rows == 0, prev_row, pltpu.roll(x, 1, 0))
    xn = jnp.where(rows == ROW_TILE - 1, next_row, pltpu.roll(x, ROW_TILE - 1, 0))
    z = xp * shift_ref[0:1, :] + x * shift_ref[1:2, :] + xn * shift_ref[2:3, :]
    m = MIX_W
    r, k, v = z[:, 0:m], z[:, m:2 * m], z[:, 2 * m:3 * m]
    o = 3 * m
    wl = z[:, o:o + RWKV_DECAY_RANK]
    al = z[:, o + RWKV_DECAY_RANK:o + RWKV_DECAY_RANK + RWKV_AAA_RANK]
    gl = z[:, o + RWKV_DECAY_RANK + RWKV_AAA_RANK:]
    dot = functools.partial(jnp.dot, preferred_element_type=F32, precision=HI)
    e, et = e_ref[...], et_ref[...]
    g_out[0] = dot(jax.nn.sigmoid(gl), gb_ref[...])
    kk0 = k * kk_ref[...]
    nrm = jnp.maximum(jnp.sqrt(_dot_split(kk0 * kk0, e)), 1e-12)
    kk = kk0 * _dot_split(1.0 / nrm, et)
    tw = jnp.tanh(wl)
    bsum = None
    for d in range(2):
        a = jax.nn.sigmoid(a0_ref[d:d + 1, :] + dot(al, ab_ref[d]))
        kd = k * (1.0 + (a - 1.0) * ka_ref[...])
        lw = (-math.exp(-0.5)) * jax.nn.sigmoid(w0_ref[d:d + 1, :] + dot(tw, wb_ref[d]))
        lw_out[d, 0] = lw
        b_out[d, 0] = kk * a
        kd_out[d, 0] = kd
        t = _dot_split(r * kd * rk_ref[...], e)
        bsum = t if bsum is None else bsum + t
    r_out[0] = r
    v_out[0] = v
    kk_out[0] = kk
    bonus_out[0] = _dot_split(bsum, et) * v


def _rwkv_prep(p3, shift, w0, wb, a0, ab, gb, k_k, k_a, r_k, ctx_len):
    b, tt, _ = p3.shape
    tpb = tt // ROW_TILE
    assert ctx_len == ROW_TILE
    w = RWKV_SHIFT_W
    e, et = _head_sum_mats()
    hb = ROW_TILE // 8
    full = lambda shape: pl.BlockSpec(shape, lambda bb, i: (0,) * len(shape))
    row1 = pl.BlockSpec((1, ROW_TILE, MIX_W), lambda bb, i: (bb, i, 0))
    row2 = pl.BlockSpec((2, 1, ROW_TILE, MIX_W), lambda bb, i: (0, bb, i, 0))
    s1 = jax.ShapeDtypeStruct((b, tt, MIX_W), F32)
    s2 = jax.ShapeDtypeStruct((2, b, tt, MIX_W), F32)
    vec = lambda a: a.reshape(1, MIX_W)
    return pl.pallas_call(
        functools.partial(_rwkv_prep_kernel, tiles_per_batch=tpb), grid=(b, tpb),
        in_specs=[pl.BlockSpec((1, ROW_TILE, w), lambda bb, i: (bb, i, 0)),
                  pl.BlockSpec((1, 8, w), lambda bb, i: (bb, jnp.maximum(i * hb - 1, 0), 0)),
                  pl.BlockSpec((1, 8, w), lambda bb, i: (bb, jnp.minimum((i + 1) * hb, tpb * hb - 1), 0)),
                  full((3, w)), full((2, MIX_W)), full((2, RWKV_DECAY_RANK, MIX_W)), full((2, MIX_W)),
                  full((2, RWKV_AAA_RANK, MIX_W)), full((RWKV_GATE_RANK, MIX_W)),
                  full((1, MIX_W)), full((1, MIX_W)), full((1, MIX_W)), full((MIX_W, LANES)), full((LANES, MIX_W))],
        out_specs=[row1, row1, row1, row1, row1, row2, row2, row2],
        out_shape=[s1, s1, s1, s1, s1, s2, s2, s2],
        compiler_params=_cparams("parallel", "parallel"), name="rwkv_prep",
    )(p3, p3, p3, shift, w0, wb, a0, ab, gb, vec(k_k), vec(k_a), vec(r_k), e, et)


def _rwkv_scan_kernel(r_ref, v_ref, kk_ref, lw_ref, b_ref, kd_ref, y_ref, s_scr):
    d = pl.program_id(0)
    i = pl.program_id(2)
    c = RWKV_CHUNK

    @pl.when(i == 0)
    def _():
        s_scr[...] = jnp.zeros_like(s_scr)

    dot = functools.partial(jnp.dot, preferred_element_type=F32, precision=HI)
    bf = lambda x: x.astype(BF16)
    mm = lambda x, y: jnp.dot(bf(x), bf(y), preferred_element_type=F32)
    sgn = 1 - 2 * d
    ti = lax.broadcasted_iota(jnp.int32, (c, c), 0)
    si = lax.broadcasted_iota(jnp.int32, (c, c), 1)
    tri = jnp.where((ti - si) * sgn >= 0, 1.0, 0.0)
    lw = lw_ref[0, 0]
    cum = dot(tri, lw)
    e_excl = jnp.exp(cum - lw)
    e_neg = jnp.exp(-cum)
    e_pos = jnp.exp(cum)
    g_tot = jnp.exp(jnp.sum(lw, axis=0, keepdims=True))
    a_t = -kk_ref[0] * e_excl
    b_t = b_ref[0, 0] * e_neg
    k_t = kd_ref[0, 0] * e_neg
    r_t = r_ref[0] * e_pos
    v_all = v_ref[0]

    ri = lax.broadcasted_iota(jnp.int32, (LANES, LANES), 0)
    ci = lax.broadcasted_iota(jnp.int32, (LANES, LANES), 1)
    same = (ri // c) == (ci // c)
    strict = same & ((ri - ci) * sgn > 0)
    incl = same & ((ri - ci) * sgn >= 0)
    lane = lax.broadcasted_iota(jnp.int32, (1, LANES), 1)
    m0 = (lane < RWKV_HD).astype(F32)
    m1 = 1.0 - m0
    stack = lambda x: jnp.concatenate([x * m0, x * m1], axis=0)
    unstack = lambda x: x[:c] + x[c:]

    pairs = range(RWKV_HEADS // 2)
    sls = [slice(p * LANES, (p + 1) * LANES) for p in pairs]
    a_st = [bf(stack(a_t[:, sl])) for sl in sls]
    r_st = [bf(stack(r_t[:, sl])) for sl in sls]
    v_st = [bf(stack(v_all[:, sl])) for sl in sls]
    b_p = [bf(b_t[:, sl]) for sl in sls]
    k_p = [bf(k_t[:, sl]) for sl in sls]
    gm = [_dot_nt(jnp.concatenate([a_st[p], r_st[p]], axis=0),
                  jnp.concatenate([b_p[p], b_p[p], k_p[p], k_p[p]], axis=0)) for p in pairs]
    m_ak = [bf(jnp.where(strict, gm[p][:LANES, LANES:], 0.0)) for p in pairs]
    x = [jnp.concatenate([a_st[p].astype(F32), jnp.dot(m_ak[p], v_st[p], preferred_element_type=F32)], axis=1)
         for p in pairs]
    pw = [jnp.where(strict, gm[p][:LANES, :LANES], 0.0) for p in pairs]
    for step in range(6):
        if step:
            pw = [mm(pw[p], pw[p]) for p in pairs]
        x = [x[p] + mm(pw[p], x[p]) for p in pairs]
    s_bd = [s_scr[p] for p in pairs]
    s_bf = [bf(s) for s in s_bd]
    u_st = [_dot_nt(bf(x[p][:, :LANES]), s_bf[p]) + x[p][:, LANES:] for p in pairs]
    m_rb = [bf(jnp.where(incl, gm[p][LANES:, :LANES], 0.0)) for p in pairs]
    m_rk = [bf(jnp.where(incl, gm[p][LANES:, LANES:], 0.0)) for p in pairs]
    for p in pairs:
        y_st = (_dot_nt(r_st[p], s_bf[p]) + jnp.dot(m_rb[p], bf(u_st[p]), preferred_element_type=F32)
                + jnp.dot(m_rk[p], v_st[p], preferred_element_type=F32))
        y_ref[0, 0, :, sls[p]] = unstack(y_st)
    for p in pairs:
        uv = jnp.concatenate([bf(unstack(u_st[p])), bf(v_all[:, sls[p]])], axis=0)
        bk = jnp.concatenate([b_p[p], k_p[p]], axis=0)
        upd = _dot_tn(uv, bk)
        s_scr[p] = g_tot[:, sls[p]] * (s_bd[p] + jnp.where(same, upd, 0.0))


def _rwkv_scan(r, v, kk, lw, bb, kd, ctx_len):
    b, tt, _ = r.shape
    c = RWKV_CHUNK
    n = tt // c
    nc = ctx_len // c
    blk1 = pl.BlockSpec((1, c, MIX_W), lambda d, b_, i: (b_, _scan_chunk(d, i, n, nc), 0))
    blk2 = pl.BlockSpec((1, 1, c, MIX_W), lambda d, b_, i: (d, b_, _scan_chunk(d, i, n, nc), 0))
    return pl.pallas_call(
        _rwkv_scan_kernel, grid=(2, b, n),
        in_specs=[blk1, blk1, blk1, blk2, blk2, blk2], out_specs=blk2,
        out_shape=jax.ShapeDtypeStruct((2, b, tt, MIX_W), F32),
        scratch_shapes=[pltpu.VMEM((RWKV_HEADS // 2, LANES, LANES), F32)],
        compiler_params=_cparams("parallel", "parallel", "arbitrary"), name="rwkv_scan",
    )(r, v, kk, lw, bb, kd)


def _rwkv_finish_kernel(y_ref, bonus_ref, g_ref, w_ref, b_ref, e_ref, et_ref, o_ref):
    e, et = e_ref[...], et_ref[...]
    y = y_ref[0, 0] + y_ref[1, 0]
    mu = _dot_split(_dot_split(y, e), et) * (1.0 / RWKV_HD)
    yc = y - mu
    var = _dot_split(_dot_split(yc * yc, e), et) * (1.0 / RWKV_HD)
    yn = yc * lax.rsqrt(var + RWKV_GN_EPS) * w_ref[...] + b_ref[...]
    o_ref[0] = ((yn + bonus_ref[0]) * g_ref[0]).astype(o_ref.dtype)


def _rwkv_finish(y, bonus, g, gn_w, gn_b):
    _, b, tt, _ = y.shape
    e, et = _head_sum_mats()
    row = pl.BlockSpec((1, ROW_TILE, MIX_W), lambda bb, i: (bb, i, 0))
    full = lambda shape: pl.BlockSpec(shape, lambda bb, i: (0,) * len(shape))
    return pl.pallas_call(
        _rwkv_finish_kernel, grid=(b, tt // ROW_TILE),
        in_specs=[pl.BlockSpec((2, 1, ROW_TILE, MIX_W), lambda bb, i: (0, bb, i, 0)), row, row,
                  full((1, MIX_W)), full((1, MIX_W)), full((MIX_W, LANES)), full((LANES, MIX_W))],
        out_specs=row, out_shape=jax.ShapeDtypeStruct((b, tt, MIX_W), BF16),
        compiler_params=_cparams("parallel", "parallel"), name="rwkv_finish",
    )(y, bonus, g, gn_w.reshape(1, MIX_W), gn_b.reshape(1, MIX_W), e, et)


def _rope_swap(x):
    lane = lax.broadcasted_iota(jnp.int32, x.shape, 1)
    return jnp.where((lane % 64) < 32, pltpu.roll(x, 96, 1), pltpu.roll(x, 32, 1))


def _rope_tables(t, ctx_len):
    pos = jnp.arange(t, dtype=jnp.int32)
    nf = RET_QK_HD // 4
    inv = ROPE_BASE ** (-jnp.arange(nf, dtype=F32) / nf)
    ang_r = (pos // GRID_W).astype(F32)[:, None] * inv[None, :]
    ang_c = (pos % GRID_W).astype(F32)[:, None] * inv[None, :]
    cos = jnp.concatenate([jnp.cos(ang_r)] * 2 + [jnp.cos(ang_c)] * 2, axis=-1)
    sin = jnp.concatenate([-jnp.sin(ang_r), jnp.sin(ang_r), -jnp.sin(ang_c), jnp.sin(ang_c)], axis=-1)
    cos = jnp.concatenate([jnp.ones((ctx_len, RET_QK_HD), F32), cos], axis=0)
    sin = jnp.concatenate([jnp.zeros((ctx_len, RET_QK_HD), F32), sin], axis=0)
    return cos, sin


def _ret_kernel(dec_ref, q_ref, k_ref, v_ref, cos_ref, sin_ref, o_ref, r_scr):
    d = pl.program_id(0)
    i = pl.program_id(2)
    c = RET_CHUNK

    @pl.when(i == 0)
    def _():
        r_scr[...] = jnp.zeros_like(r_scr)

    bf = lambda x: x.astype(BF16)
    cos, sin = cos_ref[...], sin_ref[...]
    sign = (1 - 2 * d).astype(F32)
    ii = lax.broadcasted_iota(jnp.int32, (c, c), 0).astype(F32)
    jj = lax.broadcasted_iota(jnp.int32, (c, c), 1).astype(F32)
    pos = jnp.where(d == 0, ii, c - 1.0 - ii)
    diff = (ii - jj) * sign
    heads = range(RET_HEADS)
    qs = [slice(h * RET_QK_HD, (h + 1) * RET_QK_HD) for h in heads]
    vs = [slice(h * RET_V_HD, (h + 1) * RET_V_HD) for h in heads]
    lg = [jnp.log(jax.nn.sigmoid(jnp.full((c, c), dec_ref[d, h], F32))) for h in heads]
    dmat = [jnp.where(diff >= 0, jnp.exp(lg[h] * jnp.maximum(diff, 0.0)), 0.0) for h in heads]
    q_dec = [jnp.exp(lg[h] * (pos + 1.0)) for h in heads]
    k_dec = [jnp.exp(lg[h] * (c - 1.0 - pos)) for h in heads]
    c_dec = [jnp.exp(lg[h][:1, :1] * float(c)) for h in heads]
    q = [q_ref[0, :, qs[h]] for h in heads]
    k = [k_ref[0, :, qs[h]] for h in heads]
    q = [q[h] * cos + _rope_swap(q[h]) * sin for h in heads]
    k = [(k[h] * cos + _rope_swap(k[h]) * sin) * (RET_QK_HD ** -0.5) for h in heads]
    v = [bf(v_ref[0, :, vs[h]]) for h in heads]
    r = [r_scr[h] for h in heads]
    s = [_dot_nt(bf(q[h]), bf(k[h])) * dmat[h] for h in heads]
    for h in heads:
        o = jnp.dot(bf(s[h]), v[h], preferred_element_type=F32)
        o_ref[0, 0, :, vs[h]] = o + jnp.dot(bf(q[h] * q_dec[h]), bf(r[h]), preferred_element_type=F32)
    for h in heads:
        r_scr[h] = r[h] * c_dec[h] + _dot_tn(bf(k[h] * k_dec[h]), v[h])


def _retention(p3, cos, sin, decay, ctx_len):
    b, tt, _ = p3.shape
    c = RET_CHUNK
    n = tt // c
    nc = ctx_len // c
    qb, kb, vb = OFF_RET_Q // RET_QK_W, OFF_RET_K // RET_QK_W, OFF_RET_V // MIX_W
    tok = lambda d, i: _scan_chunk(d, i, n, nc)
    grid_spec = pltpu.PrefetchScalarGridSpec(
        num_scalar_prefetch=1, grid=(2, b, n),
        in_specs=[pl.BlockSpec((1, c, RET_QK_W), lambda d, bb, i, dec: (bb, tok(d, i), qb)),
                  pl.BlockSpec((1, c, RET_QK_W), lambda d, bb, i, dec: (bb, tok(d, i), kb)),
                  pl.BlockSpec((1, c, MIX_W), lambda d, bb, i, dec: (bb, tok(d, i), vb)),
                  pl.BlockSpec((c, RET_QK_HD), lambda d, bb, i, dec: (tok(d, i), 0)),
                  pl.BlockSpec((c, RET_QK_HD), lambda d, bb, i, dec: (tok(d, i), 0))],
        out_specs=pl.BlockSpec((1, 1, c, MIX_W), lambda d, bb, i, dec: (d, bb, tok(d, i), 0)),
        scratch_shapes=[pltpu.VMEM((RET_HEADS, RET_QK_HD, RET_V_HD), F32)])
    return pl.pallas_call(
        _ret_kernel, grid_spec=grid_spec,
        out_shape=jax.ShapeDtypeStruct((2, b, tt, MIX_W), F32),
        compiler_params=_cparams("parallel", "parallel", "arbitrary"), name="retention",
    )(decay, p3, p3, p3, cos, sin)


def _ret_finish_kernel(o_ref, g_ref, w_ref, b_ref, y_ref):
    o = o_ref[0, 0] + o_ref[1, 0]
    y = _norm(o, RET_GN_EPS) * w_ref[...] + b_ref[...]
    g = g_ref[0]
    y_ref[0] = (y * (g * jax.nn.sigmoid(g))).astype(y_ref.dtype)


def _ret_finish(o, p3, gn_w, gn_b):
    _, b, tt, _ = o.shape
    gb = OFF_RET_G // RET_V_HD
    return pl.pallas_call(
        _ret_finish_kernel, grid=(b, tt // ROW_TILE, RET_HEADS),
        in_specs=[pl.BlockSpec((2, 1, ROW_TILE, RET_V_HD), lambda bb, i, h: (0, bb, i, h)),
                  pl.BlockSpec((1, ROW_TILE, RET_V_HD), lambda bb, i, h: (bb, i, gb + h)),
                  pl.BlockSpec((1, RET_V_HD), lambda bb, i, h: (0, h)),
                  pl.BlockSpec((1, RET_V_HD), lambda bb, i, h: (0, h))],
        out_specs=pl.BlockSpec((1, ROW_TILE, RET_V_HD), lambda bb, i, h: (bb, i, h)),
        out_shape=jax.ShapeDtypeStruct((b, tt, MIX_W), BF16),
        compiler_params=_cparams("parallel", "parallel", "parallel"), name="ret_finish",
    )(o, p3, gn_w.reshape(1, MIX_W), gn_b.reshape(1, MIX_W))


NA_BLOCK_ROWS = 4
NA_BLOCK_TOK = NA_BLOCK_ROWS * GRID_W
NA_WIN_TOK = NA_WIN_ROWS * GRID_W


def _softmax_pv(s_parts, v_parts):
    m = None
    for s in s_parts:
        mm = jnp.max(s, axis=1, keepdims=True)
        m = mm if m is None else jnp.maximum(m, mm)
    den, acc = None, None
    for s, v in zip(s_parts, v_parts):
        e = jnp.exp(s - m)
        dd = jnp.sum(e, axis=1, keepdims=True)
        pv = jnp.dot(e.astype(BF16), v, preferred_element_type=F32)
        den = dd if den is None else den + dd
        acc = pv if acc is None else acc + pv
    return acc / den


def _na_kernel(q_ref, kp_ref, kc_ref, kn_ref, vp_ref, vc_ref, vn_ref, kx_ref, vx_ref, bias_ref, o_ref,
               k_scr, v_scr, *, n_blocks):
    rb = pl.program_id(2)
    lane = lax.broadcasted_iota(jnp.int32, (1, LANES), 1)
    head_masks = [(lane < NA_HD).astype(F32), (lane >= NA_HD).astype(F32)]
    kx = kx_ref[0].astype(BF16)
    vx = vx_ref[0].astype(BF16)
    scale = NA_HD ** -0.5

    @pl.when(rb == 0)
    def _():
        q = q_ref[0] * scale
        out = jnp.zeros((NA_BLOCK_TOK, LANES), F32)
        for hm in head_masks:
            s = _dot_nt((q * hm).astype(BF16), kx)
            out = out + _softmax_pv([s], [vx]) * hm
        o_ref[0] = out.astype(o_ref.dtype)

    @pl.when(rb > 0)
    def _():
        t = NA_BLOCK_TOK
        k_scr[0:t, :] = kp_ref[0].astype(BF16)
        k_scr[t:2 * t, :] = kc_ref[0].astype(BF16)
        k_scr[2 * t:3 * t, :] = kn_ref[0].astype(BF16)
        v_scr[0:t, :] = vp_ref[0].astype(BF16)
        v_scr[t:2 * t, :] = vc_ref[0].astype(BF16)
        v_scr[2 * t:3 * t, :] = vn_ref[0].astype(BF16)
        first = rb == 1
        last = rb == n_blocks
        rows = range(NA_BLOCK_ROWS)
        heads = range(2)
        rs = [slice(j * GRID_W, (j + 1) * GRID_W) for j in rows]
        kw, vw, oi = [], [], []
        for j in rows:
            off = jnp.where(last, 0, jnp.where(first, NA_BLOCK_ROWS, j))
            oi.append(jnp.where(last, NA_BLOCK_ROWS + j, jnp.where(first, j, NA_WIN_ROWS // 2)))
            start = pl.multiple_of(off * GRID_W, GRID_W)
            kw.append(k_scr[pl.ds(start, NA_WIN_TOK), :])
            vw.append(v_scr[pl.ds(start, NA_WIN_TOK), :])
        q = q_ref[0] * scale
        qh = [(q * hm).astype(BF16) for hm in head_masks]
        s_ctx = [_dot_nt(qh[h], kx) for h in heads]
        s_loc = [[_dot_nt(qh[h][rs[j]], kw[j]) + bias_ref[oi[j], h] for h in heads] for j in rows]
        m_ctx = [jnp.max(s_ctx[h], axis=1, keepdims=True) for h in heads]
        m = [[jnp.maximum(jnp.max(s_loc[j][h], axis=1, keepdims=True), m_ctx[h][rs[j]]) for h in heads] for j in rows]
        m_all = [jnp.concatenate([m[j][h] for j in rows], axis=0) for h in heads]
        e_ctx = [jnp.exp(s_ctx[h] - m_all[h]) for h in heads]
        e_loc = [[jnp.exp(s_loc[j][h] - m[j][h]) for h in heads] for j in rows]
        d_ctx = [jnp.sum(e_ctx[h], axis=1, keepdims=True) for h in heads]
        pv_ctx = [jnp.dot(e_ctx[h].astype(BF16), vx, preferred_element_type=F32) for h in heads]
        for j in rows:
            out = None
            for h in heads:
                den = jnp.sum(e_loc[j][h], axis=1, keepdims=True) + d_ctx[h][rs[j]]
                pv = jnp.dot(e_loc[j][h].astype(BF16), vw[j], preferred_element_type=F32) + pv_ctx[h][rs[j]]
                term = pv / den * head_masks[h]
                out = term if out is None else out + term
            o_ref[0, rs[j], :] = out.astype(o_ref.dtype)


def _na_bias_table(rpb):
    col = jnp.arange(GRID_W)
    cs = jnp.clip(col - NA_WIN_COLS // 2, 0, GRID_W - NA_WIN_COLS)
    col_ok = (col[None, :] >= cs[:, None]) & (col[None, :] < cs[:, None] + NA_WIN_COLS)
    col_idx = jnp.clip(col[None, :] - col[:, None] + (NA_WIN_COLS - 1), 0, 2 * NA_WIN_COLS - 2)
    tabs = []
    for oi in range(NA_WIN_ROWS):
        row_idx = jnp.arange(NA_WIN_ROWS) - oi + (NA_WIN_ROWS - 1)
        bias = rpb[:, row_idx][:, :, col_idx]
        bias = jnp.where(col_ok[None, None], bias, -1e30)
        tabs.append(bias.transpose(0, 2, 1, 3).reshape(NA_HEADS, GRID_W, NA_WIN_TOK))
    return jnp.stack(tabs, 0).astype(F32)


def _na(p3, bias_tab, ctx_len):
    b, tt, _ = p3.shape
    assert ctx_len == NA_BLOCK_TOK
    t = tt - ctx_len
    n_rows = t // GRID_W
    assert n_rows % NA_BLOCK_ROWS == 0 and n_rows >= NA_WIN_ROWS
    nb = n_rows // NA_BLOCK_ROWS
    qb, kb, vb = OFF_NA_Q // LANES, OFF_NA_K // LANES, OFF_NA_V // LANES
    blk = (1, NA_BLOCK_TOK, LANES)

    def spec(colb, shift):
        return pl.BlockSpec(blk, lambda pr, bb, rb: (bb, jnp.clip(rb + shift, 1, nb), colb + pr))

    return pl.pallas_call(
        functools.partial(_na_kernel, n_blocks=nb), grid=(NA_HEADS // 2, b, nb + 1),
        in_specs=[pl.BlockSpec(blk, lambda pr, bb, rb: (bb, rb, qb + pr)),
                  spec(kb, -1), spec(kb, 0), spec(kb, 1), spec(vb, -1), spec(vb, 0), spec(vb, 1),
                  pl.BlockSpec(blk, lambda pr, bb, rb: (bb, 0, kb + pr)),
                  pl.BlockSpec(blk, lambda pr, bb, rb: (bb, 0, vb + pr)),
                  pl.BlockSpec((NA_WIN_ROWS, 2, GRID_W, NA_WIN_TOK), lambda pr, bb, rb: (0, pr, 0, 0))],
        out_specs=pl.BlockSpec(blk, lambda pr, bb, rb: (bb, rb, pr)),
        out_shape=jax.ShapeDtypeStruct((b, tt, MIX_W), BF16),
        scratch_shapes=[pltpu.VMEM((3 * NA_BLOCK_TOK, LANES), BF16), pltpu.VMEM((3 * NA_BLOCK_TOK, LANES), BF16)],
        compiler_params=_cparams("parallel", "parallel", "parallel"), name="na",
    )(p3, p3, p3, p3, p3, p3, p3, p3, p3, bias_tab)


def _build_mod(gate, shift, scale, b):
    rows = jnp.stack([gate, shift, scale], axis=1)
    lat = rows[:b]
    ctx = jnp.broadcast_to(rows[b][None], lat.shape)
    mod = jnp.stack([ctx, lat], axis=1)
    return jnp.pad(mod, ((0, 0), (0, 0), (0, 5), (0, 0)))


def kernel(x, c, ctx, c_ctx, ada_w, ada_b, w_in, rwkv_shift, rwkv_w0, rwkv_wB, rwkv_a0, rwkv_aB, rwkv_gB, rwkv_kk, rwkv_ka, rwkv_rk, rwkv_gn_w, rwkv_gn_b, ret_decay, ret_gn_w, ret_gn_b, na_rpb, w_branch, w_out, ln1_w, ln1_b, ln2_w, ln2_b, router_w, router_b, exp_w1, exp_b1, exp_w2, exp_b2):
    b, t, d = x.shape
    ctx_len = ctx.shape[1]
    tt = ctx_len + t
    n = b * tt
    tpb = tt // ROW_TILE
    depth = ada_w.shape[0]
    alpha = (2 * depth) ** 0.25
    assert d == D_MODEL and b + 1 <= 8 and ctx_len == ROW_TILE and t % ROW_TILE == 0

    xa = jnp.concatenate([ctx, x], axis=1).reshape(n, d)
    cond = jnp.concatenate([c, c_ctx[None], jnp.zeros((8 - b - 1, d), F32)], axis=0)
    ada = _adaln(cond.T, ada_w, ada_b, b + 1).reshape(depth, 8, 6, d)
    cos, sin = _rope_tables(t, ctx_len)
    n_exp = router_w.shape[-1]
    rw_pad = jnp.pad(router_w, ((0, 0), (0, 0), (0, LANES - n_exp)))
    rb_pad = jnp.pad(router_b, ((0, 0), (0, LANES - n_exp)), constant_values=-1e30)
    zero = jnp.zeros((8, d), F32)

    h = _modulate(xa, _build_mod(zero, ada[0, :, 0], ada[0, :, 1], b), tpb)
    for l in range(depth):
        w_l = w_in[l].astype(BF16)
        w_in_l = jnp.concatenate([w_l[:, :RWKV_SHIFT_W], jnp.zeros((d, RWKV_PAD_W - RWKV_SHIFT_W), BF16),
                                  w_l[:, RWKV_SHIFT_W:]], axis=1)
        p = _matmul(h, w_in_l, IN_PROJ_TN)
        p3 = p.reshape(b, tt, D_IN_PROJ_PAD)

        r, v, kk, g, bonus, lw, bb, kd = _rwkv_prep(p3, rwkv_shift[l], rwkv_w0[l], rwkv_wB[l], rwkv_a0[l], rwkv_aB[l],
                                                    rwkv_gB[l], rwkv_kk[l], rwkv_ka[l], rwkv_rk[l], ctx_len)
        y_scan = _rwkv_scan(r, v, kk, lw, bb, kd, ctx_len)
        ya = _rwkv_finish(y_scan, bonus, g, rwkv_gn_w[l], rwkv_gn_b[l])

        o_ret = _retention(p3, cos, sin, ret_decay[l], ctx_len)
        yb = _ret_finish(o_ret, p3, ret_gn_w[l], ret_gn_b[l])

        yn = _na(p3, _na_bias_table(na_rpb[l]), ctx_len)

        merged = _merge(ya.reshape(n, MIX_W), yb.reshape(n, MIX_W), yn.reshape(n, MIX_W), p, w_branch[l].astype(BF16))
        y = _matmul(merged, w_out[l].astype(BF16), 512)

        mod_f = _build_mod(ada[l, :, 2], ada[l, :, 3], ada[l, :, 4], b)
        x1, h2, route = _resid_ln(xa, y, mod_f, ln1_w[l], ln1_b[l], tpb, alpha, rw_pad[l], rb_pad[l][None])
        tok_of_pos, pos_of_slot, tile_expert, n_used = _moe_dispatch(route, n_exp)
        xs = jnp.take(h2, tok_of_pos, axis=0, mode="clip")
        ys = _moe(xs, tile_expert, n_used, exp_w1[l].astype(BF16), exp_b1[l], exp_w2[l].astype(BF16), exp_b2[l])
        f4 = jnp.take(ys, pos_of_slot, axis=0, mode="clip").reshape(n, TOP_K * d)
        nxt = min(l + 1, depth - 1)
        mod_a = _build_mod(ada[l, :, 5], ada[nxt, :, 0], ada[nxt, :, 1], b)
        xa, h = _resid_ln(x1, f4, mod_a, ln2_w[l], ln2_b[l], tpb, alpha, route=route)
    return xa.reshape(b, tt, d)[:, ctx_len:]
```

```python
import functools
import math

import jax
import jax.numpy as jnp
from jax import lax
from jax.experimental import pallas as pl
from jax.experimental.pallas import tpu as pltpu

F32 = jnp.float32
BF16 = jnp.bfloat16
HI = lax.Precision.HIGHEST

D_MODEL = 2048
GRID_W = 64
MIX_W = D_MODEL // 2
N_BRANCH = 3

RWKV_HD = 64
RWKV_HEADS = MIX_W // RWKV_HD
RWKV_DECAY_RANK = 64
RWKV_AAA_RANK = 64
RWKV_GATE_RANK = 128
RWKV_SHIFT_W = 3 * MIX_W + RWKV_DECAY_RANK + RWKV_AAA_RANK + RWKV_GATE_RANK
RWKV_GN_EPS = 64e-5
RWKV_CHUNK = 64

RET_HEADS = 4
RET_V_HD = MIX_W // RET_HEADS
RET_QK_HD = RET_V_HD // 2
RET_QK_W = RET_HEADS * RET_QK_HD
RET_CHUNK = 128
RET_GN_EPS = 1e-6

NA_HD = 64
NA_HEADS = MIX_W // NA_HD
NA_WIN_ROWS = 8
NA_WIN_COLS = 16

N_EXPERTS = 32
TOP_K = 4
D_EXPERT = D_MODEL // 4
SWIGLU_LIMIT = 7.0
SWIGLU_ALPHA = 1.702

ROPE_BASE = 10000.0
LN_EPS = 1e-6

IN_PROJ_TN = 256
D_IN_PROJ = RWKV_SHIFT_W + 2 * RET_QK_W + 5 * MIX_W + N_BRANCH * D_MODEL
IN_PROJ_ROT = RWKV_SHIFT_W // IN_PROJ_TN
OFF_RET_Q = 0
OFF_RET_K = OFF_RET_Q + RET_QK_W
OFF_RET_V = OFF_RET_K + RET_QK_W
OFF_RET_G = OFF_RET_V + MIX_W
OFF_NA_Q = OFF_RET_G + MIX_W
OFF_NA_K = OFF_NA_Q + MIX_W
OFF_NA_V = OFF_NA_K + MIX_W
OFF_GATE = OFF_NA_V + MIX_W
OFF_RWKV = OFF_GATE + N_BRANCH * D_MODEL
OFF_RWKV_LORA = OFF_RWKV + 3 * MIX_W
RWKV_LORA_W = RWKV_SHIFT_W - 3 * MIX_W

ROW_TILE = 256
LANES = 128
VMEM_LIMIT = 56 * 1024 * 1024


def _cparams(*sem):
    return pltpu.CompilerParams(dimension_semantics=sem, vmem_limit_bytes=VMEM_LIMIT)


def _adaln_kernel(ct_ref, w_ref, b_ref, o_ref, *, n_cond):
    ct = ct_ref[...]
    st = ct * jax.nn.sigmoid(ct)
    w = w_ref[0]
    o_ref[0] = jnp.zeros(o_ref.shape[1:], F32)
    for m in range(n_cond):
        o_ref[0, m:m + 1, :] = jnp.sum(st[:, m:m + 1] * w, axis=0, keepdims=True) + b_ref[0]


def _adaln(cond_t, ada_w, ada_b, n_cond):
    n_layers, d, d6 = ada_w.shape
    tn = 512
    return pl.pallas_call(
        functools.partial(_adaln_kernel, n_cond=n_cond),
        grid=(n_layers, d6 // tn),
        in_specs=[pl.BlockSpec((d, 8), lambda l, j: (0, 0)),
                  pl.BlockSpec((1, d, tn), lambda l, j: (l, 0, j)),
                  pl.BlockSpec((1, 1, tn), lambda l, j: (l, 0, j))],
        out_specs=pl.BlockSpec((1, 8, tn), lambda l, j: (l, 0, j)),
        out_shape=jax.ShapeDtypeStruct((n_layers, 8, d6), F32),
        compiler_params=_cparams("parallel", "parallel"),
        name="adaln",
    )(cond_t, ada_w, ada_b.reshape(n_layers, 1, d6))


def _norm(x, eps):
    mu = jnp.mean(x, axis=-1, keepdims=True)
    xc = x - mu
    var = jnp.mean(xc * xc, axis=-1, keepdims=True)
    return xc * lax.rsqrt(var + eps)


def _top4_route(logits):
    lane = lax.broadcasted_iota(jnp.int32, logits.shape, 1).astype(F32)
    l = logits
    idxs, vals = [], []
    for k in range(TOP_K):
        m = jnp.max(l, axis=1, keepdims=True)
        idx = jnp.min(jnp.where(l == m, lane, float(LANES)), axis=1, keepdims=True)
        l = jnp.where(lane == idx, -jnp.inf, l)
        idxs.append(idx)
        vals.append(m)
    es = [jnp.exp(v - vals[0]) for v in vals]
    den = es[0] + es[1] + es[2] + es[3]
    out = jnp.zeros(logits.shape, F32)
    for k in range(TOP_K):
        out = jnp.where(lane == float(k), idxs[k], out)
        out = jnp.where(lane == float(TOP_K + k), es[k] / den, out)
    return out


def _mod_kernel(x_ref, mod_ref, h_ref):
    md = mod_ref[0, 0]
    h = _norm(x_ref[...], LN_EPS) * (1.0 + md[2:3, :]) + md[1:2, :]
    h_ref[...] = h.astype(BF16)


def _resid_kernel(*refs, alpha, router, combine):
    refs = list(refs)
    n_y = TOP_K if combine else 1
    x_ref, y_refs = refs[0], refs[1:1 + n_y]
    mod_ref, lnw_ref, lnb_ref = refs[1 + n_y:4 + n_y]
    rest = refs[4 + n_y:]
    if combine:
        route_ref, rest = rest[0], rest[1:]
    if router:
        rw_ref, rb_ref, rest = rest[0], rest[1], rest[2:]
    xo_ref, h_ref = rest[:2]
    md = mod_ref[0, 0]
    if combine:
        route = route_ref[...]
        y = None
        for k in range(TOP_K):
            t = route[:, TOP_K + k:TOP_K + k + 1] * y_refs[k][...].astype(F32)
            y = t if y is None else y + t
    else:
        y = y_refs[0][...]
    z = alpha * x_ref[...] + md[0:1, :] * y
    x1 = _norm(z, LN_EPS) * lnw_ref[...] + lnb_ref[...]
    xo_ref[...] = x1
    h = _norm(x1, LN_EPS) * (1.0 + md[2:3, :]) + md[1:2, :]
    h_ref[...] = h.astype(BF16)
    if router:
        logits = jnp.dot(h, rw_ref[...], preferred_element_type=F32, precision=HI) + rb_ref[...]
        rest[2][...] = _top4_route(logits)


def _row_specs(n_rows, tiles_per_batch, d):
    row = pl.BlockSpec((ROW_TILE, d), lambda i: (i, 0))
    mod = pl.BlockSpec((1, 1, 8, d), lambda i: (i // tiles_per_batch, jnp.minimum(i % tiles_per_batch, 1), 0, 0))
    vec = pl.BlockSpec((1, d), lambda i: (0, 0))
    return row, mod, vec


def _modulate(x, mod, tiles_per_batch):
    n, d = x.shape
    row, modspec, _ = _row_specs(n, tiles_per_batch, d)
    return pl.pallas_call(
        _mod_kernel, grid=(n // ROW_TILE,),
        in_specs=[row, modspec], out_specs=row,
        out_shape=jax.ShapeDtypeStruct((n, d), BF16),
        compiler_params=_cparams("parallel"), name="modulate",
    )(x, mod)


def _resid_ln(x, y, mod, lnw, lnb, tiles_per_batch, alpha, router_w=None, router_b=None, route=None):
    n, d = x.shape
    row, modspec, vec = _row_specs(n, tiles_per_batch, d)
    lanes = pl.BlockSpec((ROW_TILE, LANES), lambda i: (i, 0))
    router = router_w is not None
    combine = route is not None
    ys = list(y) if combine else [y]
    in_specs = [row] + [row] * len(ys) + [modspec, vec, vec]
    args = [x, *ys, mod, lnw.reshape(1, d), lnb.reshape(1, d)]
    out_specs = [row, row]
    out_shape = [jax.ShapeDtypeStruct((n, d), F32), jax.ShapeDtypeStruct((n, d), BF16)]
    if combine:
        in_specs.append(lanes)
        args.append(route)
    if router:
        in_specs += [pl.BlockSpec((d, LANES), lambda i: (0, 0)), pl.BlockSpec((1, LANES), lambda i: (0, 0))]
        args += [router_w, router_b]
        out_specs.append(lanes)
        out_shape.append(jax.ShapeDtypeStruct((n, LANES), F32))
    return pl.pallas_call(
        functools.partial(_resid_kernel, alpha=alpha, router=router, combine=combine),
        grid=(n // ROW_TILE,), in_specs=in_specs, out_specs=out_specs, out_shape=out_shape,
        compiler_params=_cparams("parallel"), name="resid_ln",
    )(*args)


def _matmul_kernel(x_ref, w_ref, o_ref):
    o_ref[...] = jnp.dot(x_ref[...], w_ref[...], preferred_element_type=F32).astype(o_ref.dtype)


def _pick_tm(n, cap):
    tm = cap
    while n % tm:
        tm -= 16
    return tm


def _matmul(x, w, tn, out_dtype=F32, tm_cap=1536):
    n, k = x.shape
    _, m = w.shape
    tm = _pick_tm(n, tm_cap)
    return pl.pallas_call(
        _matmul_kernel, grid=(n // tm, m // tn),
        in_specs=[pl.BlockSpec((tm, k), lambda i, j: (i, 0)), pl.BlockSpec((k, tn), lambda i, j: (0, j))],
        out_specs=pl.BlockSpec((tm, tn), lambda i, j: (i, j)),
        out_shape=jax.ShapeDtypeStruct((n, m), out_dtype),
        compiler_params=_cparams("parallel", "parallel"), name="matmul",
    )(x, w)


def _in_proj_kernel(x_ref, w_ref, o_ref):
    o_ref[...] = jnp.dot(x_ref[...], w_ref[...].astype(BF16), preferred_element_type=F32)


def _in_proj(x, w):
    n, k = x.shape
    tm = _pick_tm(n, 1536)
    nblk = D_IN_PROJ // IN_PROJ_TN
    return pl.pallas_call(
        _in_proj_kernel, grid=(n // tm, nblk),
        in_specs=[pl.BlockSpec((tm, k), lambda i, j: (i, 0)),
                  pl.BlockSpec((k, IN_PROJ_TN), lambda i, j: (0, (j + IN_PROJ_ROT) % nblk))],
        out_specs=pl.BlockSpec((tm, IN_PROJ_TN), lambda i, j: (i, j)),
        out_shape=jax.ShapeDtypeStruct((n, D_IN_PROJ), F32),
        compiler_params=_cparams("parallel", "parallel"), name="in_proj",
    )(x, w)


def _merge_kernel(ya_ref, yb_ref, yn_ref, g0_ref, g1_ref, g2_ref, wb_ref, o_ref):
    acc = None
    for i, (y_ref, g_ref) in enumerate(((ya_ref, g0_ref), (yb_ref, g1_ref), (yn_ref, g2_ref))):
        z = jnp.dot(y_ref[...], wb_ref[i], preferred_element_type=F32)
        t = jax.nn.sigmoid(g_ref[...]) * z
        acc = t if acc is None else acc + t
    o_ref[...] = acc.astype(o_ref.dtype)


def _merge(ya, yb, yn, p, w_branch):
    n, mw = ya.shape
    d = w_branch.shape[-1]
    tn = 256
    tm = _pick_tm(n, 768)
    gate_blk = OFF_GATE // tn
    y_spec = pl.BlockSpec((tm, mw), lambda i, j: (i, 0))
    g_specs = [pl.BlockSpec((tm, tn), functools.partial(lambda i, j, b: (i, gate_blk + b * (d // tn) + j), b=b))
               for b in range(N_BRANCH)]
    return pl.pallas_call(
        _merge_kernel, grid=(n // tm, d // tn),
        in_specs=[y_spec, y_spec, y_spec, *g_specs, pl.BlockSpec((N_BRANCH, mw, tn), lambda i, j: (0, 0, j))],
        out_specs=pl.BlockSpec((tm, tn), lambda i, j: (i, j)),
        out_shape=jax.ShapeDtypeStruct((n, d), BF16),
        compiler_params=_cparams("parallel", "parallel"), name="merge",
    )(ya, yb, yn, p, p, p, w_branch)


MOE_TM = 512


def _moe_dispatch(route, n_exp):
    n = route.shape[0]
    n_slots = n * TOP_K
    n_tiles = (n_slots + n_exp * (MOE_TM - 1)) // MOE_TM + 1
    i32 = jnp.int32
    experts = jnp.arange(n_exp, dtype=i32)
    flat = route[:, :TOP_K].astype(i32).reshape(n_slots)
    order = jnp.argsort(flat, stable=True).astype(i32)
    rank = jnp.argsort(order).astype(i32)
    onehot = (flat[:, None] == experts[None, :]).astype(i32)
    counts = jnp.sum(onehot, axis=0)
    padded = ((counts + MOE_TM - 1) // MOE_TM) * MOE_TM
    pad_end = jnp.cumsum(padded)
    pad_start = pad_end - padded
    start = jnp.cumsum(counts) - counts
    pos_of_slot = jnp.sum(onehot * (pad_start - start)[None, :], axis=1) + rank
    n_used = (pad_end[-1] // MOE_TM).astype(i32)
    tile_start = jnp.minimum(jnp.arange(n_tiles, dtype=i32), n_used - 1) * MOE_TM
    tile_expert = jnp.minimum(jnp.sum((tile_start[:, None] >= pad_end[None, :]).astype(i32), axis=1), n_exp - 1)
    tile_hot = (tile_expert[:, None] == experts[None, :]).astype(i32)
    row = jnp.arange(n_tiles, dtype=i32)[:, None] * MOE_TM + jnp.arange(MOE_TM, dtype=i32)[None, :]
    off = row - jnp.sum(tile_hot * pad_start[None, :], axis=1)[:, None]
    valid = (off < jnp.sum(tile_hot * counts[None, :], axis=1)[:, None]) & (row < n_used * MOE_TM)
    src = jnp.clip(jnp.sum(tile_hot * start[None, :], axis=1)[:, None] + off, 0, n_slots - 1)
    tok_of_pos = jnp.where(valid, jnp.take(order // TOP_K, src, mode="clip"), row % n).reshape(-1)
    return tok_of_pos, pos_of_slot.reshape(n, TOP_K), tile_expert, n_used.reshape(1)


def _moe_kernel(te_ref, nu_ref, x_ref, w1_ref, b1_ref, w2_ref, b2_ref, o_ref, w1_scr, w2_scr):
    t = pl.program_id(0)
    live = t < nu_ref[0]
    new_expert = (t == 0) | (te_ref[t] != te_ref[jnp.maximum(t - 1, 0)])

    @pl.when(live & new_expert)
    def _():
        w1_scr[...] = w1_ref[0].astype(BF16)
        w2_scr[...] = w2_ref[0].astype(BF16)

    @pl.when(live)
    def _():
        hu = jnp.dot(x_ref[...], w1_scr[...], preferred_element_type=F32) + b1_ref[0]
        f = hu.shape[1] // 2
        g = jnp.minimum(hu[:, :f], SWIGLU_LIMIT)
        u = jnp.clip(hu[:, f:], -SWIGLU_LIMIT, SWIGLU_LIMIT)
        act = g * jax.nn.sigmoid(SWIGLU_ALPHA * g) * (u + 1.0)
        y = jnp.dot(act.astype(BF16), w2_scr[...], preferred_element_type=F32) + b2_ref[0]
        o_ref[...] = y.astype(o_ref.dtype)

    @pl.when(jnp.logical_not(live))
    def _():
        o_ref[...] = jnp.zeros_like(o_ref)


def _moe(xs, tile_expert, n_used, w1, b1, w2, b2):
    p, d = xs.shape
    n_exp, _, f2 = w1.shape
    grid_spec = pltpu.PrefetchScalarGridSpec(
        num_scalar_prefetch=2, grid=(p // MOE_TM,),
        in_specs=[pl.BlockSpec((MOE_TM, d), lambda t, te, nu: (t, 0)),
                  pl.BlockSpec((1, d, f2), lambda t, te, nu: (te[t], 0, 0)),
                  pl.BlockSpec((1, 1, f2), lambda t, te, nu: (te[t], 0, 0)),
                  pl.BlockSpec((1, f2 // 2, d), lambda t, te, nu: (te[t], 0, 0)),
                  pl.BlockSpec((1, 1, d), lambda t, te, nu: (te[t], 0, 0))],
        out_specs=pl.BlockSpec((MOE_TM, d), lambda t, te, nu: (t, 0)),
        scratch_shapes=[pltpu.VMEM((d, f2), BF16), pltpu.VMEM((f2 // 2, d), BF16)])
    return pl.pallas_call(
        _moe_kernel, grid_spec=grid_spec,
        out_shape=jax.ShapeDtypeStruct((p, d), BF16),
        compiler_params=_cparams("arbitrary"), name="moe",
    )(tile_expert, n_used, xs, w1, b1.reshape(n_exp, 1, f2), w2, b2.reshape(n_exp, 1, d))


def _dot_nt(a, b, precision=None):
    return lax.dot_general(a, b, (((1,), (1,)), ((), ())), preferred_element_type=F32, precision=precision)


def _dot_tn(a, b, precision=None):
    return lax.dot_general(a, b, (((0,), (0,)), ((), ())), preferred_element_type=F32, precision=precision)


def _scan_chunk(d, i, n_chunks, n_ctx_chunks):
    rev = jnp.where(i < n_ctx_chunks, n_ctx_chunks - 1 - i, n_chunks - 1 + n_ctx_chunks - i)
    return jnp.where(d == 0, i, rev)


def _head_sum_mats():
    ch = jnp.arange(MIX_W) // RWKV_HD
    e = (ch[:, None] == jnp.arange(LANES)[None, :]).astype(BF16)
    return e, e.T


def _dot_split(x, m_bf16):
    hi = x.astype(BF16)
    lo = (x - hi.astype(F32)).astype(BF16)
    return (jnp.dot(hi, m_bf16, preferred_element_type=F32) + jnp.dot(lo, m_bf16, preferred_element_type=F32))


def _rwkv_prep_kernel(*refs, tiles_per_batch):
    pieces, rest = [refs[3 * j:3 * j + 3] for j in range(4)], refs[12:]
    (shift_ref, w0_ref, wb_ref, a0_ref, ab_ref, gb_ref, kk_ref, ka_ref, rk_ref, e_ref, et_ref,
     r_out, v_out, kk_out, g_out, bonus_out, lw_out, b_out, kd_out) = rest
    i = pl.program_id(1)
    rows = lax.broadcasted_iota(jnp.int32, (ROW_TILE, 1), 0)
    has_prev = jnp.where(i > 1, 1.0, 0.0)
    has_next = jnp.where((i > 0) & (i < tiles_per_batch - 1), 1.0, 0.0)

    def token_shift(piece, col0):
        p_ref, pp_ref, pn_ref = piece
        x = p_ref[0]
        w = x.shape[1]
        xp = jnp.where(rows == 0, pp_ref[0, 7:8, :] * has_prev, pltpu.roll(x, 1, 0))
        xn = jnp.where(rows == ROW_TILE - 1, pn_ref[0, 0:1, :] * has_next, pltpu.roll(x, ROW_TILE - 1, 0))
        sh = shift_ref[:, col0:col0 + w]
        return xp * sh[0:1, :] + x * sh[1:2, :] + xn * sh[2:3, :]

    m = MIX_W
    r, k, v = (token_shift(pieces[j], j * m) for j in range(3))
    lora = token_shift(pieces[3], 3 * m)
    wl = lora[:, :RWKV_DECAY_RANK]
    al = lora[:, RWKV_DECAY_RANK:RWKV_DECAY_RANK + RWKV_AAA_RANK]
    gl = lora[:, RWKV_DECAY_RANK + RWKV_AAA_RANK:]
    dot = functools.partial(jnp.dot, preferred_element_type=F32, precision=HI)
    e, et = e_ref[...], et_ref[...]
    g_out[0] = dot(jax.nn.sigmoid(gl), gb_ref[...])
    kk0 = k * kk_ref[...]
    nrm = jnp.maximum(jnp.sqrt(_dot_split(kk0 * kk0, e)), 1e-12)
    kk = kk0 * _dot_split(1.0 / nrm, et)
    tw = jnp.tanh(wl)
    bsum = None
    for d in range(2):
        a = jax.nn.sigmoid(a0_ref[d:d + 1, :] + dot(al, ab_ref[d]))
        kd = k * (1.0 + (a - 1.0) * ka_ref[...])
        lw = (-math.exp(-0.5)) * jax.nn.sigmoid(w0_ref[d:d + 1, :] + dot(tw, wb_ref[d]))
        lw_out[d, 0] = lw
        b_out[d, 0] = kk * a
        kd_out[d, 0] = kd
        t = _dot_split(r * kd * rk_ref[...], e)
        bsum = t if bsum is None else bsum + t
    r_out[0] = r
    v_out[0] = v
    kk_out[0] = kk
    bonus_out[0] = _dot_split(bsum, et) * v


def _rwkv_prep(p3, shift, w0, wb, a0, ab, gb, k_k, k_a, r_k, ctx_len):
    b, tt, _ = p3.shape
    tpb = tt // ROW_TILE
    assert ctx_len == ROW_TILE
    e, et = _head_sum_mats()
    hb = ROW_TILE // 8
    full = lambda shape: pl.BlockSpec(shape, lambda bb, i: (0,) * len(shape))
    row1 = pl.BlockSpec((1, ROW_TILE, MIX_W), lambda bb, i: (bb, i, 0))
    row2 = pl.BlockSpec((2, 1, ROW_TILE, MIX_W), lambda bb, i: (0, bb, i, 0))
    s1 = jax.ShapeDtypeStruct((b, tt, MIX_W), F32)
    s2 = jax.ShapeDtypeStruct((2, b, tt, MIX_W), F32)
    vec = lambda a: a.reshape(1, MIX_W)

    def piece(width, col_block):
        return [pl.BlockSpec((1, ROW_TILE, width), lambda bb, i: (bb, i, col_block)),
                pl.BlockSpec((1, 8, width), lambda bb, i: (bb, jnp.maximum(i * hb - 1, 0), col_block)),
                pl.BlockSpec((1, 8, width), lambda bb, i: (bb, jnp.minimum((i + 1) * hb, tpb * hb - 1), col_block))]

    pieces = [s for j in range(3) for s in piece(MIX_W, OFF_RWKV // MIX_W + j)] + piece(RWKV_LORA_W, OFF_RWKV_LORA // RWKV_LORA_W)
    return pl.pallas_call(
        functools.partial(_rwkv_prep_kernel, tiles_per_batch=tpb), grid=(b, tpb),
        in_specs=pieces + [full((3, RWKV_SHIFT_W)), full((2, MIX_W)), full((2, RWKV_DECAY_RANK, MIX_W)), full((2, MIX_W)),
                           full((2, RWKV_AAA_RANK, MIX_W)), full((RWKV_GATE_RANK, MIX_W)),
                           full((1, MIX_W)), full((1, MIX_W)), full((1, MIX_W)), full((MIX_W, LANES)), full((LANES, MIX_W))],
        out_specs=[row1, row1, row1, row1, row1, row2, row2, row2],
        out_shape=[s1, s1, s1, s1, s1, s2, s2, s2],
        compiler_params=_cparams("parallel", "parallel"), name="rwkv_prep",
    )(*([p3] * 12), shift, w0, wb, a0, ab, gb, vec(k_k), vec(k_a), vec(r_k), e, et)


def _rwkv_scan_kernel(r_ref, v_ref, kk_ref, lw_ref, b_ref, kd_ref, y_ref, s_scr):
    d = pl.program_id(0)
    i = pl.program_id(2)
    c = RWKV_CHUNK

    @pl.when(i == 0)
    def _():
        s_scr[...] = jnp.zeros_like(s_scr)

    dot = functools.partial(jnp.dot, preferred_element_type=F32, precision=HI)
    bf = lambda x: x.astype(BF16)
    mm = lambda x, y: jnp.dot(bf(x), bf(y), preferred_element_type=F32)
    sgn = 1 - 2 * d
    ti = lax.broadcasted_iota(jnp.int32, (c, c), 0)
    si = lax.broadcasted_iota(jnp.int32, (c, c), 1)
    tri = jnp.where((ti - si) * sgn >= 0, 1.0, 0.0)
    lw = lw_ref[0, 0]
    cum = dot(tri, lw)
    e_excl = jnp.exp(cum - lw)
    e_neg = jnp.exp(-cum)
    e_pos = jnp.exp(cum)
    g_tot = jnp.exp(jnp.sum(lw, axis=0, keepdims=True))
    a_t = -kk_ref[0] * e_excl
    b_t = b_ref[0, 0] * e_neg
    k_t = kd_ref[0, 0] * e_neg
    r_t = r_ref[0] * e_pos
    v_all = v_ref[0]

    ri = lax.broadcasted_iota(jnp.int32, (LANES, LANES), 0)
    ci = lax.broadcasted_iota(jnp.int32, (LANES, LANES), 1)
    same = (ri // c) == (ci // c)
    strict = same & ((ri - ci) * sgn > 0)
    incl = same & ((ri - ci) * sgn >= 0)
    lane = lax.broadcasted_iota(jnp.int32, (1, LANES), 1)
    m0 = (lane < RWKV_HD).astype(F32)
    m1 = 1.0 - m0
    stack = lambda x: jnp.concatenate([x * m0, x * m1], axis=0)
    unstack = lambda x: x[:c] + x[c:]

    pairs = range(RWKV_HEADS // 2)
    sls = [slice(p * LANES, (p + 1) * LANES) for p in pairs]
    a_st = [bf(stack(a_t[:, sl])) for sl in sls]
    r_st = [bf(stack(r_t[:, sl])) for sl in sls]
    v_st = [bf(stack(v_all[:, sl])) for sl in sls]
    b_p = [bf(b_t[:, sl]) for sl in sls]
    k_p = [bf(k_t[:, sl]) for sl in sls]
    gm = [_dot_nt(jnp.concatenate([a_st[p], r_st[p]], axis=0),
                  jnp.concatenate([b_p[p], b_p[p], k_p[p], k_p[p]], axis=0)) for p in pairs]
    m_ak = [bf(jnp.where(strict, gm[p][:LANES, LANES:], 0.0)) for p in pairs]
    x = [jnp.concatenate([a_st[p].astype(F32), jnp.dot(m_ak[p], v_st[p], preferred_element_type=F32)], axis=1)
         for p in pairs]
    pw = [jnp.where(strict, gm[p][:LANES, :LANES], 0.0) for p in pairs]
    for step in range(6):
        if step:
            pw = [mm(pw[p], pw[p]) for p in pairs]
        x = [x[p] + mm(pw[p], x[p]) for p in pairs]
    s_bd = [s_scr[p] for p in pairs]
    s_bf = [bf(s) for s in s_bd]
    u_st = [_dot_nt(bf(x[p][:, :LANES]), s_bf[p]) + x[p][:, LANES:] for p in pairs]
    m_rb = [bf(jnp.where(incl, gm[p][LANES:, :LANES], 0.0)) for p in pairs]
    m_rk = [bf(jnp.where(incl, gm[p][LANES:, LANES:], 0.0)) for p in pairs]
    for p in pairs:
        y_st = (_dot_nt(r_st[p], s_bf[p]) + jnp.dot(m_rb[p], bf(u_st[p]), preferred_element_type=F32)
                + jnp.dot(m_rk[p], v_st[p], preferred_element_type=F32))
        y_ref[0, 0, :, sls[p]] = unstack(y_st)
    for p in pairs:
        uv = jnp.concatenate([bf(unstack(u_st[p])), bf(v_all[:, sls[p]])], axis=0)
        bk = jnp.concatenate([b_p[p], k_p[p]], axis=0)
        upd = _dot_tn(uv, bk)
        s_scr[p] = g_tot[:, sls[p]] * (s_bd[p] + jnp.where(same, upd, 0.0))


def _rwkv_scan(r, v, kk, lw, bb, kd, ctx_len):
    b, tt, _ = r.shape
    c = RWKV_CHUNK
    n = tt // c
    nc = ctx_len // c
    blk1 = pl.BlockSpec((1, c, MIX_W), lambda d, b_, i: (b_, _scan_chunk(d, i, n, nc), 0))
    blk2 = pl.BlockSpec((1, 1, c, MIX_W), lambda d, b_, i: (d, b_, _scan_chunk(d, i, n, nc), 0))
    return pl.pallas_call(
        _rwkv_scan_kernel, grid=(2, b, n),
        in_specs=[blk1, blk1, blk1, blk2, blk2, blk2], out_specs=blk2,
        out_shape=jax.ShapeDtypeStruct((2, b, tt, MIX_W), F32),
        scratch_shapes=[pltpu.VMEM((RWKV_HEADS // 2, LANES, LANES), F32)],
        compiler_params=_cparams("parallel", "parallel", "arbitrary"), name="rwkv_scan",
    )(r, v, kk, lw, bb, kd)


def _rwkv_finish_kernel(y_ref, bonus_ref, g_ref, w_ref, b_ref, e_ref, et_ref, o_ref):
    e, et = e_ref[...], et_ref[...]
    y = y_ref[0, 0] + y_ref[1, 0]
    mu = _dot_split(_dot_split(y, e), et) * (1.0 / RWKV_HD)
    yc = y - mu
    var = _dot_split(_dot_split(yc * yc, e), et) * (1.0 / RWKV_HD)
    yn = yc * lax.rsqrt(var + RWKV_GN_EPS) * w_ref[...] + b_ref[...]
    o_ref[0] = ((yn + bonus_ref[0]) * g_ref[0]).astype(o_ref.dtype)


def _rwkv_finish(y, bonus, g, gn_w, gn_b):
    _, b, tt, _ = y.shape
    e, et = _head_sum_mats()
    row = pl.BlockSpec((1, ROW_TILE, MIX_W), lambda bb, i: (bb, i, 0))
    full = lambda shape: pl.BlockSpec(shape, lambda bb, i: (0,) * len(shape))
    return pl.pallas_call(
        _rwkv_finish_kernel, grid=(b, tt // ROW_TILE),
        in_specs=[pl.BlockSpec((2, 1, ROW_TILE, MIX_W), lambda bb, i: (0, bb, i, 0)), row, row,
                  full((1, MIX_W)), full((1, MIX_W)), full((MIX_W, LANES)), full((LANES, MIX_W))],
        out_specs=row, out_shape=jax.ShapeDtypeStruct((b, tt, MIX_W), BF16),
        compiler_params=_cparams("parallel", "parallel"), name="rwkv_finish",
    )(y, bonus, g, gn_w.reshape(1, MIX_W), gn_b.reshape(1, MIX_W), e, et)


def _rope_swap(x):
    lane = lax.broadcasted_iota(jnp.int32, x.shape, 1)
    return jnp.where((lane % 64) < 32, pltpu.roll(x, 96, 1), pltpu.roll(x, 32, 1))


def _rope_tables(t, ctx_len):
    pos = jnp.arange(t, dtype=jnp.int32)
    nf = RET_QK_HD // 4
    inv = ROPE_BASE ** (-jnp.arange(nf, dtype=F32) / nf)
    ang_r = (pos // GRID_W).astype(F32)[:, None] * inv[None, :]
    ang_c = (pos % GRID_W).astype(F32)[:, None] * inv[None, :]
    cos = jnp.concatenate([jnp.cos(ang_r)] * 2 + [jnp.cos(ang_c)] * 2, axis=-1)
    sin = jnp.concatenate([-jnp.sin(ang_r), jnp.sin(ang_r), -jnp.sin(ang_c), jnp.sin(ang_c)], axis=-1)
    cos = jnp.concatenate([jnp.ones((ctx_len, RET_QK_HD), F32), cos], axis=0)
    sin = jnp.concatenate([jnp.zeros((ctx_len, RET_QK_HD), F32), sin], axis=0)
    return cos, sin


def _ret_kernel(dec_ref, q_ref, k_ref, v_ref, cos_ref, sin_ref, o_ref, r_scr):
    d = pl.program_id(0)
    i = pl.program_id(2)
    c = RET_CHUNK

    @pl.when(i == 0)
    def _():
        r_scr[...] = jnp.zeros_like(r_scr)

    bf = lambda x: x.astype(BF16)
    cos, sin = cos_ref[...], sin_ref[...]
    sign = (1 - 2 * d).astype(F32)
    ii = lax.broadcasted_iota(jnp.int32, (c, c), 0).astype(F32)
    jj = lax.broadcasted_iota(jnp.int32, (c, c), 1).astype(F32)
    pos = jnp.where(d == 0, ii, c - 1.0 - ii)
    diff = (ii - jj) * sign
    heads = range(RET_HEADS)
    qs = [slice(h * RET_QK_HD, (h + 1) * RET_QK_HD) for h in heads]
    vs = [slice(h * RET_V_HD, (h + 1) * RET_V_HD) for h in heads]
    lg = [jnp.log(jax.nn.sigmoid(jnp.full((c, c), dec_ref[d, h], F32))) for h in heads]
    dmat = [jnp.where(diff >= 0, jnp.exp(lg[h] * jnp.maximum(diff, 0.0)), 0.0) for h in heads]
    q_dec = [jnp.exp(lg[h] * (pos + 1.0)) for h in heads]
    k_dec = [jnp.exp(lg[h] * (c - 1.0 - pos)) for h in heads]
    c_dec = [jnp.exp(lg[h][:1, :1] * float(c)) for h in heads]
    q = [q_ref[0, :, qs[h]] for h in heads]
    k = [k_ref[0, :, qs[h]] for h in heads]
    q = [q[h] * cos + _rope_swap(q[h]) * sin for h in heads]
    k = [(k[h] * cos + _rope_swap(k[h]) * sin) * (RET_QK_HD ** -0.5) for h in heads]
    v = [bf(v_ref[0, :, vs[h]]) for h in heads]
    r = [r_scr[h] for h in heads]
    s = [_dot_nt(bf(q[h]), bf(k[h])) * dmat[h] for h in heads]
    for h in heads:
        o = jnp.dot(bf(s[h]), v[h], preferred_element_type=F32)
        o_ref[0, 0, :, vs[h]] = o + jnp.dot(bf(q[h] * q_dec[h]), bf(r[h]), preferred_element_type=F32)
    for h in heads:
        r_scr[h] = r[h] * c_dec[h] + _dot_tn(bf(k[h] * k_dec[h]), v[h])


def _retention(p3, cos, sin, decay, ctx_len):
    b, tt, _ = p3.shape
    c = RET_CHUNK
    n = tt // c
    nc = ctx_len // c
    qb, kb, vb = OFF_RET_Q // RET_QK_W, OFF_RET_K // RET_QK_W, OFF_RET_V // MIX_W
    tok = lambda d, i: _scan_chunk(d, i, n, nc)
    grid_spec = pltpu.PrefetchScalarGridSpec(
        num_scalar_prefetch=1, grid=(2, b, n),
        in_specs=[pl.BlockSpec((1, c, RET_QK_W), lambda d, bb, i, dec: (bb, tok(d, i), qb)),
                  pl.BlockSpec((1, c, RET_QK_W), lambda d, bb, i, dec: (bb, tok(d, i), kb)),
                  pl.BlockSpec((1, c, MIX_W), lambda d, bb, i, dec: (bb, tok(d, i), vb)),
                  pl.BlockSpec((c, RET_QK_HD), lambda d, bb, i, dec: (tok(d, i), 0)),
                  pl.BlockSpec((c, RET_QK_HD), lambda d, bb, i, dec: (tok(d, i), 0))],
        out_specs=pl.BlockSpec((1, 1, c, MIX_W), lambda d, bb, i, dec: (d, bb, tok(d, i), 0)),
        scratch_shapes=[pltpu.VMEM((RET_HEADS, RET_QK_HD, RET_V_HD), F32)])
    return pl.pallas_call(
        _ret_kernel, grid_spec=grid_spec,
        out_shape=jax.ShapeDtypeStruct((2, b, tt, MIX_W), F32),
        compiler_params=_cparams("parallel", "parallel", "arbitrary"), name="retention",
    )(decay, p3, p3, p3, cos, sin)


def _ret_finish_kernel(o_ref, g_ref, w_ref, b_ref, y_ref):
    o = o_ref[0, 0] + o_ref[1, 0]
    y = _norm(o, RET_GN_EPS) * w_ref[...] + b_ref[...]
    g = g_ref[0]
    y_ref[0] = (y * (g * jax.nn.sigmoid(g))).astype(y_ref.dtype)


def _ret_finish(o, p3, gn_w, gn_b):
    _, b, tt, _ = o.shape
    gb = OFF_RET_G // RET_V_HD
    return pl.pallas_call(
        _ret_finish_kernel, grid=(b, tt // ROW_TILE, RET_HEADS),
        in_specs=[pl.BlockSpec((2, 1, ROW_TILE, RET_V_HD), lambda bb, i, h: (0, bb, i, h)),
                  pl.BlockSpec((1, ROW_TILE, RET_V_HD), lambda bb, i, h: (bb, i, gb + h)),
                  pl.BlockSpec((1, RET_V_HD), lambda bb, i, h: (0, h)),
                  pl.BlockSpec((1, RET_V_HD), lambda bb, i, h: (0, h))],
        out_specs=pl.BlockSpec((1, ROW_TILE, RET_V_HD), lambda bb, i, h: (bb, i, h)),
        out_shape=jax.ShapeDtypeStruct((b, tt, MIX_W), BF16),
        compiler_params=_cparams("parallel", "parallel", "parallel"), name="ret_finish",
    )(o, p3, gn_w.reshape(1, MIX_W), gn_b.reshape(1, MIX_W))


NA_BLOCK_ROWS = 4
NA_BLOCK_TOK = NA_BLOCK_ROWS * GRID_W
NA_WIN_TOK = NA_WIN_ROWS * GRID_W


def _softmax_pv(s_parts, v_parts):
    m = None
    for s in s_parts:
        mm = jnp.max(s, axis=1, keepdims=True)
        m = mm if m is None else jnp.maximum(m, mm)
    den, acc = None, None
    for s, v in zip(s_parts, v_parts):
        e = jnp.exp(s - m)
        dd = jnp.sum(e, axis=1, keepdims=True)
        pv = jnp.dot(e.astype(BF16), v, preferred_element_type=F32)
        den = dd if den is None else den + dd
        acc = pv if acc is None else acc + pv
    return acc / den


def _na_kernel(q_ref, kp_ref, kc_ref, kn_ref, vp_ref, vc_ref, vn_ref, kx_ref, vx_ref, bias_ref, o_ref,
               k_scr, v_scr, *, n_blocks):
    rb = pl.program_id(2)
    lane = lax.broadcasted_iota(jnp.int32, (1, LANES), 1)
    head_masks = [(lane < NA_HD).astype(F32), (lane >= NA_HD).astype(F32)]
    kx = kx_ref[0].astype(BF16)
    vx = vx_ref[0].astype(BF16)
    scale = NA_HD ** -0.5

    @pl.when(rb == 0)
    def _():
        q = q_ref[0] * scale
        out = jnp.zeros((NA_BLOCK_TOK, LANES), F32)
        for hm in head_masks:
            s = _dot_nt((q * hm).astype(BF16), kx)
            out = out + _softmax_pv([s], [vx]) * hm
        o_ref[0] = out.astype(o_ref.dtype)

    @pl.when(rb > 0)
    def _():
        t = NA_BLOCK_TOK
        k_scr[0:t, :] = kp_ref[0].astype(BF16)
        k_scr[t:2 * t, :] = kc_ref[0].astype(BF16)
        k_scr[2 * t:3 * t, :] = kn_ref[0].astype(BF16)
        v_scr[0:t, :] = vp_ref[0].astype(BF16)
        v_scr[t:2 * t, :] = vc_ref[0].astype(BF16)
        v_scr[2 * t:3 * t, :] = vn_ref[0].astype(BF16)
        first = rb == 1
        last = rb == n_blocks
        rows = range(NA_BLOCK_ROWS)
        heads = range(2)
        rs = [slice(j * GRID_W, (j + 1) * GRID_W) for j in rows]
        kw, vw, oi = [], [], []
        for j in rows:
            off = jnp.where(last, 0, jnp.where(first, NA_BLOCK_ROWS, j))
            oi.append(jnp.where(last, NA_BLOCK_ROWS + j, jnp.where(first, j, NA_WIN_ROWS // 2)))
            start = pl.multiple_of(off * GRID_W, GRID_W)
            kw.append(k_scr[pl.ds(start, NA_WIN_TOK), :])
            vw.append(v_scr[pl.ds(start, NA_WIN_TOK), :])
        q = q_ref[0] * scale
        qh = [(q * hm).astype(BF16) for hm in head_masks]
        s_ctx = [_dot_nt(qh[h], kx) for h in heads]
        s_loc = [[_dot_nt(qh[h][rs[j]], kw[j]) + bias_ref[oi[j], h] for h in heads] for j in rows]
        m_ctx = [jnp.max(s_ctx[h], axis=1, keepdims=True) for h in heads]
        m = [[jnp.maximum(jnp.max(s_loc[j][h], axis=1, keepdims=True), m_ctx[h][rs[j]]) for h in heads] for j in rows]
        m_all = [jnp.concatenate([m[j][h] for j in rows], axis=0) for h in heads]
        e_ctx = [jnp.exp(s_ctx[h] - m_all[h]) for h in heads]
        e_loc = [[jnp.exp(s_loc[j][h] - m[j][h]) for h in heads] for j in rows]
        d_ctx = [jnp.sum(e_ctx[h], axis=1, keepdims=True) for h in heads]
        pv_ctx = [jnp.dot(e_ctx[h].astype(BF16), vx, preferred_element_type=F32) for h in heads]
        for j in rows:
            out = None
            for h in heads:
                den = jnp.sum(e_loc[j][h], axis=1, keepdims=True) + d_ctx[h][rs[j]]
                pv = jnp.dot(e_loc[j][h].astype(BF16), vw[j], preferred_element_type=F32) + pv_ctx[h][rs[j]]
                term = pv / den * head_masks[h]
                out = term if out is None else out + term
            o_ref[0, rs[j], :] = out.astype(o_ref.dtype)


def _na_bias_table(rpb):
    col = jnp.arange(GRID_W)
    cs = jnp.clip(col - NA_WIN_COLS // 2, 0, GRID_W - NA_WIN_COLS)
    col_ok = (col[None, :] >= cs[:, None]) & (col[None, :] < cs[:, None] + NA_WIN_COLS)
    col_idx = jnp.clip(col[None, :] - col[:, None] + (NA_WIN_COLS - 1), 0, 2 * NA_WIN_COLS - 2)
    tabs = []
    for oi in range(NA_WIN_ROWS):
        row_idx = jnp.arange(NA_WIN_ROWS) - oi + (NA_WIN_ROWS - 1)
        bias = rpb[:, row_idx][:, :, col_idx]
        bias = jnp.where(col_ok[None, None], bias, -1e30)
        tabs.append(bias.transpose(0, 2, 1, 3).reshape(NA_HEADS, GRID_W, NA_WIN_TOK))
    return jnp.stack(tabs, 0).astype(F32)


def _na(p3, bias_tab, ctx_len):
    b, tt, _ = p3.shape
    assert ctx_len == NA_BLOCK_TOK
    t = tt - ctx_len
    n_rows = t // GRID_W
    assert n_rows % NA_BLOCK_ROWS == 0 and n_rows >= NA_WIN_ROWS
    nb = n_rows // NA_BLOCK_ROWS
    qb, kb, vb = OFF_NA_Q // LANES, OFF_NA_K // LANES, OFF_NA_V // LANES
    blk = (1, NA_BLOCK_TOK, LANES)

    def spec(colb, shift):
        return pl.BlockSpec(blk, lambda pr, bb, rb: (bb, jnp.clip(rb + shift, 1, nb), colb + pr))

    return pl.pallas_call(
        functools.partial(_na_kernel, n_blocks=nb), grid=(NA_HEADS // 2, b, nb + 1),
        in_specs=[pl.BlockSpec(blk, lambda pr, bb, rb: (bb, rb, qb + pr)),
                  spec(kb, -1), spec(kb, 0), spec(kb, 1), spec(vb, -1), spec(vb, 0), spec(vb, 1),
                  pl.BlockSpec(blk, lambda pr, bb, rb: (bb, 0, kb + pr)),
                  pl.BlockSpec(blk, lambda pr, bb, rb: (bb, 0, vb + pr)),
                  pl.BlockSpec((NA_WIN_ROWS, 2, GRID_W, NA_WIN_TOK), lambda pr, bb, rb: (0, pr, 0, 0))],
        out_specs=pl.BlockSpec(blk, lambda pr, bb, rb: (bb, rb, pr)),
        out_shape=jax.ShapeDtypeStruct((b, tt, MIX_W), BF16),
        scratch_shapes=[pltpu.VMEM((3 * NA_BLOCK_TOK, LANES), BF16), pltpu.VMEM((3 * NA_BLOCK_TOK, LANES), BF16)],
        compiler_params=_cparams("parallel", "parallel", "parallel"), name="na",
    )(p3, p3, p3, p3, p3, p3, p3, p3, p3, bias_tab)


def _build_mod(gate, shift, scale, b):
    rows = jnp.stack([gate, shift, scale], axis=1)
    lat = rows[:b]
    ctx = jnp.broadcast_to(rows[b][None], lat.shape)
    mod = jnp.stack([ctx, lat], axis=1)
    return jnp.pad(mod, ((0, 0), (0, 0), (0, 5), (0, 0)))


def kernel(x, c, ctx, c_ctx, ada_w, ada_b, w_in, rwkv_shift, rwkv_w0, rwkv_wB, rwkv_a0, rwkv_aB, rwkv_gB, rwkv_kk, rwkv_ka, rwkv_rk, rwkv_gn_w, rwkv_gn_b, ret_decay, ret_gn_w, ret_gn_b, na_rpb, w_branch, w_out, ln1_w, ln1_b, ln2_w, ln2_b, router_w, router_b, exp_w1, exp_b1, exp_w2, exp_b2):
    b, t, d = x.shape
    ctx_len = ctx.shape[1]
    tt = ctx_len + t
    n = b * tt
    tpb = tt // ROW_TILE
    depth = ada_w.shape[0]
    alpha = (2 * depth) ** 0.25
    assert d == D_MODEL and b + 1 <= 8 and ctx_len == ROW_TILE and t % ROW_TILE == 0

    xa = jnp.concatenate([ctx, x], axis=1).reshape(n, d)
    cond = jnp.concatenate([c, c_ctx[None], jnp.zeros((8 - b - 1, d), F32)], axis=0)
    ada = _adaln(cond.T, ada_w, ada_b, b + 1).reshape(depth, 8, 6, d)
    cos, sin = _rope_tables(t, ctx_len)
    n_exp = router_w.shape[-1]
    rw_pad = jnp.pad(router_w, ((0, 0), (0, 0), (0, LANES - n_exp)))
    rb_pad = jnp.pad(router_b, ((0, 0), (0, LANES - n_exp)), constant_values=-1e30)
    zero = jnp.zeros((8, d), F32)

    h = _modulate(xa, _build_mod(zero, ada[0, :, 0], ada[0, :, 1], b), tpb)
    for l in range(depth):
        p = _in_proj(h, w_in[l])
        p3 = p.reshape(b, tt, D_IN_PROJ)

        r, v, kk, g, bonus, lw, bb, kd = _rwkv_prep(p3, rwkv_shift[l], rwkv_w0[l], rwkv_wB[l], rwkv_a0[l], rwkv_aB[l],
                                                    rwkv_gB[l], rwkv_kk[l], rwkv_ka[l], rwkv_rk[l], ctx_len)
        y_scan = _rwkv_scan(r, v, kk, lw, bb, kd, ctx_len)
        ya = _rwkv_finish(y_scan, bonus, g, rwkv_gn_w[l], rwkv_gn_b[l])

        o_ret = _retention(p3, cos, sin, ret_decay[l], ctx_len)
        yb = _ret_finish(o_ret, p3, ret_gn_w[l], ret_gn_b[l])

        yn = _na(p3, _na_bias_table(na_rpb[l]), ctx_len)

        merged = _merge(ya.reshape(n, MIX_W), yb.reshape(n, MIX_W), yn.reshape(n, MIX_W), p, w_branch[l].astype(BF16))
        y = _matmul(merged, w_out[l].astype(BF16), 512)

        mod_f = _build_mod(ada[l, :, 2], ada[l, :, 3], ada[l, :, 4], b)
        x1, h2, route = _resid_ln(xa, y, mod_f, ln1_w[l], ln1_b[l], tpb, alpha, rw_pad[l], rb_pad[l][None])
        tok_of_pos, pos_of_slot, tile_expert, n_used = _moe_dispatch(route, n_exp)
        xs = jnp.take(h2, tok_of_pos, axis=0, mode="clip")
        ys = _moe(xs, tile_expert, n_used, exp_w1[l], exp_b1[l], exp_w2[l], exp_b2[l])
        f4 = [jnp.take(ys, pos_of_slot[:, k], axis=0, mode="clip") for k in range(TOP_K)]
        nxt = min(l + 1, depth - 1)
        mod_a = _build_mod(ada[l, :, 5], ada[nxt, :, 0], ada[nxt, :, 1], b)
        xa, h = _resid_ln(x1, f4, mod_a, ln2_w[l], ln2_b[l], tpb, alpha, route=route)
    return xa.reshape(b, tt, d)[:, ctx_len:]
```

```python
import functools
import math

import jax
import jax.numpy as jnp
from jax import lax
from jax.experimental import pallas as pl
from jax.experimental.pallas import tpu as pltpu

F32 = jnp.float32
BF16 = jnp.bfloat16
HI = lax.Precision.HIGHEST

D_MODEL = 2048
GRID_W = 64
MIX_W = D_MODEL // 2
N_BRANCH = 3

RWKV_HD = 64
RWKV_HEADS = MIX_W // RWKV_HD
RWKV_DECAY_RANK = 64
RWKV_AAA_RANK = 64
RWKV_GATE_RANK = 128
RWKV_SHIFT_W = 3 * MIX_W + RWKV_DECAY_RANK + RWKV_AAA_RANK + RWKV_GATE_RANK
RWKV_GN_EPS = 64e-5
RWKV_CHUNK = 64

RET_HEADS = 4
RET_V_HD = MIX_W // RET_HEADS
RET_QK_HD = RET_V_HD // 2
RET_QK_W = RET_HEADS * RET_QK_HD
RET_CHUNK = 128
RET_GN_EPS = 1e-6

NA_HD = 64
NA_HEADS = MIX_W // NA_HD
NA_WIN_ROWS = 8
NA_WIN_COLS = 16

N_EXPERTS = 32
TOP_K = 4
D_EXPERT = D_MODEL // 4
SWIGLU_LIMIT = 7.0
SWIGLU_ALPHA = 1.702

ROPE_BASE = 10000.0
LN_EPS = 1e-6

IN_PROJ_TN = 256
D_IN_PROJ = RWKV_SHIFT_W + 2 * RET_QK_W + 5 * MIX_W + N_BRANCH * D_MODEL
IN_PROJ_ROT = RWKV_SHIFT_W // IN_PROJ_TN
OFF_RET_Q = 0
OFF_RET_K = OFF_RET_Q + RET_QK_W
OFF_RET_V = OFF_RET_K + RET_QK_W
OFF_RET_G = OFF_RET_V + MIX_W
OFF_NA_Q = OFF_RET_G + MIX_W
OFF_NA_K = OFF_NA_Q + MIX_W
OFF_NA_V = OFF_NA_K + MIX_W
OFF_GATE = OFF_NA_V + MIX_W
OFF_RWKV = OFF_GATE + N_BRANCH * D_MODEL
OFF_RWKV_LORA = OFF_RWKV + 3 * MIX_W
RWKV_LORA_W = RWKV_SHIFT_W - 3 * MIX_W

ROW_TILE = 256
LANES = 128
VMEM_LIMIT = 56 * 1024 * 1024


def _cparams(*sem):
    return pltpu.CompilerParams(dimension_semantics=sem, vmem_limit_bytes=VMEM_LIMIT)


def _adaln_kernel(ct_ref, w_ref, b_ref, o_ref, *, n_cond):
    ct = ct_ref[...]
    st = ct * jax.nn.sigmoid(ct)
    w = w_ref[0]
    o_ref[0] = jnp.zeros(o_ref.shape[1:], F32)
    for m in range(n_cond):
        o_ref[0, m:m + 1, :] = jnp.sum(st[:, m:m + 1] * w, axis=0, keepdims=True) + b_ref[0]


def _adaln(cond_t, ada_w, ada_b, n_cond):
    n_layers, d, d6 = ada_w.shape
    tn = 512
    return pl.pallas_call(
        functools.partial(_adaln_kernel, n_cond=n_cond),
        grid=(n_layers, d6 // tn),
        in_specs=[pl.BlockSpec((d, 8), lambda l, j: (0, 0)),
                  pl.BlockSpec((1, d, tn), lambda l, j: (l, 0, j)),
                  pl.BlockSpec((1, 1, tn), lambda l, j: (l, 0, j))],
        out_specs=pl.BlockSpec((1, 8, tn), lambda l, j: (l, 0, j)),
        out_shape=jax.ShapeDtypeStruct((n_layers, 8, d6), F32),
        compiler_params=_cparams("parallel", "parallel"),
        name="adaln",
    )(cond_t, ada_w, ada_b.reshape(n_layers, 1, d6))


def _norm(x, eps):
    mu = jnp.mean(x, axis=-1, keepdims=True)
    xc = x - mu
    var = jnp.mean(xc * xc, axis=-1, keepdims=True)
    return xc * lax.rsqrt(var + eps)


def _top4_route(logits):
    lane = lax.broadcasted_iota(jnp.int32, logits.shape, 1).astype(F32)
    l = logits
    idxs, vals = [], []
    for k in range(TOP_K):
        m = jnp.max(l, axis=1, keepdims=True)
        idx = jnp.min(jnp.where(l == m, lane, float(LANES)), axis=1, keepdims=True)
        l = jnp.where(lane == idx, -jnp.inf, l)
        idxs.append(idx)
        vals.append(m)
    es = [jnp.exp(v - vals[0]) for v in vals]
    den = es[0] + es[1] + es[2] + es[3]
    out = jnp.zeros(logits.shape, F32)
    for k in range(TOP_K):
        out = jnp.where(lane == float(k), idxs[k], out)
        out = jnp.where(lane == float(TOP_K + k), es[k] / den, out)
    return out


def _mod_kernel(x_ref, mod_ref, h_ref):
    md = mod_ref[0, 0]
    h = _norm(x_ref[...], LN_EPS) * (1.0 + md[2:3, :]) + md[1:2, :]
    h_ref[...] = h.astype(BF16)


def _resid_kernel(*refs, alpha, router, combine):
    refs = list(refs)
    n_y = TOP_K if combine else 1
    x_ref, y_refs = refs[0], refs[1:1 + n_y]
    mod_ref, lnw_ref, lnb_ref = refs[1 + n_y:4 + n_y]
    rest = refs[4 + n_y:]
    if combine:
        route_ref, rest = rest[0], rest[1:]
    if router:
        rw_ref, rb_ref, rest = rest[0], rest[1], rest[2:]
    xo_ref, h_ref = rest[:2]
    md = mod_ref[0, 0]
    if combine:
        route = route_ref[...]
        y = None
        for k in range(TOP_K):
            t = route[:, TOP_K + k:TOP_K + k + 1] * y_refs[k][...].astype(F32)
            y = t if y is None else y + t
    else:
        y = y_refs[0][...]
    z = alpha * x_ref[...] + md[0:1, :] * y
    x1 = _norm(z, LN_EPS) * lnw_ref[...] + lnb_ref[...]
    xo_ref[...] = x1
    h = _norm(x1, LN_EPS) * (1.0 + md[2:3, :]) + md[1:2, :]
    h_ref[...] = h.astype(BF16)
    if router:
        logits = jnp.dot(h, rw_ref[...], preferred_element_type=F32, precision=HI) + rb_ref[...]
        rest[2][...] = _top4_route(logits)


def _row_specs(n_rows, tiles_per_batch, d):
    row = pl.BlockSpec((ROW_TILE, d), lambda i: (i, 0))
    mod = pl.BlockSpec((1, 1, 8, d), lambda i: (i // tiles_per_batch, jnp.minimum(i % tiles_per_batch, 1), 0, 0))
    vec = pl.BlockSpec((1, d), lambda i: (0, 0))
    return row, mod, vec


def _modulate(x, mod, tiles_per_batch):
    n, d = x.shape
    row, modspec, _ = _row_specs(n, tiles_per_batch, d)
    return pl.pallas_call(
        _mod_kernel, grid=(n // ROW_TILE,),
        in_specs=[row, modspec], out_specs=row,
        out_shape=jax.ShapeDtypeStruct((n, d), BF16),
        compiler_params=_cparams("parallel"), name="modulate",
    )(x, mod)


def _resid_ln(x, y, mod, lnw, lnb, tiles_per_batch, alpha, router_w=None, router_b=None, route=None):
    n, d = x.shape
    row, modspec, vec = _row_specs(n, tiles_per_batch, d)
    lanes = pl.BlockSpec((ROW_TILE, LANES), lambda i: (i, 0))
    router = router_w is not None
    combine = route is not None
    ys = list(y) if combine else [y]
    in_specs = [row] + [row] * len(ys) + [modspec, vec, vec]
    args = [x, *ys, mod, lnw.reshape(1, d), lnb.reshape(1, d)]
    out_specs = [row, row]
    out_shape = [jax.ShapeDtypeStruct((n, d), F32), jax.ShapeDtypeStruct((n, d), BF16)]
    if combine:
        in_specs.append(lanes)
        args.append(route)
    if router:
        in_specs += [pl.BlockSpec((d, LANES), lambda i: (0, 0)), pl.BlockSpec((1, LANES), lambda i: (0, 0))]
        args += [router_w, router_b]
        out_specs.append(lanes)
        out_shape.append(jax.ShapeDtypeStruct((n, LANES), F32))
    return pl.pallas_call(
        functools.partial(_resid_kernel, alpha=alpha, router=router, combine=combine),
        grid=(n // ROW_TILE,), in_specs=in_specs, out_specs=out_specs, out_shape=out_shape,
        compiler_params=_cparams("parallel"), name="resid_ln",
    )(*args)


def _matmul_kernel(x_ref, w_ref, o_ref):
    o_ref[...] = jnp.dot(x_ref[...], w_ref[...], preferred_element_type=F32).astype(o_ref.dtype)


def _pick_tm(n, cap):
    tm = cap
    while n % tm:
        tm -= 16
    return tm


def _matmul(x, w, tn, out_dtype=F32, tm_cap=1536):
    n, k = x.shape
    _, m = w.shape
    tm = _pick_tm(n, tm_cap)
    return pl.pallas_call(
        _matmul_kernel, grid=(n // tm, m // tn),
        in_specs=[pl.BlockSpec((tm, k), lambda i, j: (i, 0)), pl.BlockSpec((k, tn), lambda i, j: (0, j))],
        out_specs=pl.BlockSpec((tm, tn), lambda i, j: (i, j)),
        out_shape=jax.ShapeDtypeStruct((n, m), out_dtype),
        compiler_params=_cparams("parallel", "parallel"), name="matmul",
    )(x, w)


def _in_proj_kernel(x_ref, w_ref, o_ref):
    o_ref[...] = jnp.dot(x_ref[...], w_ref[...].astype(BF16), preferred_element_type=F32)


def _in_proj(x, w, layer):
    n, k = x.shape
    tm = _pick_tm(n, 2816)
    nblk = D_IN_PROJ // IN_PROJ_TN
    return pl.pallas_call(
        _in_proj_kernel, grid=(n // tm, nblk),
        in_specs=[pl.BlockSpec((tm, k), lambda i, j: (i, 0)),
                  pl.BlockSpec((None, k, IN_PROJ_TN), lambda i, j: (layer, 0, (j + IN_PROJ_ROT) % nblk))],
        out_specs=pl.BlockSpec((tm, IN_PROJ_TN), lambda i, j: (i, j)),
        out_shape=jax.ShapeDtypeStruct((n, D_IN_PROJ), F32),
        compiler_params=_cparams("parallel", "parallel"), name="in_proj",
    )(x, w)


def _merge_kernel(ya_ref, yb_ref, yn_ref, g0_ref, g1_ref, g2_ref, wb_ref, o_ref):
    acc = None
    for i, (y_ref, g_ref) in enumerate(((ya_ref, g0_ref), (yb_ref, g1_ref), (yn_ref, g2_ref))):
        z = jnp.dot(y_ref[...], wb_ref[i], preferred_element_type=F32)
        t = jax.nn.sigmoid(g_ref[...]) * z
        acc = t if acc is None else acc + t
    o_ref[...] = acc.astype(o_ref.dtype)


def _merge(ya, yb, yn, p, w_branch):
    n, mw = ya.shape
    d = w_branch.shape[-1]
    tn = 256
    tm = _pick_tm(n, 768)
    gate_blk = OFF_GATE // tn
    y_spec = pl.BlockSpec((tm, mw), lambda i, j: (i, 0))
    g_specs = [pl.BlockSpec((tm, tn), functools.partial(lambda i, j, b: (i, gate_blk + b * (d // tn) + j), b=b))
               for b in range(N_BRANCH)]
    return pl.pallas_call(
        _merge_kernel, grid=(n // tm, d // tn),
        in_specs=[y_spec, y_spec, y_spec, *g_specs, pl.BlockSpec((N_BRANCH, mw, tn), lambda i, j: (0, 0, j))],
        out_specs=pl.BlockSpec((tm, tn), lambda i, j: (i, j)),
        out_shape=jax.ShapeDtypeStruct((n, d), BF16),
        compiler_params=_cparams("parallel", "parallel"), name="merge",
    )(ya, yb, yn, p, p, p, w_branch)


MOE_TM = 512


def _moe_dispatch(route, n_exp):
    n = route.shape[0]
    n_slots = n * TOP_K
    n_tiles = (n_slots + n_exp * (MOE_TM - 1)) // MOE_TM + 1
    i32 = jnp.int32
    experts = jnp.arange(n_exp, dtype=i32)
    flat = route[:, :TOP_K].astype(i32).reshape(n_slots)
    order = jnp.argsort(flat, stable=True).astype(i32)
    rank = jnp.argsort(order).astype(i32)
    onehot = (flat[:, None] == experts[None, :]).astype(i32)
    counts = jnp.sum(onehot, axis=0)
    padded = ((counts + MOE_TM - 1) // MOE_TM) * MOE_TM
    pad_end = jnp.cumsum(padded)
    pad_start = pad_end - padded
    start = jnp.cumsum(counts) - counts
    pos_of_slot = jnp.sum(onehot * (pad_start - start)[None, :], axis=1) + rank
    n_used = (pad_end[-1] // MOE_TM).astype(i32)
    tile_start = jnp.minimum(jnp.arange(n_tiles, dtype=i32), n_used - 1) * MOE_TM
    tile_expert = jnp.minimum(jnp.sum((tile_start[:, None] >= pad_end[None, :]).astype(i32), axis=1), n_exp - 1)
    tile_hot = (tile_expert[:, None] == experts[None, :]).astype(i32)
    row = jnp.arange(n_tiles, dtype=i32)[:, None] * MOE_TM + jnp.arange(MOE_TM, dtype=i32)[None, :]
    off = row - jnp.sum(tile_hot * pad_start[None, :], axis=1)[:, None]
    valid = (off < jnp.sum(tile_hot * counts[None, :], axis=1)[:, None]) & (row < n_used * MOE_TM)
    src = jnp.clip(jnp.sum(tile_hot * start[None, :], axis=1)[:, None] + off, 0, n_slots - 1)
    tok_of_pos = jnp.where(valid, jnp.take(order // TOP_K, src, mode="clip"), row % n).reshape(-1)
    return tok_of_pos, pos_of_slot.reshape(n, TOP_K), tile_expert, n_used.reshape(1)


def _moe_kernel(te_ref, nu_ref, x_ref, w1_ref, b1_ref, w2_ref, b2_ref, o_ref, w1_scr, w2_scr):
    t = pl.program_id(0)
    live = t < nu_ref[0]
    new_expert = (t == 0) | (te_ref[t] != te_ref[jnp.maximum(t - 1, 0)])

    @pl.when(live & new_expert)
    def _():
        w1_scr[...] = w1_ref[...].astype(BF16)
        w2_scr[...] = w2_ref[...].astype(BF16)

    @pl.when(live)
    def _():
        hu = jnp.dot(x_ref[...], w1_scr[...], preferred_element_type=F32) + b1_ref[0]
        f = hu.shape[1] // 2
        g = jnp.minimum(hu[:, :f], SWIGLU_LIMIT)
        u = jnp.clip(hu[:, f:], -SWIGLU_LIMIT, SWIGLU_LIMIT)
        act = g * jax.nn.sigmoid(SWIGLU_ALPHA * g) * (u + 1.0)
        y = jnp.dot(act.astype(BF16), w2_scr[...], preferred_element_type=F32) + b2_ref[0]
        o_ref[...] = y.astype(o_ref.dtype)

    @pl.when(jnp.logical_not(live))
    def _():
        o_ref[...] = jnp.zeros_like(o_ref)


def _moe(xs, tile_expert, n_used, w1, b1, w2, b2, layer):
    p, d = xs.shape
    _, n_exp, _, f2 = w1.shape
    grid_spec = pltpu.PrefetchScalarGridSpec(
        num_scalar_prefetch=2, grid=(p // MOE_TM,),
        in_specs=[pl.BlockSpec((MOE_TM, d), lambda t, te, nu: (t, 0)),
                  pl.BlockSpec((None, None, d, f2), lambda t, te, nu: (layer, te[t], 0, 0)),
                  pl.BlockSpec((1, 1, f2), lambda t, te, nu: (te[t], 0, 0)),
                  pl.BlockSpec((None, None, f2 // 2, d), lambda t, te, nu: (layer, te[t], 0, 0)),
                  pl.BlockSpec((1, 1, d), lambda t, te, nu: (te[t], 0, 0))],
        out_specs=pl.BlockSpec((MOE_TM, d), lambda t, te, nu: (t, 0)),
        scratch_shapes=[pltpu.VMEM((d, f2), BF16), pltpu.VMEM((f2 // 2, d), BF16)])
    return pl.pallas_call(
        _moe_kernel, grid_spec=grid_spec,
        out_shape=jax.ShapeDtypeStruct((p, d), BF16),
        compiler_params=_cparams("arbitrary"), name="moe",
    )(tile_expert, n_used, xs, w1, b1.reshape(n_exp, 1, f2), w2, b2.reshape(n_exp, 1, d))


def _dot_nt(a, b, precision=None):
    return lax.dot_general(a, b, (((1,), (1,)), ((), ())), preferred_element_type=F32, precision=precision)


def _dot_tn(a, b, precision=None):
    return lax.dot_general(a, b, (((0,), (0,)), ((), ())), preferred_element_type=F32, precision=precision)


def _scan_chunk(d, i, n_chunks, n_ctx_chunks):
    rev = jnp.where(i < n_ctx_chunks, n_ctx_chunks - 1 - i, n_chunks - 1 + n_ctx_chunks - i)
    return jnp.where(d == 0, i, rev)


def _head_sum_mats():
    ch = jnp.arange(MIX_W) // RWKV_HD
    e = (ch[:, None] == jnp.arange(LANES)[None, :]).astype(BF16)
    return e, e.T


def _dot_split(x, m_bf16):
    hi = x.astype(BF16)
    lo = (x - hi.astype(F32)).astype(BF16)
    return (jnp.dot(hi, m_bf16, preferred_element_type=F32) + jnp.dot(lo, m_bf16, preferred_element_type=F32))


def _rwkv_prep_kernel(*refs, tiles_per_batch):
    pieces, rest = [refs[3 * j:3 * j + 3] for j in range(4)], refs[12:]
    (shift_ref, w0_ref, wb_ref, a0_ref, ab_ref, gb_ref, kk_ref, ka_ref, rk_ref, e_ref, et_ref,
     r_out, v_out, kk_out, g_out, bonus_out, lw_out, b_out, kd_out) = rest
    i = pl.program_id(1)
    rows = lax.broadcasted_iota(jnp.int32, (ROW_TILE, 1), 0)
    has_prev = jnp.where(i > 1, 1.0, 0.0)
    has_next = jnp.where((i > 0) & (i < tiles_per_batch - 1), 1.0, 0.0)

    def token_shift(piece, col0):
        p_ref, pp_ref, pn_ref = piece
        x = p_ref[0]
        w = x.shape[1]
        xp = jnp.where(rows == 0, pp_ref[0, 7:8, :] * has_prev, pltpu.roll(x, 1, 0))
        xn = jnp.where(rows == ROW_TILE - 1, pn_ref[0, 0:1, :] * has_next, pltpu.roll(x, ROW_TILE - 1, 0))
        sh = shift_ref[:, col0:col0 + w]
        return xp * sh[0:1, :] + x * sh[1:2, :] + xn * sh[2:3, :]

    m = MIX_W
    r, k, v = (token_shift(pieces[j], j * m) for j in range(3))
    lora = token_shift(pieces[3], 3 * m)
    wl = lora[:, :RWKV_DECAY_RANK]
    al = lora[:, RWKV_DECAY_RANK:RWKV_DECAY_RANK + RWKV_AAA_RANK]
    gl = lora[:, RWKV_DECAY_RANK + RWKV_AAA_RANK:]
    dot = functools.partial(jnp.dot, preferred_element_type=F32, precision=HI)
    e, et = e_ref[...], et_ref[...]
    g_out[0] = dot(jax.nn.sigmoid(gl), gb_ref[...])
    kk0 = k * kk_ref[...]
    nrm = jnp.maximum(jnp.sqrt(_dot_split(kk0 * kk0, e)), 1e-12)
    kk = kk0 * _dot_split(1.0 / nrm, et)
    tw = jnp.tanh(wl)
    bsum = None
    for d in range(2):
        a = jax.nn.sigmoid(a0_ref[d:d + 1, :] + dot(al, ab_ref[d]))
        kd = k * (1.0 + (a - 1.0) * ka_ref[...])
        lw = (-math.exp(-0.5)) * jax.nn.sigmoid(w0_ref[d:d + 1, :] + dot(tw, wb_ref[d]))
        lw_out[d, 0] = lw
        b_out[d, 0] = kk * a
        kd_out[d, 0] = kd
        t = _dot_split(r * kd * rk_ref[...], e)
        bsum = t if bsum is None else bsum + t
    r_out[0] = r
    v_out[0] = v
    kk_out[0] = kk
    bonus_out[0] = _dot_split(bsum, et) * v


def _rwkv_prep(p3, shift, w0, wb, a0, ab, gb, k_k, k_a, r_k, ctx_len):
    b, tt, _ = p3.shape
    tpb = tt // ROW_TILE
    assert ctx_len == ROW_TILE
    e, et = _head_sum_mats()
    hb = ROW_TILE // 8
    full = lambda shape: pl.BlockSpec(shape, lambda bb, i: (0,) * len(shape))
    row1 = pl.BlockSpec((1, ROW_TILE, MIX_W), lambda bb, i: (bb, i, 0))
    row2 = pl.BlockSpec((2, 1, ROW_TILE, MIX_W), lambda bb, i: (0, bb, i, 0))
    s1 = jax.ShapeDtypeStruct((b, tt, MIX_W), F32)
    s2 = jax.ShapeDtypeStruct((2, b, tt, MIX_W), F32)
    vec = lambda a: a.reshape(1, MIX_W)

    def piece(width, col_block):
        return [pl.BlockSpec((1, ROW_TILE, width), lambda bb, i: (bb, i, col_block)),
                pl.BlockSpec((1, 8, width), lambda bb, i: (bb, jnp.maximum(i * hb - 1, 0), col_block)),
                pl.BlockSpec((1, 8, width), lambda bb, i: (bb, jnp.minimum((i + 1) * hb, tpb * hb - 1), col_block))]

    pieces = [s for j in range(3) for s in piece(MIX_W, OFF_RWKV // MIX_W + j)] + piece(RWKV_LORA_W, OFF_RWKV_LORA // RWKV_LORA_W)
    return pl.pallas_call(
        functools.partial(_rwkv_prep_kernel, tiles_per_batch=tpb), grid=(b, tpb),
        in_specs=pieces + [full((3, RWKV_SHIFT_W)), full((2, MIX_W)), full((2, RWKV_DECAY_RANK, MIX_W)), full((2, MIX_W)),
                           full((2, RWKV_AAA_RANK, MIX_W)), full((RWKV_GATE_RANK, MIX_W)),
                           full((1, MIX_W)), full((1, MIX_W)), full((1, MIX_W)), full((MIX_W, LANES)), full((LANES, MIX_W))],
        out_specs=[row1, row1, row1, row1, row1, row2, row2, row2],
        out_shape=[s1, s1, s1, s1, s1, s2, s2, s2],
        compiler_params=_cparams("parallel", "parallel"), name="rwkv_prep",
    )(*([p3] * 12), shift, w0, wb, a0, ab, gb, vec(k_k), vec(k_a), vec(r_k), e, et)


def _rwkv_scan_kernel(r0_ref, r1_ref, v0_ref, v1_ref, kk0_ref, kk1_ref, lw0_ref, lw1_ref, b0_ref, b1_ref,
                      kd0_ref, kd1_ref, y0_ref, y1_ref, s_scr):
    i = pl.program_id(0)
    c = RWKV_CHUNK
    n_b = r0_ref.shape[0]

    @pl.when(i == 0)
    def _():
        s_scr[...] = jnp.zeros_like(s_scr)

    dot = functools.partial(jnp.dot, preferred_element_type=F32, precision=HI)
    bf = lambda x: x.astype(BF16)
    mm = lambda x, y: jnp.dot(bf(x), bf(y), preferred_element_type=F32)
    ti = lax.broadcasted_iota(jnp.int32, (c, c), 0)
    si = lax.broadcasted_iota(jnp.int32, (c, c), 1)
    ri = lax.broadcasted_iota(jnp.int32, (LANES, LANES), 0)
    ci = lax.broadcasted_iota(jnp.int32, (LANES, LANES), 1)
    same = (ri // c) == (ci // c)
    lane = lax.broadcasted_iota(jnp.int32, (1, LANES), 1)
    m0 = (lane < RWKV_HD).astype(F32)
    m1 = 1.0 - m0
    stack = lambda x: jnp.concatenate([x * m0, x * m1], axis=0)
    unstack = lambda x: x[:c] + x[c:]
    pairs = range(RWKV_HEADS // 2)
    sls = [slice(p * LANES, (p + 1) * LANES) for p in pairs]

    tri, strict, incl = [], [], []
    for sgn in (1, -1):
        tri.append(jnp.where((ti - si) * sgn >= 0, 1.0, 0.0))
        strict.append(same & ((ri - ci) * sgn > 0))
        incl.append(same & ((ri - ci) * sgn >= 0))

    refs = ((r0_ref, v0_ref, kk0_ref, lw0_ref, b0_ref, kd0_ref, y0_ref),
            (r1_ref, v1_ref, kk1_ref, lw1_ref, b1_ref, kd1_ref, y1_ref))
    streams = [(d, bi) for d in range(2) for bi in range(n_b)]
    a_t, b_t, k_t, r_t, v_all, g_tot = [], [], [], [], [], []
    for d, bi in streams:
        r_ref, v_ref, kk_ref, lw_ref, b_ref, kd_ref, _ = refs[d]
        lw = lw_ref[0, bi]
        cum = dot(tri[d], lw)
        e_neg = jnp.exp(-cum)
        a_t.append(-kk_ref[bi] * jnp.exp(cum - lw))
        b_t.append(b_ref[0, bi] * e_neg)
        k_t.append(kd_ref[0, bi] * e_neg)
        r_t.append(r_ref[bi] * jnp.exp(cum))
        v_all.append(v_ref[bi])
        g_tot.append(jnp.exp(jnp.sum(lw, axis=0, keepdims=True)))

    keys = [(s, p) for s in range(len(streams)) for p in pairs]
    dirs = [streams[s][0] for s, _ in keys]
    a_st = [bf(stack(a_t[s][:, sls[p]])) for s, p in keys]
    r_st = [bf(stack(r_t[s][:, sls[p]])) for s, p in keys]
    v_st = [bf(stack(v_all[s][:, sls[p]])) for s, p in keys]
    b_p = [bf(b_t[s][:, sls[p]]) for s, p in keys]
    k_p = [bf(k_t[s][:, sls[p]]) for s, p in keys]
    ks = range(len(keys))
    gm = [_dot_nt(jnp.concatenate([a_st[j], r_st[j]], axis=0),
                  jnp.concatenate([b_p[j], b_p[j], k_p[j], k_p[j]], axis=0)) for j in ks]
    m_ak = [bf(jnp.where(strict[dirs[j]], gm[j][:LANES, LANES:], 0.0)) for j in ks]
    x = [jnp.concatenate([a_st[j].astype(F32), jnp.dot(m_ak[j], v_st[j], preferred_element_type=F32)], axis=1)
         for j in ks]
    pw = [jnp.where(strict[dirs[j]], gm[j][:LANES, :LANES], 0.0) for j in ks]
    for step in range(6):
        if step:
            pw = [mm(pw[j], pw[j]) for j in ks]
        x = [x[j] + mm(pw[j], x[j]) for j in ks]
    s_bd = [s_scr[streams[s][0], streams[s][1], p] for s, p in keys]
    s_bf = [bf(t) for t in s_bd]
    u_st = [_dot_nt(bf(x[j][:, :LANES]), s_bf[j]) + x[j][:, LANES:] for j in ks]
    m_rb = [bf(jnp.where(incl[dirs[j]], gm[j][LANES:, :LANES], 0.0)) for j in ks]
    m_rk = [bf(jnp.where(incl[dirs[j]], gm[j][LANES:, LANES:], 0.0)) for j in ks]
    for j, (s, p) in enumerate(keys):
        y_st = (_dot_nt(r_st[j], s_bf[j]) + jnp.dot(m_rb[j], bf(u_st[j]), preferred_element_type=F32)
                + jnp.dot(m_rk[j], v_st[j], preferred_element_type=F32))
        refs[streams[s][0]][6][streams[s][1], :, sls[p]] = unstack(y_st)
    for j, (s, p) in enumerate(keys):
        uv = jnp.concatenate([bf(unstack(u_st[j])), bf(v_all[s][:, sls[p]])], axis=0)
        bk = jnp.concatenate([b_p[j], k_p[j]], axis=0)
        upd = _dot_tn(uv, bk)
        s_scr[streams[s][0], streams[s][1], p] = g_tot[s][:, sls[p]] * (s_bd[j] + jnp.where(same, upd, 0.0))


def _rwkv_scan(r, v, kk, lw, bb, kd, ctx_len):
    b, tt, _ = r.shape
    c = RWKV_CHUNK
    n = tt // c
    nc = ctx_len // c
    shared = [pl.BlockSpec((b, c, MIX_W), functools.partial(lambda i, d: (0, _scan_chunk(d, i, n, nc), 0), d=d))
              for d in range(2)]
    per_dir = [pl.BlockSpec((1, b, c, MIX_W), functools.partial(lambda i, d: (d, 0, _scan_chunk(d, i, n, nc), 0), d=d))
               for d in range(2)]
    out = jax.ShapeDtypeStruct((b, tt, MIX_W), F32)
    return pl.pallas_call(
        _rwkv_scan_kernel, grid=(n,),
        in_specs=[*shared, *shared, *shared, *per_dir, *per_dir, *per_dir], out_specs=shared, out_shape=[out, out],
        scratch_shapes=[pltpu.VMEM((2, b, RWKV_HEADS // 2, LANES, LANES), F32)],
        compiler_params=_cparams("arbitrary"), name="rwkv_scan",
    )(r, r, v, v, kk, kk, lw, lw, bb, bb, kd, kd)


def _rwkv_finish_kernel(y0_ref, y1_ref, bonus_ref, g_ref, w_ref, b_ref, e_ref, et_ref, o_ref):
    e, et = e_ref[...], et_ref[...]
    y = y0_ref[0] + y1_ref[0]
    mu = _dot_split(_dot_split(y, e), et) * (1.0 / RWKV_HD)
    yc = y - mu
    var = _dot_split(_dot_split(yc * yc, e), et) * (1.0 / RWKV_HD)
    yn = yc * lax.rsqrt(var + RWKV_GN_EPS) * w_ref[...] + b_ref[...]
    o_ref[0] = ((yn + bonus_ref[0]) * g_ref[0]).astype(o_ref.dtype)


def _rwkv_finish(y0, y1, bonus, g, gn_w, gn_b):
    b, tt, _ = y0.shape
    e, et = _head_sum_mats()
    row = pl.BlockSpec((1, ROW_TILE, MIX_W), lambda bb, i: (bb, i, 0))
    full = lambda shape: pl.BlockSpec(shape, lambda bb, i: (0,) * len(shape))
    return pl.pallas_call(
        _rwkv_finish_kernel, grid=(b, tt // ROW_TILE),
        in_specs=[row, row, row, row,
                  full((1, MIX_W)), full((1, MIX_W)), full((MIX_W, LANES)), full((LANES, MIX_W))],
        out_specs=row, out_shape=jax.ShapeDtypeStruct((b, tt, MIX_W), BF16),
        compiler_params=_cparams("parallel", "parallel"), name="rwkv_finish",
    )(y0, y1, bonus, g, gn_w.reshape(1, MIX_W), gn_b.reshape(1, MIX_W), e, et)


def _rope_swap(x):
    lane = lax.broadcasted_iota(jnp.int32, x.shape, 1)
    return jnp.where((lane % 64) < 32, pltpu.roll(x, 96, 1), pltpu.roll(x, 32, 1))


def _rope_tables(t, ctx_len):
    pos = jnp.arange(t, dtype=jnp.int32)
    nf = RET_QK_HD // 4
    inv = ROPE_BASE ** (-jnp.arange(nf, dtype=F32) / nf)
    ang_r = (pos // GRID_W).astype(F32)[:, None] * inv[None, :]
    ang_c = (pos % GRID_W).astype(F32)[:, None] * inv[None, :]
    cos = jnp.concatenate([jnp.cos(ang_r)] * 2 + [jnp.cos(ang_c)] * 2, axis=-1)
    sin = jnp.concatenate([-jnp.sin(ang_r), jnp.sin(ang_r), -jnp.sin(ang_c), jnp.sin(ang_c)], axis=-1)
    cos = jnp.concatenate([jnp.ones((ctx_len, RET_QK_HD), F32), cos], axis=0)
    sin = jnp.concatenate([jnp.zeros((ctx_len, RET_QK_HD), F32), sin], axis=0)
    return cos, sin


def _ret_kernel(dec_ref, q_ref, k_ref, v_ref, cos_ref, sin_ref, o_ref, r_scr):
    d = pl.program_id(0)
    i = pl.program_id(2)
    c = RET_CHUNK

    @pl.when(i == 0)
    def _():
        r_scr[...] = jnp.zeros_like(r_scr)

    bf = lambda x: x.astype(BF16)
    cos, sin = cos_ref[...], sin_ref[...]
    sign = (1 - 2 * d).astype(F32)
    ii = lax.broadcasted_iota(jnp.int32, (c, c), 0).astype(F32)
    jj = lax.broadcasted_iota(jnp.int32, (c, c), 1).astype(F32)
    pos = jnp.where(d == 0, ii, c - 1.0 - ii)
    diff = (ii - jj) * sign
    heads = range(RET_HEADS)
    qs = [slice(h * RET_QK_HD, (h + 1) * RET_QK_HD) for h in heads]
    vs = [slice(h * RET_V_HD, (h + 1) * RET_V_HD) for h in heads]
    lg = [jnp.log(jax.nn.sigmoid(jnp.full((c, c), dec_ref[d, h], F32))) for h in heads]
    dmat = [jnp.where(diff >= 0, jnp.exp(lg[h] * jnp.maximum(diff, 0.0)), 0.0) for h in heads]
    q_dec = [jnp.exp(lg[h] * (pos + 1.0)) for h in heads]
    k_dec = [jnp.exp(lg[h] * (c - 1.0 - pos)) for h in heads]
    c_dec = [jnp.exp(lg[h][:1, :1] * float(c)) for h in heads]
    q = [q_ref[0, :, qs[h]] for h in heads]
    k = [k_ref[0, :, qs[h]] for h in heads]
    q = [q[h] * cos + _rope_swap(q[h]) * sin for h in heads]
    k = [(k[h] * cos + _rope_swap(k[h]) * sin) * (RET_QK_HD ** -0.5) for h in heads]
    v = [bf(v_ref[0, :, vs[h]]) for h in heads]
    r = [r_scr[h] for h in heads]
    s = [_dot_nt(bf(q[h]), bf(k[h])) * dmat[h] for h in heads]
    for h in heads:
        o = jnp.dot(bf(s[h]), v[h], preferred_element_type=F32)
        o_ref[0, 0, :, vs[h]] = o + jnp.dot(bf(q[h] * q_dec[h]), bf(r[h]), preferred_element_type=F32)
    for h in heads:
        r_scr[h] = r[h] * c_dec[h] + _dot_tn(bf(k[h] * k_dec[h]), v[h])


def _retention(p3, cos, sin, decay, ctx_len):
    b, tt, _ = p3.shape
    c = RET_CHUNK
    n = tt // c
    nc = ctx_len // c
    qb, kb, vb = OFF_RET_Q // RET_QK_W, OFF_RET_K // RET_QK_W, OFF_RET_V // MIX_W
    tok = lambda d, i: _scan_chunk(d, i, n, nc)
    grid_spec = pltpu.PrefetchScalarGridSpec(
        num_scalar_prefetch=1, grid=(2, b, n),
        in_specs=[pl.BlockSpec((1, c, RET_QK_W), lambda d, bb, i, dec: (bb, tok(d, i), qb)),
                  pl.BlockSpec((1, c, RET_QK_W), lambda d, bb, i, dec: (bb, tok(d, i), kb)),
                  pl.BlockSpec((1, c, MIX_W), lambda d, bb, i, dec: (bb, tok(d, i), vb)),
                  pl.BlockSpec((c, RET_QK_HD), lambda d, bb, i, dec: (tok(d, i), 0)),
                  pl.BlockSpec((c, RET_QK_HD), lambda d, bb, i, dec: (tok(d, i), 0))],
        out_specs=pl.BlockSpec((1, 1, c, MIX_W), lambda d, bb, i, dec: (d, bb, tok(d, i), 0)),
        scratch_shapes=[pltpu.VMEM((RET_HEADS, RET_QK_HD, RET_V_HD), F32)])
    return pl.pallas_call(
        _ret_kernel, grid_spec=grid_spec,
        out_shape=jax.ShapeDtypeStruct((2, b, tt, MIX_W), F32),
        compiler_params=_cparams("parallel", "parallel", "arbitrary"), name="retention",
    )(decay, p3, p3, p3, cos, sin)


def _ret_finish_kernel(o_ref, g_ref, w_ref, b_ref, y_ref):
    o = o_ref[0, 0] + o_ref[1, 0]
    y = _norm(o, RET_GN_EPS) * w_ref[...] + b_ref[...]
    g = g_ref[0]
    y_ref[0] = (y * (g * jax.nn.sigmoid(g))).astype(y_ref.dtype)


def _ret_finish(o, p3, gn_w, gn_b):
    _, b, tt, _ = o.shape
    gb = OFF_RET_G // RET_V_HD
    return pl.pallas_call(
        _ret_finish_kernel, grid=(b, tt // ROW_TILE, RET_HEADS),
        in_specs=[pl.BlockSpec((2, 1, ROW_TILE, RET_V_HD), lambda bb, i, h: (0, bb, i, h)),
                  pl.BlockSpec((1, ROW_TILE, RET_V_HD), lambda bb, i, h: (bb, i, gb + h)),
                  pl.BlockSpec((1, RET_V_HD), lambda bb, i, h: (0, h)),
                  pl.BlockSpec((1, RET_V_HD), lambda bb, i, h: (0, h))],
        out_specs=pl.BlockSpec((1, ROW_TILE, RET_V_HD), lambda bb, i, h: (bb, i, h)),
        out_shape=jax.ShapeDtypeStruct((b, tt, MIX_W), BF16),
        compiler_params=_cparams("parallel", "parallel", "parallel"), name="ret_finish",
    )(o, p3, gn_w.reshape(1, MIX_W), gn_b.reshape(1, MIX_W))


NA_BLOCK_ROWS = 4
NA_BLOCK_TOK = NA_BLOCK_ROWS * GRID_W
NA_WIN_TOK = NA_WIN_ROWS * GRID_W


def _softmax_pv(s_parts, v_parts):
    m = None
    for s in s_parts:
        mm = jnp.max(s, axis=1, keepdims=True)
        m = mm if m is None else jnp.maximum(m, mm)
    den, acc = None, None
    for s, v in zip(s_parts, v_parts):
        e = jnp.exp(s - m)
        dd = jnp.sum(e, axis=1, keepdims=True)
        pv = jnp.dot(e.astype(BF16), v, preferred_element_type=F32)
        den = dd if den is None else den + dd
        acc = pv if acc is None else acc + pv
    return acc / den


def _na_kernel(q_ref, kp_ref, kc_ref, kn_ref, vp_ref, vc_ref, vn_ref, kx_ref, vx_ref, bias_ref, o_ref,
               k_scr, v_scr, *, n_blocks):
    rb = pl.program_id(1)
    n_b = q_ref.shape[0]
    batch = range(n_b)
    lane = lax.broadcasted_iota(jnp.int32, (1, LANES), 1)
    head_masks = [(lane < NA_HD).astype(F32), (lane >= NA_HD).astype(F32)]
    heads = range(2)
    kx = [kx_ref[bi].astype(BF16) for bi in batch]
    vx = [vx_ref[bi].astype(BF16) for bi in batch]
    scale = NA_HD ** -0.5

    @pl.when(rb == 0)
    def _():
        for bi in batch:
            q = q_ref[bi] * scale
            out = jnp.zeros((NA_BLOCK_TOK, LANES), F32)
            for hm in head_masks:
                s = _dot_nt((q * hm).astype(BF16), kx[bi])
                out = out + _softmax_pv([s], [vx[bi]]) * hm
            o_ref[bi] = out.astype(o_ref.dtype)

    @pl.when(rb > 0)
    def _():
        t = NA_BLOCK_TOK
        for bi in batch:
            k_scr[bi, 0:t, :] = kp_ref[bi].astype(BF16)
            k_scr[bi, t:2 * t, :] = kc_ref[bi].astype(BF16)
            k_scr[bi, 2 * t:3 * t, :] = kn_ref[bi].astype(BF16)
            v_scr[bi, 0:t, :] = vp_ref[bi].astype(BF16)
            v_scr[bi, t:2 * t, :] = vc_ref[bi].astype(BF16)
            v_scr[bi, 2 * t:3 * t, :] = vn_ref[bi].astype(BF16)
        first = rb == 1
        last = rb == n_blocks
        rows = range(NA_BLOCK_ROWS)
        rs = [slice(j * GRID_W, (j + 1) * GRID_W) for j in rows]
        starts, oi = [], []
        for j in rows:
            off = jnp.where(last, 0, jnp.where(first, NA_BLOCK_ROWS, j))
            oi.append(jnp.where(last, NA_BLOCK_ROWS + j, jnp.where(first, j, NA_WIN_ROWS // 2)))
            starts.append(pl.multiple_of(off * GRID_W, GRID_W))
        kw = [[k_scr[bi, pl.ds(starts[j], NA_WIN_TOK), :] for j in rows] for bi in batch]
        vw = [[v_scr[bi, pl.ds(starts[j], NA_WIN_TOK), :] for j in rows] for bi in batch]
        bias = [[bias_ref[oi[j], h] for h in heads] for j in rows]
        qh = [[(q_ref[bi] * scale * hm).astype(BF16) for hm in head_masks] for bi in batch]
        bh = [(bi, h) for bi in batch for h in heads]
        bjh = [(bi, j, h) for bi in batch for j in rows for h in heads]
        s_ctx = {(bi, h): _dot_nt(qh[bi][h], kx[bi]) for bi, h in bh}
        s_loc = {(bi, j, h): _dot_nt(qh[bi][h][rs[j]], kw[bi][j]) + bias[j][h] for bi, j, h in bjh}
        m_ctx = {k: jnp.max(s_ctx[k], axis=1, keepdims=True) for k in bh}
        m = {(bi, j, h): jnp.maximum(jnp.max(s_loc[bi, j, h], axis=1, keepdims=True), m_ctx[bi, h][rs[j]])
             for bi, j, h in bjh}
        m_all = {(bi, h): jnp.concatenate([m[bi, j, h] for j in rows], axis=0) for bi, h in bh}
        e_ctx = {k: jnp.exp(s_ctx[k] - m_all[k]) for k in bh}
        e_loc = {k: jnp.exp(s_loc[k] - m[k]) for k in bjh}
        d_ctx = {k: jnp.sum(e_ctx[k], axis=1, keepdims=True) for k in bh}
        pv_ctx = {(bi, h): jnp.dot(e_ctx[bi, h].astype(BF16), vx[bi], preferred_element_type=F32) for bi, h in bh}
        for bi in batch:
            for j in rows:
                out = None
                for h in heads:
                    den = jnp.sum(e_loc[bi, j, h], axis=1, keepdims=True) + d_ctx[bi, h][rs[j]]
                    pv = jnp.dot(e_loc[bi, j, h].astype(BF16), vw[bi][j], preferred_element_type=F32) + pv_ctx[bi, h][rs[j]]
                    term = pv / den * head_masks[h]
                    out = term if out is None else out + term
                o_ref[bi, rs[j], :] = out.astype(o_ref.dtype)


def _na_bias_table(rpb):
    col = jnp.arange(GRID_W)
    cs = jnp.clip(col - NA_WIN_COLS // 2, 0, GRID_W - NA_WIN_COLS)
    col_ok = (col[None, :] >= cs[:, None]) & (col[None, :] < cs[:, None] + NA_WIN_COLS)
    col_idx = jnp.clip(col[None, :] - col[:, None] + (NA_WIN_COLS - 1), 0, 2 * NA_WIN_COLS - 2)
    tabs = []
    for oi in range(NA_WIN_ROWS):
        row_idx = jnp.arange(NA_WIN_ROWS) - oi + (NA_WIN_ROWS - 1)
        bias = rpb[:, row_idx][:, :, col_idx]
        bias = jnp.where(col_ok[None, None], bias, -1e30)
        tabs.append(bias.transpose(0, 2, 1, 3).reshape(NA_HEADS, GRID_W, NA_WIN_TOK))
    return jnp.stack(tabs, 0).astype(F32)


def _na(p3, bias_tab, ctx_len):
    b, tt, _ = p3.shape
    assert ctx_len == NA_BLOCK_TOK
    t = tt - ctx_len
    n_rows = t // GRID_W
    assert n_rows % NA_BLOCK_ROWS == 0 and n_rows >= NA_WIN_ROWS
    nb = n_rows // NA_BLOCK_ROWS
    qb, kb, vb = OFF_NA_Q // LANES, OFF_NA_K // LANES, OFF_NA_V // LANES
    blk = (b, NA_BLOCK_TOK, LANES)

    def spec(colb, shift):
        return pl.BlockSpec(blk, lambda pr, rb: (0, jnp.clip(rb + shift, 1, nb), colb + pr))

    return pl.pallas_call(
        functools.partial(_na_kernel, n_blocks=nb), grid=(NA_HEADS // 2, nb + 1),
        in_specs=[pl.BlockSpec(blk, lambda pr, rb: (0, rb, qb + pr)),
                  spec(kb, -1), spec(kb, 0), spec(kb, 1), spec(vb, -1), spec(vb, 0), spec(vb, 1),
                  pl.BlockSpec(blk, lambda pr, rb: (0, 0, kb + pr)),
                  pl.BlockSpec(blk, lambda pr, rb: (0, 0, vb + pr)),
                  pl.BlockSpec((NA_WIN_ROWS, 2, GRID_W, NA_WIN_TOK), lambda pr, rb: (0, pr, 0, 0))],
        out_specs=pl.BlockSpec(blk, lambda pr, rb: (0, rb, pr)),
        out_shape=jax.ShapeDtypeStruct((b, tt, MIX_W), BF16),
        scratch_shapes=[pltpu.VMEM((b, 3 * NA_BLOCK_TOK, LANES), BF16), pltpu.VMEM((b, 3 * NA_BLOCK_TOK, LANES), BF16)],
        compiler_params=_cparams("parallel", "parallel"), name="na",
    )(p3, p3, p3, p3, p3, p3, p3, p3, p3, bias_tab)


def _build_mod(gate, shift, scale, b):
    rows = jnp.stack([gate, shift, scale], axis=1)
    lat = rows[:b]
    ctx = jnp.broadcast_to(rows[b][None], lat.shape)
    mod = jnp.stack([ctx, lat], axis=1)
    return jnp.pad(mod, ((0, 0), (0, 0), (0, 5), (0, 0)))


def kernel(x, c, ctx, c_ctx, ada_w, ada_b, w_in, rwkv_shift, rwkv_w0, rwkv_wB, rwkv_a0, rwkv_aB, rwkv_gB, rwkv_kk, rwkv_ka, rwkv_rk, rwkv_gn_w, rwkv_gn_b, ret_decay, ret_gn_w, ret_gn_b, na_rpb, w_branch, w_out, ln1_w, ln1_b, ln2_w, ln2_b, router_w, router_b, exp_w1, exp_b1, exp_w2, exp_b2):
    b, t, d = x.shape
    ctx_len = ctx.shape[1]
    tt = ctx_len + t
    n = b * tt
    tpb = tt // ROW_TILE
    depth = ada_w.shape[0]
    alpha = (2 * depth) ** 0.25
    assert d == D_MODEL and b + 1 <= 8 and ctx_len == ROW_TILE and t % ROW_TILE == 0

    xa = jnp.concatenate([ctx, x], axis=1).reshape(n, d)
    cond = jnp.concatenate([c, c_ctx[None], jnp.zeros((8 - b - 1, d), F32)], axis=0)
    ada = _adaln(cond.T, ada_w, ada_b, b + 1).reshape(depth, 8, 6, d)
    cos, sin = _rope_tables(t, ctx_len)
    n_exp = router_w.shape[-1]
    rw_pad = jnp.pad(router_w, ((0, 0), (0, 0), (0, LANES - n_exp)))
    rb_pad = jnp.pad(router_b, ((0, 0), (0, LANES - n_exp)), constant_values=-1e30)
    zero = jnp.zeros((8, d), F32)

    h = _modulate(xa, _build_mod(zero, ada[0, :, 0], ada[0, :, 1], b), tpb)
    for l in range(depth):
        p = _in_proj(h, w_in, l)
        p3 = p.reshape(b, tt, D_IN_PROJ)

        r, v, kk, g, bonus, lw, bb, kd = _rwkv_prep(p3, rwkv_shift[l], rwkv_w0[l], rwkv_wB[l], rwkv_a0[l], rwkv_aB[l],
                                                    rwkv_gB[l], rwkv_kk[l], rwkv_ka[l], rwkv_rk[l], ctx_len)
        y_fwd, y_bwd = _rwkv_scan(r, v, kk, lw, bb, kd, ctx_len)
        ya = _rwkv_finish(y_fwd, y_bwd, bonus, g, rwkv_gn_w[l], rwkv_gn_b[l])

        o_ret = _retention(p3, cos, sin, ret_decay[l], ctx_len)
        yb = _ret_finish(o_ret, p3, ret_gn_w[l], ret_gn_b[l])

        yn = _na(p3, _na_bias_table(na_rpb[l]), ctx_len)

        merged = _merge(ya.reshape(n, MIX_W), yb.reshape(n, MIX_W), yn.reshape(n, MIX_W), p, w_branch[l].astype(BF16))
        y = _matmul(merged, w_out[l].astype(BF16), 512)

        mod_f = _build_mod(ada[l, :, 2], ada[l, :, 3], ada[l, :, 4], b)
        x1, h2, route = _resid_ln(xa, y, mod_f, ln1_w[l], ln1_b[l], tpb, alpha, rw_pad[l], rb_pad[l][None])
        tok_of_pos, pos_of_slot, tile_expert, n_used = _moe_dispatch(route, n_exp)
        xs = jnp.take(h2, tok_of_pos, axis=0, mode="clip")
        ys = _moe(xs, tile_expert, n_used, exp_w1, exp_b1[l], exp_w2, exp_b2[l], l)
        f4 = [jnp.take(ys, pos_of_slot[:, k], axis=0, mode="clip") for k in range(TOP_K)]
        nxt = min(l + 1, depth - 1)
        mod_a = _build_mod(ada[l, :, 5], ada[nxt, :, 0], ada[nxt, :, 1], b)
        xa, h = _resid_ln(x1, f4, mod_a, ln2_w[l], ln2_b[l], tpb, alpha, route=route)
    return xa.reshape(b, tt, d)[:, ctx_len:]
```

```python
import functools
import math

import jax
import jax.numpy as jnp
from jax import lax
from jax.experimental import pallas as pl
from jax.experimental.pallas import tpu as pltpu

F32 = jnp.float32
BF16 = jnp.bfloat16
HI = lax.Precision.HIGHEST

D_MODEL = 2048
GRID_W = 64
MIX_W = D_MODEL // 2
N_BRANCH = 3

RWKV_HD = 64
RWKV_HEADS = MIX_W // RWKV_HD
RWKV_DECAY_RANK = 64
RWKV_AAA_RANK = 64
RWKV_GATE_RANK = 128
RWKV_SHIFT_W = 3 * MIX_W + RWKV_DECAY_RANK + RWKV_AAA_RANK + RWKV_GATE_RANK
RWKV_GN_EPS = 64e-5
RWKV_CHUNK = 64

RET_HEADS = 4
RET_V_HD = MIX_W // RET_HEADS
RET_QK_HD = RET_V_HD // 2
RET_QK_W = RET_HEADS * RET_QK_HD
RET_CHUNK = 128
RET_GN_EPS = 1e-6

NA_HD = 64
NA_HEADS = MIX_W // NA_HD
NA_WIN_ROWS = 8
NA_WIN_COLS = 16

N_EXPERTS = 32
TOP_K = 4
D_EXPERT = D_MODEL // 4
SWIGLU_LIMIT = 7.0
SWIGLU_ALPHA = 1.702

ROPE_BASE = 10000.0
LN_EPS = 1e-6

IN_PROJ_TN = 256
D_IN_PROJ = RWKV_SHIFT_W + 2 * RET_QK_W + 5 * MIX_W + N_BRANCH * D_MODEL
IN_PROJ_ROT = RWKV_SHIFT_W // IN_PROJ_TN
OFF_RET_Q = 0
OFF_RET_K = OFF_RET_Q + RET_QK_W
OFF_RET_V = OFF_RET_K + RET_QK_W
OFF_RET_G = OFF_RET_V + MIX_W
OFF_NA_Q = OFF_RET_G + MIX_W
OFF_NA_K = OFF_NA_Q + MIX_W
OFF_NA_V = OFF_NA_K + MIX_W
OFF_GATE = OFF_NA_V + MIX_W
OFF_RWKV = OFF_GATE + N_BRANCH * D_MODEL
OFF_RWKV_LORA = OFF_RWKV + 3 * MIX_W
RWKV_LORA_W = RWKV_SHIFT_W - 3 * MIX_W

ROW_TILE = 256
LANES = 128
VMEM_LIMIT = 56 * 1024 * 1024


def _cparams(*sem):
    return pltpu.CompilerParams(dimension_semantics=sem, vmem_limit_bytes=VMEM_LIMIT)


def _adaln_kernel(ct_ref, w_ref, b_ref, o_ref, *, n_cond):
    ct = ct_ref[...]
    st = ct * jax.nn.sigmoid(ct)
    w = w_ref[0]
    o_ref[0] = jnp.zeros(o_ref.shape[1:], F32)
    for m in range(n_cond):
        o_ref[0, m:m + 1, :] = jnp.sum(st[:, m:m + 1] * w, axis=0, keepdims=True) + b_ref[0]


def _adaln(cond_t, ada_w, ada_b, n_cond):
    n_layers, d, d6 = ada_w.shape
    tn = 512
    return pl.pallas_call(
        functools.partial(_adaln_kernel, n_cond=n_cond),
        grid=(n_layers, d6 // tn),
        in_specs=[pl.BlockSpec((d, 8), lambda l, j: (0, 0)),
                  pl.BlockSpec((1, d, tn), lambda l, j: (l, 0, j)),
                  pl.BlockSpec((1, 1, tn), lambda l, j: (l, 0, j))],
        out_specs=pl.BlockSpec((1, 8, tn), lambda l, j: (l, 0, j)),
        out_shape=jax.ShapeDtypeStruct((n_layers, 8, d6), F32),
        compiler_params=_cparams("parallel", "parallel"),
        name="adaln",
    )(cond_t, ada_w, ada_b.reshape(n_layers, 1, d6))


def _norm(x, eps):
    mu = jnp.mean(x, axis=-1, keepdims=True)
    xc = x - mu
    var = jnp.mean(xc * xc, axis=-1, keepdims=True)
    return xc * lax.rsqrt(var + eps)


def _top4_route(logits):
    lane = lax.broadcasted_iota(jnp.int32, logits.shape, 1).astype(F32)
    l = logits
    idxs, vals = [], []
    for k in range(TOP_K):
        m = jnp.max(l, axis=1, keepdims=True)
        idx = jnp.min(jnp.where(l == m, lane, float(LANES)), axis=1, keepdims=True)
        l = jnp.where(lane == idx, -jnp.inf, l)
        idxs.append(idx)
        vals.append(m)
    es = [jnp.exp(v - vals[0]) for v in vals]
    den = es[0] + es[1] + es[2] + es[3]
    out = jnp.zeros(logits.shape, F32)
    for k in range(TOP_K):
        out = jnp.where(lane == float(k), idxs[k], out)
        out = jnp.where(lane == float(TOP_K + k), es[k] / den, out)
    return out


def _mod_kernel(x_ref, mod_ref, h_ref):
    md = mod_ref[0, 0]
    h = _norm(x_ref[...], LN_EPS) * (1.0 + md[2:3, :]) + md[1:2, :]
    h_ref[...] = h.astype(BF16)


def _resid_kernel(*refs, alpha, router, combine):
    refs = list(refs)
    x_ref, y_ref, mod_ref, lnw_ref, lnb_ref = refs[:5]
    rest = refs[5:]
    if combine:
        route_ref, rest = rest[0], rest[1:]
    if router:
        rw_ref, rb_ref, rest = rest[0], rest[1], rest[2:]
    xo_ref, h_ref = rest[:2]
    md = mod_ref[0, 0]
    if combine:
        route = route_ref[...]
        y = None
        for k in range(TOP_K):
            t = route[:, TOP_K + k:TOP_K + k + 1] * y_ref[k].astype(F32)
            y = t if y is None else y + t
    else:
        y = y_ref[...]
    z = alpha * x_ref[...] + md[0:1, :] * y
    x1 = _norm(z, LN_EPS) * lnw_ref[...] + lnb_ref[...]
    xo_ref[...] = x1
    h = _norm(x1, LN_EPS) * (1.0 + md[2:3, :]) + md[1:2, :]
    h_ref[...] = h.astype(BF16)
    if router:
        logits = jnp.dot(h, rw_ref[...], preferred_element_type=F32, precision=HI) + rb_ref[...]
        rest[2][...] = _top4_route(logits)


def _row_specs(n_rows, tiles_per_batch, d):
    row = pl.BlockSpec((ROW_TILE, d), lambda i: (i, 0))
    mod = pl.BlockSpec((1, 1, 8, d), lambda i: (i // tiles_per_batch, jnp.minimum(i % tiles_per_batch, 1), 0, 0))
    vec = pl.BlockSpec((1, d), lambda i: (0, 0))
    return row, mod, vec


def _modulate(x, mod, tiles_per_batch):
    n, d = x.shape
    row, modspec, _ = _row_specs(n, tiles_per_batch, d)
    return pl.pallas_call(
        _mod_kernel, grid=(n // ROW_TILE,),
        in_specs=[row, modspec], out_specs=row,
        out_shape=jax.ShapeDtypeStruct((n, d), BF16),
        compiler_params=_cparams("parallel"), name="modulate",
    )(x, mod)


def _resid_ln(x, y, mod, lnw, lnb, tiles_per_batch, alpha, router_w=None, router_b=None, route=None):
    n, d = x.shape
    row, modspec, vec = _row_specs(n, tiles_per_batch, d)
    lanes = pl.BlockSpec((ROW_TILE, LANES), lambda i: (i, 0))
    router = router_w is not None
    combine = route is not None
    y_spec = pl.BlockSpec((TOP_K, ROW_TILE, d), lambda i: (0, i, 0)) if combine else row
    in_specs = [row, y_spec, modspec, vec, vec]
    args = [x, y, mod, lnw.reshape(1, d), lnb.reshape(1, d)]
    out_specs = [row, row]
    out_shape = [jax.ShapeDtypeStruct((n, d), F32), jax.ShapeDtypeStruct((n, d), BF16)]
    if combine:
        in_specs.append(lanes)
        args.append(route)
    if router:
        in_specs += [pl.BlockSpec((d, LANES), lambda i: (0, 0)), pl.BlockSpec((1, LANES), lambda i: (0, 0))]
        args += [router_w, router_b]
        out_specs.append(lanes)
        out_shape.append(jax.ShapeDtypeStruct((n, LANES), F32))
    return pl.pallas_call(
        functools.partial(_resid_kernel, alpha=alpha, router=router, combine=combine),
        grid=(n // ROW_TILE,), in_specs=in_specs, out_specs=out_specs, out_shape=out_shape,
        compiler_params=_cparams("parallel"), name="resid_ln",
    )(*args)


def _matmul_kernel(x_ref, w_ref, o_ref):
    o_ref[...] = jnp.dot(x_ref[...], w_ref[...], preferred_element_type=F32).astype(o_ref.dtype)


def _pick_tm(n, cap):
    tm = cap
    while n % tm:
        tm -= 16
    return tm


def _matmul(x, w, tn, out_dtype=F32, tm_cap=1536):
    n, k = x.shape
    _, m = w.shape
    tm = _pick_tm(n, tm_cap)
    return pl.pallas_call(
        _matmul_kernel, grid=(n // tm, m // tn),
        in_specs=[pl.BlockSpec((tm, k), lambda i, j: (i, 0)), pl.BlockSpec((k, tn), lambda i, j: (0, j))],
        out_specs=pl.BlockSpec((tm, tn), lambda i, j: (i, j)),
        out_shape=jax.ShapeDtypeStruct((n, m), out_dtype),
        compiler_params=_cparams("parallel", "parallel"), name="matmul",
    )(x, w)


def _in_proj_kernel(x_ref, w_ref, o_ref):
    o_ref[...] = jnp.dot(x_ref[...], w_ref[...].astype(BF16), preferred_element_type=F32)


def _in_proj(x, w, layer):
    n, k = x.shape
    tm = _pick_tm(n, 2816)
    nblk = D_IN_PROJ // IN_PROJ_TN
    return pl.pallas_call(
        _in_proj_kernel, grid=(n // tm, nblk),
        in_specs=[pl.BlockSpec((tm, k), lambda i, j: (i, 0)),
                  pl.BlockSpec((None, k, IN_PROJ_TN), lambda i, j: (layer, 0, (j + IN_PROJ_ROT) % nblk))],
        out_specs=pl.BlockSpec((tm, IN_PROJ_TN), lambda i, j: (i, j)),
        out_shape=jax.ShapeDtypeStruct((n, D_IN_PROJ), F32),
        compiler_params=_cparams("parallel", "parallel"), name="in_proj",
    )(x, w)


def _merge_kernel(ya_ref, yb_ref, yn_ref, g0_ref, g1_ref, g2_ref, wb_ref, o_ref):
    acc = None
    for i, (y_ref, g_ref) in enumerate(((ya_ref, g0_ref), (yb_ref, g1_ref), (yn_ref, g2_ref))):
        z = jnp.dot(y_ref[...], wb_ref[i], preferred_element_type=F32)
        t = jax.nn.sigmoid(g_ref[...]) * z
        acc = t if acc is None else acc + t
    o_ref[...] = acc.astype(o_ref.dtype)


def _merge(ya, yb, yn, p, w_branch):
    n, mw = ya.shape
    d = w_branch.shape[-1]
    tn = 256
    tm = _pick_tm(n, 768)
    gate_blk = OFF_GATE // tn
    y_spec = pl.BlockSpec((tm, mw), lambda i, j: (i, 0))
    g_specs = [pl.BlockSpec((tm, tn), functools.partial(lambda i, j, b: (i, gate_blk + b * (d // tn) + j), b=b))
               for b in range(N_BRANCH)]
    return pl.pallas_call(
        _merge_kernel, grid=(n // tm, d // tn),
        in_specs=[y_spec, y_spec, y_spec, *g_specs, pl.BlockSpec((N_BRANCH, mw, tn), lambda i, j: (0, 0, j))],
        out_specs=pl.BlockSpec((tm, tn), lambda i, j: (i, j)),
        out_shape=jax.ShapeDtypeStruct((n, d), BF16),
        compiler_params=_cparams("parallel", "parallel"), name="merge",
    )(ya, yb, yn, p, p, p, w_branch)


MOE_TM = 512


def _moe_dispatch(route, n_exp):
    n = route.shape[0]
    n_slots = n * TOP_K
    n_tiles = (n_slots + n_exp * (MOE_TM - 1)) // MOE_TM + 1
    i32 = jnp.int32
    experts = jnp.arange(n_exp, dtype=i32)
    flat = route[:, :TOP_K].astype(i32).reshape(n_slots)
    order = jnp.argsort(flat, stable=True).astype(i32)
    rank = jnp.argsort(order).astype(i32)
    onehot = (flat[:, None] == experts[None, :]).astype(i32)
    counts = jnp.sum(onehot, axis=0)
    padded = ((counts + MOE_TM - 1) // MOE_TM) * MOE_TM
    pad_end = jnp.cumsum(padded)
    pad_start = pad_end - padded
    start = jnp.cumsum(counts) - counts
    pos_of_slot = jnp.sum(onehot * (pad_start - start)[None, :], axis=1) + rank
    n_used = (pad_end[-1] // MOE_TM).astype(i32)
    tile_start = jnp.minimum(jnp.arange(n_tiles, dtype=i32), n_used - 1) * MOE_TM
    tile_expert = jnp.minimum(jnp.sum((tile_start[:, None] >= pad_end[None, :]).astype(i32), axis=1), n_exp - 1)
    tile_hot = (tile_expert[:, None] == experts[None, :]).astype(i32)
    row = jnp.arange(n_tiles, dtype=i32)[:, None] * MOE_TM + jnp.arange(MOE_TM, dtype=i32)[None, :]
    off = row - jnp.sum(tile_hot * pad_start[None, :], axis=1)[:, None]
    valid = (off < jnp.sum(tile_hot * counts[None, :], axis=1)[:, None]) & (row < n_used * MOE_TM)
    src = jnp.clip(jnp.sum(tile_hot * start[None, :], axis=1)[:, None] + off, 0, n_slots - 1)
    tok_of_pos = jnp.where(valid, jnp.take(order // TOP_K, src, mode="clip"), row % n).reshape(-1)
    return tok_of_pos, pos_of_slot.reshape(n, TOP_K), tile_expert, n_used.reshape(1)


def _moe_kernel(te_ref, nu_ref, x_ref, w1_ref, b1_ref, w2_ref, b2_ref, o_ref, w1_scr, w2_scr):
    t = pl.program_id(0)
    live = t < nu_ref[0]
    new_expert = (t == 0) | (te_ref[t] != te_ref[jnp.maximum(t - 1, 0)])

    @pl.when(live & new_expert)
    def _():
        w1_scr[...] = w1_ref[...].astype(BF16)
        w2_scr[...] = w2_ref[...].astype(BF16)

    @pl.when(live)
    def _():
        hu = jnp.dot(x_ref[...], w1_scr[...], preferred_element_type=F32) + b1_ref[0]
        f = hu.shape[1] // 2
        g = jnp.minimum(hu[:, :f], SWIGLU_LIMIT)
        u = jnp.clip(hu[:, f:], -SWIGLU_LIMIT, SWIGLU_LIMIT)
        act = g * jax.nn.sigmoid(SWIGLU_ALPHA * g) * (u + 1.0)
        y = jnp.dot(act.astype(BF16), w2_scr[...], preferred_element_type=F32) + b2_ref[0]
        o_ref[...] = y.astype(o_ref.dtype)

    @pl.when(jnp.logical_not(live))
    def _():
        o_ref[...] = jnp.zeros_like(o_ref)


def _moe(xs, tile_expert, n_used, w1, b1, w2, b2, layer):
    p, d = xs.shape
    _, n_exp, _, f2 = w1.shape
    grid_spec = pltpu.PrefetchScalarGridSpec(
        num_scalar_prefetch=2, grid=(p // MOE_TM,),
        in_specs=[pl.BlockSpec((MOE_TM, d), lambda t, te, nu: (t, 0)),
                  pl.BlockSpec((None, None, d, f2), lambda t, te, nu: (layer, te[t], 0, 0)),
                  pl.BlockSpec((1, 1, f2), lambda t, te, nu: (te[t], 0, 0)),
                  pl.BlockSpec((None, None, f2 // 2, d), lambda t, te, nu: (layer, te[t], 0, 0)),
                  pl.BlockSpec((1, 1, d), lambda t, te, nu: (te[t], 0, 0))],
        out_specs=pl.BlockSpec((MOE_TM, d), lambda t, te, nu: (t, 0)),
        scratch_shapes=[pltpu.VMEM((d, f2), BF16), pltpu.VMEM((f2 // 2, d), BF16)])
    return pl.pallas_call(
        _moe_kernel, grid_spec=grid_spec,
        out_shape=jax.ShapeDtypeStruct((p, d), BF16),
        compiler_params=_cparams("arbitrary"), name="moe",
    )(tile_expert, n_used, xs, w1, b1.reshape(n_exp, 1, f2), w2, b2.reshape(n_exp, 1, d))


def _dot_nt(a, b, precision=None):
    return lax.dot_general(a, b, (((1,), (1,)), ((), ())), preferred_element_type=F32, precision=precision)


def _dot_tn(a, b, precision=None):
    return lax.dot_general(a, b, (((0,), (0,)), ((), ())), preferred_element_type=F32, precision=precision)


def _scan_chunk(d, i, n_chunks, n_ctx_chunks):
    rev = jnp.where(i < n_ctx_chunks, n_ctx_chunks - 1 - i, n_chunks - 1 + n_ctx_chunks - i)
    return jnp.where(d == 0, i, rev)


def _head_sum_mats():
    ch = jnp.arange(MIX_W) // RWKV_HD
    e = (ch[:, None] == jnp.arange(LANES)[None, :]).astype(BF16)
    return e, e.T


def _dot_split(x, m_bf16):
    hi = x.astype(BF16)
    lo = (x - hi.astype(F32)).astype(BF16)
    return (jnp.dot(hi, m_bf16, preferred_element_type=F32) + jnp.dot(lo, m_bf16, preferred_element_type=F32))


def _rwkv_prep_kernel(*refs, tiles_per_batch):
    pieces, rest = [refs[3 * j:3 * j + 3] for j in range(4)], refs[12:]
    (shift_ref, w0_ref, wb_ref, a0_ref, ab_ref, gb_ref, kk_ref, ka_ref, rk_ref, e_ref, et_ref,
     r_out, v_out, kk_out, g_out, bonus_out, lw_out, b_out, kd_out) = rest
    i = pl.program_id(1)
    rows = lax.broadcasted_iota(jnp.int32, (ROW_TILE, 1), 0)
    has_prev = jnp.where(i > 1, 1.0, 0.0)
    has_next = jnp.where((i > 0) & (i < tiles_per_batch - 1), 1.0, 0.0)

    def token_shift(piece, col0):
        p_ref, pp_ref, pn_ref = piece
        x = p_ref[0]
        w = x.shape[1]
        xp = jnp.where(rows == 0, pp_ref[0, 7:8, :] * has_prev, pltpu.roll(x, 1, 0))
        xn = jnp.where(rows == ROW_TILE - 1, pn_ref[0, 0:1, :] * has_next, pltpu.roll(x, ROW_TILE - 1, 0))
        sh = shift_ref[:, col0:col0 + w]
        return xp * sh[0:1, :] + x * sh[1:2, :] + xn * sh[2:3, :]

    m = MIX_W
    r, k, v = (token_shift(pieces[j], j * m) for j in range(3))
    lora = token_shift(pieces[3], 3 * m)
    wl = lora[:, :RWKV_DECAY_RANK]
    al = lora[:, RWKV_DECAY_RANK:RWKV_DECAY_RANK + RWKV_AAA_RANK]
    gl = lora[:, RWKV_DECAY_RANK + RWKV_AAA_RANK:]
    dot = lambda x, w: jnp.dot(x.astype(BF16), w.astype(BF16), preferred_element_type=F32)
    e, et = e_ref[...], et_ref[...]
    g_out[0] = dot(jax.nn.sigmoid(gl), gb_ref[...])
    kk0 = k * kk_ref[...]
    nrm = jnp.maximum(jnp.sqrt(_dot_split(kk0 * kk0, e)), 1e-12)
    kk = kk0 * _dot_split(1.0 / nrm, et)
    tw = jnp.tanh(wl)
    bsum = None
    for d in range(2):
        a = jax.nn.sigmoid(a0_ref[d:d + 1, :] + dot(al, ab_ref[d]))
        kd = k * (1.0 + (a - 1.0) * ka_ref[...])
        lw = (-math.exp(-0.5)) * jax.nn.sigmoid(w0_ref[d:d + 1, :] + dot(tw, wb_ref[d]))
        lw_out[d, 0] = lw
        b_out[d, 0] = (kk * a).astype(b_out.dtype)
        kd_out[d, 0] = kd.astype(kd_out.dtype)
        t = _dot_split(r * kd * rk_ref[...], e)
        bsum = t if bsum is None else bsum + t
    r_out[0] = r.astype(r_out.dtype)
    v_out[0] = v.astype(v_out.dtype)
    kk_out[0] = kk.astype(kk_out.dtype)
    bonus_out[0] = _dot_split(bsum, et) * v


def _rwkv_prep(p3, shift, w0, wb, a0, ab, gb, k_k, k_a, r_k, ctx_len):
    b, tt, _ = p3.shape
    tpb = tt // ROW_TILE
    assert ctx_len == ROW_TILE
    e, et = _head_sum_mats()
    hb = ROW_TILE // 8
    full = lambda shape: pl.BlockSpec(shape, lambda bb, i: (0,) * len(shape))
    row1 = pl.BlockSpec((1, ROW_TILE, MIX_W), lambda bb, i: (bb, i, 0))
    row2 = pl.BlockSpec((2, 1, ROW_TILE, MIX_W), lambda bb, i: (0, bb, i, 0))
    s1 = jax.ShapeDtypeStruct((b, tt, MIX_W), F32)
    s2 = jax.ShapeDtypeStruct((2, b, tt, MIX_W), F32)
    h1 = jax.ShapeDtypeStruct((b, tt, MIX_W), BF16)
    h2 = jax.ShapeDtypeStruct((2, b, tt, MIX_W), BF16)
    vec = lambda a: a.reshape(1, MIX_W)

    def piece(width, col_block):
        return [pl.BlockSpec((1, ROW_TILE, width), lambda bb, i: (bb, i, col_block)),
                pl.BlockSpec((1, 8, width), lambda bb, i: (bb, jnp.maximum(i * hb - 1, 0), col_block)),
                pl.BlockSpec((1, 8, width), lambda bb, i: (bb, jnp.minimum((i + 1) * hb, tpb * hb - 1), col_block))]

    pieces = [s for j in range(3) for s in piece(MIX_W, OFF_RWKV // MIX_W + j)] + piece(RWKV_LORA_W, OFF_RWKV_LORA // RWKV_LORA_W)
    return pl.pallas_call(
        functools.partial(_rwkv_prep_kernel, tiles_per_batch=tpb), grid=(b, tpb),
        in_specs=pieces + [full((3, RWKV_SHIFT_W)), full((2, MIX_W)), full((2, RWKV_DECAY_RANK, MIX_W)), full((2, MIX_W)),
                           full((2, RWKV_AAA_RANK, MIX_W)), full((RWKV_GATE_RANK, MIX_W)),
                           full((1, MIX_W)), full((1, MIX_W)), full((1, MIX_W)), full((MIX_W, LANES)), full((LANES, MIX_W))],
        out_specs=[row1, row1, row1, row1, row1, row2, row2, row2],
        out_shape=[h1, h1, h1, s1, s1, s2, h2, h2],
        compiler_params=_cparams("parallel", "parallel"), name="rwkv_prep",
    )(*([p3] * 12), shift, w0, wb, a0, ab, gb, vec(k_k), vec(k_a), vec(r_k), e, et)


def _rwkv_scan_kernel(r0_ref, r1_ref, v0_ref, v1_ref, kk0_ref, kk1_ref, lw0_ref, lw1_ref, b0_ref, b1_ref,
                      kd0_ref, kd1_ref, y0_ref, y1_ref, s_scr):
    i = pl.program_id(0)
    c = RWKV_CHUNK
    n_b = r0_ref.shape[0]

    @pl.when(i == 0)
    def _():
        s_scr[...] = jnp.zeros_like(s_scr)

    bf = lambda x: x.astype(BF16)
    mm = lambda x, y: jnp.dot(bf(x), bf(y), preferred_element_type=F32)

    def cumsum_dot(t_bf16, x):
        hi = bf(x)
        r1 = x - hi.astype(F32)
        mid = bf(r1)
        lo = bf(r1 - mid.astype(F32))
        return (jnp.dot(t_bf16, hi, preferred_element_type=F32) + jnp.dot(t_bf16, mid, preferred_element_type=F32)
                + jnp.dot(t_bf16, lo, preferred_element_type=F32))

    ti = lax.broadcasted_iota(jnp.int32, (c, c), 0)
    si = lax.broadcasted_iota(jnp.int32, (c, c), 1)
    ri = lax.broadcasted_iota(jnp.int32, (LANES, LANES), 0)
    ci = lax.broadcasted_iota(jnp.int32, (LANES, LANES), 1)
    same = (ri // c) == (ci // c)
    lane = lax.broadcasted_iota(jnp.int32, (1, LANES), 1)
    m0 = (lane < RWKV_HD).astype(F32)
    m1 = 1.0 - m0
    stack = lambda x: jnp.concatenate([x * m0, x * m1], axis=0)
    unstack = lambda x: x[:c] + x[c:]
    pairs = range(RWKV_HEADS // 2)
    sls = [slice(p * LANES, (p + 1) * LANES) for p in pairs]

    tri, strict, incl = [], [], []
    for sgn in (1, -1):
        tri.append(jnp.where((ti - si) * sgn >= 0, 1.0, 0.0).astype(BF16))
        strict.append(same & ((ri - ci) * sgn > 0))
        incl.append(same & ((ri - ci) * sgn >= 0))

    refs = ((r0_ref, v0_ref, kk0_ref, lw0_ref, b0_ref, kd0_ref, y0_ref),
            (r1_ref, v1_ref, kk1_ref, lw1_ref, b1_ref, kd1_ref, y1_ref))
    streams = [(d, bi) for d in range(2) for bi in range(n_b)]
    a_t, b_t, k_t, r_t, v_all, g_tot = [], [], [], [], [], []
    for d, bi in streams:
        r_ref, v_ref, kk_ref, lw_ref, b_ref, kd_ref, _ = refs[d]
        lw = lw_ref[0, bi]
        cum = cumsum_dot(tri[d], lw)
        e_neg = jnp.exp(-cum)
        a_t.append(-kk_ref[bi].astype(F32) * jnp.exp(cum - lw))
        b_t.append(b_ref[0, bi].astype(F32) * e_neg)
        k_t.append(kd_ref[0, bi].astype(F32) * e_neg)
        r_t.append(r_ref[bi].astype(F32) * jnp.exp(cum))
        v_all.append(v_ref[bi])
        g_tot.append(jnp.exp(jnp.sum(lw, axis=0, keepdims=True)))

    keys = [(s, p) for s in range(len(streams)) for p in pairs]
    dirs = [streams[s][0] for s, _ in keys]
    a_st = [bf(stack(a_t[s][:, sls[p]])) for s, p in keys]
    r_st = [bf(stack(r_t[s][:, sls[p]])) for s, p in keys]
    v_st = [bf(stack(v_all[s][:, sls[p]])) for s, p in keys]
    b_p = [bf(b_t[s][:, sls[p]]) for s, p in keys]
    k_p = [bf(k_t[s][:, sls[p]]) for s, p in keys]
    ks = range(len(keys))
    gm = [_dot_nt(jnp.concatenate([a_st[j], r_st[j]], axis=0),
                  jnp.concatenate([b_p[j], b_p[j], k_p[j], k_p[j]], axis=0)) for j in ks]
    m_ak = [bf(jnp.where(strict[dirs[j]], gm[j][:LANES, LANES:], 0.0)) for j in ks]
    x = [jnp.concatenate([a_st[j].astype(F32), jnp.dot(m_ak[j], v_st[j], preferred_element_type=F32)], axis=1)
         for j in ks]
    pw = [jnp.where(strict[dirs[j]], gm[j][:LANES, :LANES], 0.0) for j in ks]
    for step in range(6):
        if step:
            pw = [mm(pw[j], pw[j]) for j in ks]
        x = [x[j] + mm(pw[j], x[j]) for j in ks]
    s_bd = [s_scr[streams[s][0], streams[s][1], p] for s, p in keys]
    s_bf = [bf(t) for t in s_bd]
    u_st = [_dot_nt(bf(x[j][:, :LANES]), s_bf[j]) + x[j][:, LANES:] for j in ks]
    m_rb = [bf(jnp.where(incl[dirs[j]], gm[j][LANES:, :LANES], 0.0)) for j in ks]
    m_rk = [bf(jnp.where(incl[dirs[j]], gm[j][LANES:, LANES:], 0.0)) for j in ks]
    for j, (s, p) in enumerate(keys):
        y_st = (_dot_nt(r_st[j], s_bf[j]) + jnp.dot(m_rb[j], bf(u_st[j]), preferred_element_type=F32)
                + jnp.dot(m_rk[j], v_st[j], preferred_element_type=F32))
        refs[streams[s][0]][6][streams[s][1], :, sls[p]] = unstack(y_st)
    for j, (s, p) in enumerate(keys):
        uv = jnp.concatenate([bf(unstack(u_st[j])), bf(v_all[s][:, sls[p]])], axis=0)
        bk = jnp.concatenate([b_p[j], k_p[j]], axis=0)
        upd = _dot_tn(uv, bk)
        s_scr[streams[s][0], streams[s][1], p] = g_tot[s][:, sls[p]] * (s_bd[j] + jnp.where(same, upd, 0.0))


def _rwkv_scan(r, v, kk, lw, bb, kd, ctx_len):
    b, tt, _ = r.shape
    c = RWKV_CHUNK
    n = tt // c
    nc = ctx_len // c
    shared = [pl.BlockSpec((b, c, MIX_W), functools.partial(lambda i, d: (0, _scan_chunk(d, i, n, nc), 0), d=d))
              for d in range(2)]
    per_dir = [pl.BlockSpec((1, b, c, MIX_W), functools.partial(lambda i, d: (d, 0, _scan_chunk(d, i, n, nc), 0), d=d))
               for d in range(2)]
    out = jax.ShapeDtypeStruct((b, tt, MIX_W), F32)
    return pl.pallas_call(
        _rwkv_scan_kernel, grid=(n,),
        in_specs=[*shared, *shared, *shared, *per_dir, *per_dir, *per_dir], out_specs=shared, out_shape=[out, out],
        scratch_shapes=[pltpu.VMEM((2, b, RWKV_HEADS // 2, LANES, LANES), F32)],
        compiler_params=_cparams("arbitrary"), name="rwkv_scan",
    )(r, r, v, v, kk, kk, lw, lw, bb, bb, kd, kd)


def _rwkv_finish_kernel(y0_ref, y1_ref, bonus_ref, g_ref, w_ref, b_ref, e_ref, et_ref, o_ref):
    e, et = e_ref[...], et_ref[...]
    y = y0_ref[0] + y1_ref[0]
    mu = _dot_split(_dot_split(y, e), et) * (1.0 / RWKV_HD)
    yc = y - mu
    var = _dot_split(_dot_split(yc * yc, e), et) * (1.0 / RWKV_HD)
    yn = yc * lax.rsqrt(var + RWKV_GN_EPS) * w_ref[...] + b_ref[...]
    o_ref[0] = ((yn + bonus_ref[0]) * g_ref[0]).astype(o_ref.dtype)


def _rwkv_finish(y0, y1, bonus, g, gn_w, gn_b):
    b, tt, _ = y0.shape
    e, et = _head_sum_mats()
    row = pl.BlockSpec((1, ROW_TILE, MIX_W), lambda bb, i: (bb, i, 0))
    full = lambda shape: pl.BlockSpec(shape, lambda bb, i: (0,) * len(shape))
    return pl.pallas_call(
        _rwkv_finish_kernel, grid=(b, tt // ROW_TILE),
        in_specs=[row, row, row, row,
                  full((1, MIX_W)), full((1, MIX_W)), full((MIX_W, LANES)), full((LANES, MIX_W))],
        out_specs=row, out_shape=jax.ShapeDtypeStruct((b, tt, MIX_W), BF16),
        compiler_params=_cparams("parallel", "parallel"), name="rwkv_finish",
    )(y0, y1, bonus, g, gn_w.reshape(1, MIX_W), gn_b.reshape(1, MIX_W), e, et)


def _rope_swap(x):
    lane = lax.broadcasted_iota(jnp.int32, x.shape, 1)
    return jnp.where((lane % 64) < 32, pltpu.roll(x, 96, 1), pltpu.roll(x, 32, 1))


def _rope_tables(t, ctx_len):
    pos = jnp.arange(t, dtype=jnp.int32)
    nf = RET_QK_HD // 4
    inv = ROPE_BASE ** (-jnp.arange(nf, dtype=F32) / nf)
    ang_r = (pos // GRID_W).astype(F32)[:, None] * inv[None, :]
    ang_c = (pos % GRID_W).astype(F32)[:, None] * inv[None, :]
    cos = jnp.concatenate([jnp.cos(ang_r)] * 2 + [jnp.cos(ang_c)] * 2, axis=-1)
    sin = jnp.concatenate([-jnp.sin(ang_r), jnp.sin(ang_r), -jnp.sin(ang_c), jnp.sin(ang_c)], axis=-1)
    cos = jnp.concatenate([jnp.ones((ctx_len, RET_QK_HD), F32), cos], axis=0)
    sin = jnp.concatenate([jnp.zeros((ctx_len, RET_QK_HD), F32), sin], axis=0)
    return cos, sin


def _ret_kernel(dec_ref, q0_ref, q1_ref, k0_ref, k1_ref, v0_ref, v1_ref, cos0_ref, cos1_ref, sin0_ref, sin1_ref,
                o0_ref, o1_ref, r_scr):
    i = pl.program_id(0)
    c = RET_CHUNK
    n_b = q0_ref.shape[0]

    @pl.when(i == 0)
    def _():
        r_scr[...] = jnp.zeros_like(r_scr)

    bf = lambda x: x.astype(BF16)
    ii = lax.broadcasted_iota(jnp.int32, (c, c), 0).astype(F32)
    jj = lax.broadcasted_iota(jnp.int32, (c, c), 1).astype(F32)
    heads = range(RET_HEADS)
    qs = [slice(h * RET_QK_HD, (h + 1) * RET_QK_HD) for h in heads]
    vs = [slice(h * RET_V_HD, (h + 1) * RET_V_HD) for h in heads]
    refs = ((q0_ref, k0_ref, v0_ref, cos0_ref, sin0_ref, o0_ref), (q1_ref, k1_ref, v1_ref, cos1_ref, sin1_ref, o1_ref))
    dmat, q_dec, k_dec, c_dec = {}, {}, {}, {}
    for d in range(2):
        pos = ii if d == 0 else c - 1.0 - ii
        diff = (ii - jj) if d == 0 else (jj - ii)
        for h in heads:
            lg = jnp.log(jax.nn.sigmoid(jnp.full((c, c), dec_ref[d, h], F32)))
            dmat[d, h] = jnp.where(diff >= 0, jnp.exp(lg * jnp.maximum(diff, 0.0)), 0.0)
            q_dec[d, h] = jnp.exp(lg * (pos + 1.0))
            k_dec[d, h] = jnp.exp(lg * (c - 1.0 - pos))
            c_dec[d, h] = jnp.exp(lg[:1, :1] * float(c))
    keys = [(d, bi, h) for d in range(2) for bi in range(n_b) for h in heads]
    q, k, v = {}, {}, {}
    for d, bi, h in keys:
        q_ref, k_ref, v_ref, cos_ref, sin_ref, _ = refs[d]
        cos, sin = cos_ref[...], sin_ref[...]
        qq = q_ref[bi, :, qs[h]]
        kk = k_ref[bi, :, qs[h]]
        q[d, bi, h] = qq * cos + _rope_swap(qq) * sin
        k[d, bi, h] = (kk * cos + _rope_swap(kk) * sin) * (RET_QK_HD ** -0.5)
        v[d, bi, h] = bf(v_ref[bi, :, vs[h]])
    r = {key: r_scr[key] for key in keys}
    s = {(d, bi, h): _dot_nt(bf(q[d, bi, h]), bf(k[d, bi, h])) * dmat[d, h] for d, bi, h in keys}
    for d, bi, h in keys:
        o = jnp.dot(bf(s[d, bi, h]), v[d, bi, h], preferred_element_type=F32)
        o = o + jnp.dot(bf(q[d, bi, h] * q_dec[d, h]), bf(r[d, bi, h]), preferred_element_type=F32)
        refs[d][5][bi, :, vs[h]] = o
    for d, bi, h in keys:
        r_scr[d, bi, h] = r[d, bi, h] * c_dec[d, h] + _dot_tn(bf(k[d, bi, h] * k_dec[d, h]), v[d, bi, h])


def _retention(p3, cos, sin, decay, ctx_len):
    b, tt, _ = p3.shape
    c = RET_CHUNK
    n = tt // c
    nc = ctx_len // c
    qb, kb, vb = OFF_RET_Q // RET_QK_W, OFF_RET_K // RET_QK_W, OFF_RET_V // MIX_W

    def per_dir(make):
        return [make(functools.partial(lambda i, dec, d: _scan_chunk(d, i, n, nc), d=d)) for d in range(2)]

    col = lambda width, cb: per_dir(lambda tok: pl.BlockSpec((b, c, width), lambda i, dec: (0, tok(i, dec), cb)))
    tab = per_dir(lambda tok: pl.BlockSpec((c, RET_QK_HD), lambda i, dec: (tok(i, dec), 0)))
    out = jax.ShapeDtypeStruct((b, tt, MIX_W), F32)
    grid_spec = pltpu.PrefetchScalarGridSpec(
        num_scalar_prefetch=1, grid=(n,),
        in_specs=[*col(RET_QK_W, qb), *col(RET_QK_W, kb), *col(MIX_W, vb), *tab, *tab],
        out_specs=col(MIX_W, 0),
        scratch_shapes=[pltpu.VMEM((2, b, RET_HEADS, RET_QK_HD, RET_V_HD), F32)])
    return pl.pallas_call(
        _ret_kernel, grid_spec=grid_spec, out_shape=[out, out],
        compiler_params=_cparams("arbitrary"), name="retention",
    )(decay, p3, p3, p3, p3, p3, p3, cos, cos, sin, sin)


def _ret_finish_kernel(o0_ref, o1_ref, g_ref, w_ref, b_ref, y_ref):
    for h in range(RET_HEADS):
        sl = slice(h * RET_V_HD, (h + 1) * RET_V_HD)
        o = o0_ref[0, :, sl] + o1_ref[0, :, sl]
        y = _norm(o, RET_GN_EPS) * w_ref[:, sl] + b_ref[:, sl]
        g = g_ref[0, :, sl]
        y_ref[0, :, sl] = (y * (g * jax.nn.sigmoid(g))).astype(y_ref.dtype)


def _ret_finish(o0, o1, p3, gn_w, gn_b):
    b, tt, _ = o0.shape
    gb = OFF_RET_G // MIX_W
    row = pl.BlockSpec((1, ROW_TILE, MIX_W), lambda bb, i: (bb, i, 0))
    vec = pl.BlockSpec((1, MIX_W), lambda bb, i: (0, 0))
    return pl.pallas_call(
        _ret_finish_kernel, grid=(b, tt // ROW_TILE),
        in_specs=[row, row, pl.BlockSpec((1, ROW_TILE, MIX_W), lambda bb, i: (bb, i, gb)), vec, vec],
        out_specs=row, out_shape=jax.ShapeDtypeStruct((b, tt, MIX_W), BF16),
        compiler_params=_cparams("parallel", "parallel"), name="ret_finish",
    )(o0, o1, p3, gn_w.reshape(1, MIX_W), gn_b.reshape(1, MIX_W))


NA_BLOCK_ROWS = 4
NA_BLOCK_TOK = NA_BLOCK_ROWS * GRID_W
NA_WIN_TOK = NA_WIN_ROWS * GRID_W


def _softmax_pv(s_parts, v_parts):
    m = None
    for s in s_parts:
        mm = jnp.max(s, axis=1, keepdims=True)
        m = mm if m is None else jnp.maximum(m, mm)
    den, acc = None, None
    for s, v in zip(s_parts, v_parts):
        e = jnp.exp(s - m)
        dd = jnp.sum(e, axis=1, keepdims=True)
        pv = jnp.dot(e.astype(BF16), v, preferred_element_type=F32)
        den = dd if den is None else den + dd
        acc = pv if acc is None else acc + pv
    return acc / den


def _na_kernel(q_ref, kp_ref, kc_ref, kn_ref, vp_ref, vc_ref, vn_ref, kx_ref, vx_ref, bias_ref, o_ref,
               k_scr, v_scr, *, n_blocks):
    rb = pl.program_id(1)
    n_b = q_ref.shape[0]
    batch = range(n_b)
    lane = lax.broadcasted_iota(jnp.int32, (1, LANES), 1)
    head_masks = [(lane < NA_HD).astype(F32), (lane >= NA_HD).astype(F32)]
    heads = range(2)
    kx = [kx_ref[bi].astype(BF16) for bi in batch]
    vx = [vx_ref[bi].astype(BF16) for bi in batch]
    scale = NA_HD ** -0.5

    @pl.when(rb == 0)
    def _():
        for bi in batch:
            q = q_ref[bi] * scale
            out = jnp.zeros((NA_BLOCK_TOK, LANES), F32)
            for hm in head_masks:
                s = _dot_nt((q * hm).astype(BF16), kx[bi])
                out = out + _softmax_pv([s], [vx[bi]]) * hm
            o_ref[bi] = out.astype(o_ref.dtype)

    @pl.when(rb > 0)
    def _():
        t = NA_BLOCK_TOK
        for bi in batch:
            k_scr[bi, 0:t, :] = kp_ref[bi].astype(BF16)
            k_scr[bi, t:2 * t, :] = kc_ref[bi].astype(BF16)
            k_scr[bi, 2 * t:3 * t, :] = kn_ref[bi].astype(BF16)
            v_scr[bi, 0:t, :] = vp_ref[bi].astype(BF16)
            v_scr[bi, t:2 * t, :] = vc_ref[bi].astype(BF16)
            v_scr[bi, 2 * t:3 * t, :] = vn_ref[bi].astype(BF16)
        first = rb == 1
        last = rb == n_blocks
        rows = range(NA_BLOCK_ROWS)
        rs = [slice(j * GRID_W, (j + 1) * GRID_W) for j in rows]
        starts, oi = [], []
        for j in rows:
            off = jnp.where(last, 0, jnp.where(first, NA_BLOCK_ROWS, j))
            oi.append(jnp.where(last, NA_BLOCK_ROWS + j, jnp.where(first, j, NA_WIN_ROWS // 2)))
            starts.append(pl.multiple_of(off * GRID_W, GRID_W))
        kw = [[k_scr[bi, pl.ds(starts[j], NA_WIN_TOK), :] for j in rows] for bi in batch]
        vw = [[v_scr[bi, pl.ds(starts[j], NA_WIN_TOK), :] for j in rows] for bi in batch]
        bias = [[bias_ref[oi[j], h] for h in heads] for j in rows]
        qh = [[(q_ref[bi] * scale * hm).astype(BF16) for hm in head_masks] for bi in batch]
        bh = [(bi, h) for bi in batch for h in heads]
        bjh = [(bi, j, h) for bi in batch for j in rows for h in heads]
        s_ctx = {(bi, h): _dot_nt(qh[bi][h], kx[bi]) for bi, h in bh}
        s_loc = {(bi, j, h): _dot_nt(qh[bi][h][rs[j]], kw[bi][j]) + bias[j][h] for bi, j, h in bjh}
        m_ctx = {k: jnp.max(s_ctx[k], axis=1, keepdims=True) for k in bh}
        m = {(bi, j, h): jnp.maximum(jnp.max(s_loc[bi, j, h], axis=1, keepdims=True), m_ctx[bi, h][rs[j]])
             for bi, j, h in bjh}
        m_all = {(bi, h): jnp.concatenate([m[bi, j, h] for j in rows], axis=0) for bi, h in bh}
        e_ctx = {k: jnp.exp(s_ctx[k] - m_all[k]) for k in bh}
        e_loc = {k: jnp.exp(s_loc[k] - m[k]) for k in bjh}
        d_ctx = {k: jnp.sum(e_ctx[k], axis=1, keepdims=True) for k in bh}
        pv_ctx = {(bi, h): jnp.dot(e_ctx[bi, h].astype(BF16), vx[bi], preferred_element_type=F32) for bi, h in bh}
        for bi in batch:
            for j in rows:
                out = None
                for h in heads:
                    den = jnp.sum(e_loc[bi, j, h], axis=1, keepdims=True) + d_ctx[bi, h][rs[j]]
                    pv = jnp.dot(e_loc[bi, j, h].astype(BF16), vw[bi][j], preferred_element_type=F32) + pv_ctx[bi, h][rs[j]]
                    term = pv / den * head_masks[h]
                    out = term if out is None else out + term
                o_ref[bi, rs[j], :] = out.astype(o_ref.dtype)


def _na_bias_table(rpb):
    col = jnp.arange(GRID_W)
    cs = jnp.clip(col - NA_WIN_COLS // 2, 0, GRID_W - NA_WIN_COLS)
    col_ok = (col[None, :] >= cs[:, None]) & (col[None, :] < cs[:, None] + NA_WIN_COLS)
    col_idx = jnp.clip(col[None, :] - col[:, None] + (NA_WIN_COLS - 1), 0, 2 * NA_WIN_COLS - 2)
    win = jnp.arange(NA_WIN_ROWS)
    row_idx = win[None, :] - win[:, None] + (NA_WIN_ROWS - 1)
    bias = rpb[:, row_idx[:, :, None, None], col_idx[None, None, :, :]]
    bias = jnp.where(col_ok[None, None, None], bias, -1e30)
    return bias.transpose(1, 0, 3, 2, 4).reshape(NA_WIN_ROWS, NA_HEADS, GRID_W, NA_WIN_TOK).astype(F32)


def _na(p3, bias_tab, ctx_len):
    b, tt, _ = p3.shape
    assert ctx_len == NA_BLOCK_TOK
    t = tt - ctx_len
    n_rows = t // GRID_W
    assert n_rows % NA_BLOCK_ROWS == 0 and n_rows >= NA_WIN_ROWS
    nb = n_rows // NA_BLOCK_ROWS
    qb, kb, vb = OFF_NA_Q // LANES, OFF_NA_K // LANES, OFF_NA_V // LANES
    blk = (b, NA_BLOCK_TOK, LANES)

    def spec(colb, shift):
        return pl.BlockSpec(blk, lambda pr, rb: (0, jnp.clip(rb + shift, 1, nb), colb + pr))

    return pl.pallas_call(
        functools.partial(_na_kernel, n_blocks=nb), grid=(NA_HEADS // 2, nb + 1),
        in_specs=[pl.BlockSpec(blk, lambda pr, rb: (0, rb, qb + pr)),
                  spec(kb, -1), spec(kb, 0), spec(kb, 1), spec(vb, -1), spec(vb, 0), spec(vb, 1),
                  pl.BlockSpec(blk, lambda pr, rb: (0, 0, kb + pr)),
                  pl.BlockSpec(blk, lambda pr, rb: (0, 0, vb + pr)),
                  pl.BlockSpec((NA_WIN_ROWS, 2, GRID_W, NA_WIN_TOK), lambda pr, rb: (0, pr, 0, 0))],
        out_specs=pl.BlockSpec(blk, lambda pr, rb: (0, rb, pr)),
        out_shape=jax.ShapeDtypeStruct((b, tt, MIX_W), BF16),
        scratch_shapes=[pltpu.VMEM((b, 3 * NA_BLOCK_TOK, LANES), BF16), pltpu.VMEM((b, 3 * NA_BLOCK_TOK, LANES), BF16)],
        compiler_params=_cparams("parallel", "parallel"), name="na",
    )(p3, p3, p3, p3, p3, p3, p3, p3, p3, bias_tab)


def _build_mod(gate, shift, scale, b):
    rows = jnp.stack([gate, shift, scale], axis=1)
    lat = rows[:b]
    ctx = jnp.broadcast_to(rows[b][None], lat.shape)
    mod = jnp.stack([ctx, lat], axis=1)
    return jnp.pad(mod, ((0, 0), (0, 0), (0, 5), (0, 0)))


def kernel(x, c, ctx, c_ctx, ada_w, ada_b, w_in, rwkv_shift, rwkv_w0, rwkv_wB, rwkv_a0, rwkv_aB, rwkv_gB, rwkv_kk, rwkv_ka, rwkv_rk, rwkv_gn_w, rwkv_gn_b, ret_decay, ret_gn_w, ret_gn_b, na_rpb, w_branch, w_out, ln1_w, ln1_b, ln2_w, ln2_b, router_w, router_b, exp_w1, exp_b1, exp_w2, exp_b2):
    b, t, d = x.shape
    ctx_len = ctx.shape[1]
    tt = ctx_len + t
    n = b * tt
    tpb = tt // ROW_TILE
    depth = ada_w.shape[0]
    alpha = (2 * depth) ** 0.25
    assert d == D_MODEL and b + 1 <= 8 and ctx_len == ROW_TILE and t % ROW_TILE == 0

    xa = jnp.concatenate([ctx, x], axis=1).reshape(n, d)
    cond = jnp.concatenate([c, c_ctx[None], jnp.zeros((8 - b - 1, d), F32)], axis=0)
    ada = _adaln(cond.T, ada_w, ada_b, b + 1).reshape(depth, 8, 6, d)
    cos, sin = _rope_tables(t, ctx_len)
    n_exp = router_w.shape[-1]
    rw_pad = jnp.pad(router_w, ((0, 0), (0, 0), (0, LANES - n_exp)))
    rb_pad = jnp.pad(router_b, ((0, 0), (0, LANES - n_exp)), constant_values=-1e30)
    zero = jnp.zeros((8, d), F32)

    h = _modulate(xa, _build_mod(zero, ada[0, :, 0], ada[0, :, 1], b), tpb)
    for l in range(depth):
        p = _in_proj(h, w_in, l)
        p3 = p.reshape(b, tt, D_IN_PROJ)

        r, v, kk, g, bonus, lw, bb, kd = _rwkv_prep(p3, rwkv_shift[l], rwkv_w0[l], rwkv_wB[l], rwkv_a0[l], rwkv_aB[l],
                                                    rwkv_gB[l], rwkv_kk[l], rwkv_ka[l], rwkv_rk[l], ctx_len)
        y_fwd, y_bwd = _rwkv_scan(r, v, kk, lw, bb, kd, ctx_len)
        ya = _rwkv_finish(y_fwd, y_bwd, bonus, g, rwkv_gn_w[l], rwkv_gn_b[l])

        o_fwd, o_bwd = _retention(p3, cos, sin, ret_decay[l], ctx_len)
        yb = _ret_finish(o_fwd, o_bwd, p3, ret_gn_w[l], ret_gn_b[l])

        yn = _na(p3, _na_bias_table(na_rpb[l]), ctx_len)

        merged = _merge(ya.reshape(n, MIX_W), yb.reshape(n, MIX_W), yn.reshape(n, MIX_W), p, w_branch[l].astype(BF16))
        y = _matmul(merged, w_out[l].astype(BF16), 512)

        mod_f = _build_mod(ada[l, :, 2], ada[l, :, 3], ada[l, :, 4], b)
        x1, h2, route = _resid_ln(xa, y, mod_f, ln1_w[l], ln1_b[l], tpb, alpha, rw_pad[l], rb_pad[l][None])
        tok_of_pos, pos_of_slot, tile_expert, n_used = _moe_dispatch(route, n_exp)
        xs = jnp.take(h2, tok_of_pos, axis=0, mode="clip")
        ys = _moe(xs, tile_expert, n_used, exp_w1, exp_b1[l], exp_w2, exp_b2[l], l)
        f4 = jnp.take(ys, pos_of_slot.T.reshape(-1), axis=0, mode="clip").reshape(TOP_K, n, d)
        nxt = min(l + 1, depth - 1)
        mod_a = _build_mod(ada[l, :, 5], ada[nxt, :, 0], ada[nxt, :, 1], b)
        xa, h = _resid_ln(x1, f4, mod_a, ln2_w[l], ln2_b[l], tpb, alpha, route=route)
    return xa.reshape(b, tt, d)[:, ctx_len:]
```

```python
import functools
import math

import jax
import jax.numpy as jnp
from jax import lax
from jax.experimental import pallas as pl
from jax.experimental.pallas import tpu as pltpu

F32 = jnp.float32
BF16 = jnp.bfloat16
HI = lax.Precision.HIGHEST

D_MODEL = 2048
GRID_W = 64
MIX_W = D_MODEL // 2
N_BRANCH = 3

RWKV_HD = 64
RWKV_HEADS = MIX_W // RWKV_HD
RWKV_DECAY_RANK = 64
RWKV_AAA_RANK = 64
RWKV_GATE_RANK = 128
RWKV_SHIFT_W = 3 * MIX_W + RWKV_DECAY_RANK + RWKV_AAA_RANK + RWKV_GATE_RANK
RWKV_GN_EPS = 64e-5
RWKV_CHUNK = 64

RET_HEADS = 4
RET_V_HD = MIX_W // RET_HEADS
RET_QK_HD = RET_V_HD // 2
RET_QK_W = RET_HEADS * RET_QK_HD
RET_CHUNK = 128
RET_GN_EPS = 1e-6

NA_HD = 64
NA_HEADS = MIX_W // NA_HD
NA_WIN_ROWS = 8
NA_WIN_COLS = 16

N_EXPERTS = 32
TOP_K = 4
D_EXPERT = D_MODEL // 4
SWIGLU_LIMIT = 7.0
SWIGLU_ALPHA = 1.702

ROPE_BASE = 10000.0
LN_EPS = 1e-6

IN_PROJ_TN = 256
D_IN_PROJ = RWKV_SHIFT_W + 2 * RET_QK_W + 5 * MIX_W + N_BRANCH * D_MODEL
IN_PROJ_ROT = RWKV_SHIFT_W // IN_PROJ_TN
OFF_RET_Q = 0
OFF_RET_K = OFF_RET_Q + RET_QK_W
OFF_RET_V = OFF_RET_K + RET_QK_W
OFF_RET_G = OFF_RET_V + MIX_W
OFF_NA_Q = OFF_RET_G + MIX_W
OFF_NA_K = OFF_NA_Q + MIX_W
OFF_NA_V = OFF_NA_K + MIX_W
OFF_GATE = OFF_NA_V + MIX_W
OFF_RWKV = OFF_GATE + N_BRANCH * D_MODEL
OFF_RWKV_LORA = OFF_RWKV + 3 * MIX_W
RWKV_LORA_W = RWKV_SHIFT_W - 3 * MIX_W

ROW_TILE = 256
LANES = 128
VMEM_LIMIT = 56 * 1024 * 1024


def _cparams(*sem):
    return pltpu.CompilerParams(dimension_semantics=sem, vmem_limit_bytes=VMEM_LIMIT)


def _adaln_kernel(ct_ref, w_ref, b_ref, o_ref, *, n_cond):
    ct = ct_ref[...]
    st = ct * jax.nn.sigmoid(ct)
    w = w_ref[0]
    o_ref[0] = jnp.zeros(o_ref.shape[1:], F32)
    for m in range(n_cond):
        o_ref[0, m:m + 1, :] = jnp.sum(st[:, m:m + 1] * w, axis=0, keepdims=True) + b_ref[0]


def _adaln(cond_t, ada_w, ada_b, n_cond):
    n_layers, d, d6 = ada_w.shape
    tn = 512
    return pl.pallas_call(
        functools.partial(_adaln_kernel, n_cond=n_cond),
        grid=(n_layers, d6 // tn),
        in_specs=[pl.BlockSpec((d, 8), lambda l, j: (0, 0)),
                  pl.BlockSpec((1, d, tn), lambda l, j: (l, 0, j)),
                  pl.BlockSpec((1, 1, tn), lambda l, j: (l, 0, j))],
        out_specs=pl.BlockSpec((1, 8, tn), lambda l, j: (l, 0, j)),
        out_shape=jax.ShapeDtypeStruct((n_layers, 8, d6), F32),
        compiler_params=_cparams("parallel", "parallel"),
        name="adaln",
    )(cond_t, ada_w, ada_b.reshape(n_layers, 1, d6))


def _norm(x, eps):
    mu = jnp.mean(x, axis=-1, keepdims=True)
    xc = x - mu
    var = jnp.mean(xc * xc, axis=-1, keepdims=True)
    return xc * lax.rsqrt(var + eps)


def _top4_route(logits):
    lane = lax.broadcasted_iota(jnp.int32, logits.shape, 1).astype(F32)
    l = logits
    idxs, vals = [], []
    for k in range(TOP_K):
        m = jnp.max(l, axis=1, keepdims=True)
        idx = jnp.min(jnp.where(l == m, lane, float(LANES)), axis=1, keepdims=True)
        l = jnp.where(lane == idx, -jnp.inf, l)
        idxs.append(idx)
        vals.append(m)
    es = [jnp.exp(v - vals[0]) for v in vals]
    den = es[0] + es[1] + es[2] + es[3]
    out = jnp.zeros(logits.shape, F32)
    for k in range(TOP_K):
        out = jnp.where(lane == float(k), idxs[k], out)
        out = jnp.where(lane == float(TOP_K + k), es[k] / den, out)
    return out


def _mod_kernel(x_ref, mod_ref, h_ref):
    md = mod_ref[0, 0]
    h = _norm(x_ref[...], LN_EPS) * (1.0 + md[2:3, :]) + md[1:2, :]
    h_ref[...] = h.astype(BF16)


def _resid_kernel(*refs, alpha, router, combine):
    refs = list(refs)
    x_ref, y_ref, mod_ref, lnw_ref, lnb_ref = refs[:5]
    rest = refs[5:]
    if combine:
        route_ref, rest = rest[0], rest[1:]
    if router:
        rw_ref, rb_ref, rest = rest[0], rest[1], rest[2:]
    xo_ref, h_ref = rest[:2]
    md = mod_ref[0, 0]
    if combine:
        route = route_ref[...]
        y = None
        for k in range(TOP_K):
            t = route[:, TOP_K + k:TOP_K + k + 1] * y_ref[k].astype(F32)
            y = t if y is None else y + t
    else:
        y = y_ref[...]
    z = alpha * x_ref[...] + md[0:1, :] * y
    x1 = _norm(z, LN_EPS) * lnw_ref[...] + lnb_ref[...]
    xo_ref[...] = x1
    h = _norm(x1, LN_EPS) * (1.0 + md[2:3, :]) + md[1:2, :]
    h_ref[...] = h.astype(BF16)
    if router:
        logits = jnp.dot(h, rw_ref[...], preferred_element_type=F32, precision=HI) + rb_ref[...]
        rest[2][...] = _top4_route(logits)


def _row_specs(n_rows, tiles_per_batch, d):
    row = pl.BlockSpec((ROW_TILE, d), lambda i: (i, 0))
    mod = pl.BlockSpec((1, 1, 8, d), lambda i: (i // tiles_per_batch, jnp.minimum(i % tiles_per_batch, 1), 0, 0))
    vec = pl.BlockSpec((1, d), lambda i: (0, 0))
    return row, mod, vec


def _modulate(x, mod, tiles_per_batch):
    n, d = x.shape
    row, modspec, _ = _row_specs(n, tiles_per_batch, d)
    return pl.pallas_call(
        _mod_kernel, grid=(n // ROW_TILE,),
        in_specs=[row, modspec], out_specs=row,
        out_shape=jax.ShapeDtypeStruct((n, d), BF16),
        compiler_params=_cparams("parallel"), name="modulate",
    )(x, mod)


def _resid_ln(x, y, mod, lnw, lnb, tiles_per_batch, alpha, router_w=None, router_b=None, route=None):
    n, d = x.shape
    row, modspec, vec = _row_specs(n, tiles_per_batch, d)
    lanes = pl.BlockSpec((ROW_TILE, LANES), lambda i: (i, 0))
    router = router_w is not None
    combine = route is not None
    y_spec = pl.BlockSpec((TOP_K, ROW_TILE, d), lambda i: (0, i, 0)) if combine else row
    in_specs = [row, y_spec, modspec, vec, vec]
    args = [x, y, mod, lnw.reshape(1, d), lnb.reshape(1, d)]
    out_specs = [row, row]
    out_shape = [jax.ShapeDtypeStruct((n, d), F32), jax.ShapeDtypeStruct((n, d), BF16)]
    if combine:
        in_specs.append(lanes)
        args.append(route)
    if router:
        in_specs += [pl.BlockSpec((d, LANES), lambda i: (0, 0)), pl.BlockSpec((1, LANES), lambda i: (0, 0))]
        args += [router_w, router_b]
        out_specs.append(lanes)
        out_shape.append(jax.ShapeDtypeStruct((n, LANES), F32))
    return pl.pallas_call(
        functools.partial(_resid_kernel, alpha=alpha, router=router, combine=combine),
        grid=(n // ROW_TILE,), in_specs=in_specs, out_specs=out_specs, out_shape=out_shape,
        compiler_params=_cparams("parallel"), name="resid_ln",
    )(*args)


def _matmul_kernel(x_ref, w_ref, o_ref):
    o_ref[...] = jnp.dot(x_ref[...], w_ref[...], preferred_element_type=F32).astype(o_ref.dtype)


def _pick_tm(n, cap):
    tm = cap
    while n % tm:
        tm -= 16
    return tm


def _matmul(x, w, tn, out_dtype=F32, tm_cap=1536):
    n, k = x.shape
    _, m = w.shape
    tm = _pick_tm(n, tm_cap)
    return pl.pallas_call(
        _matmul_kernel, grid=(n // tm, m // tn),
        in_specs=[pl.BlockSpec((tm, k), lambda i, j: (i, 0)), pl.BlockSpec((k, tn), lambda i, j: (0, j))],
        out_specs=pl.BlockSpec((tm, tn), lambda i, j: (i, j)),
        out_shape=jax.ShapeDtypeStruct((n, m), out_dtype),
        compiler_params=_cparams("parallel", "parallel"), name="matmul",
    )(x, w)


def _in_proj_kernel(x_ref, w_ref, o_ref):
    o_ref[...] = jnp.dot(x_ref[...], w_ref[...].astype(BF16), preferred_element_type=F32)


def _in_proj(x, w, layer):
    n, k = x.shape
    tm = _pick_tm(n, 2816)
    nblk = D_IN_PROJ // IN_PROJ_TN
    return pl.pallas_call(
        _in_proj_kernel, grid=(n // tm, nblk),
        in_specs=[pl.BlockSpec((tm, k), lambda i, j: (i, 0)),
                  pl.BlockSpec((None, k, IN_PROJ_TN), lambda i, j: (layer, 0, (j + IN_PROJ_ROT) % nblk))],
        out_specs=pl.BlockSpec((tm, IN_PROJ_TN), lambda i, j: (i, j)),
        out_shape=jax.ShapeDtypeStruct((n, D_IN_PROJ), F32),
        compiler_params=_cparams("parallel", "parallel"), name="in_proj",
    )(x, w)


def _merge_kernel(ya_ref, yb_ref, yn_ref, g0_ref, g1_ref, g2_ref, wb_ref, o_ref):
    acc = None
    for i, (y_ref, g_ref) in enumerate(((ya_ref, g0_ref), (yb_ref, g1_ref), (yn_ref, g2_ref))):
        z = jnp.dot(y_ref[...], wb_ref[i], preferred_element_type=F32)
        t = jax.nn.sigmoid(g_ref[...]) * z
        acc = t if acc is None else acc + t
    o_ref[...] = acc.astype(o_ref.dtype)


def _merge(ya, yb, yn, p, w_branch):
    n, mw = ya.shape
    d = w_branch.shape[-1]
    tn = 512
    tm = _pick_tm(n, 768)
    gate_blk = OFF_GATE // tn
    y_spec = pl.BlockSpec((tm, mw), lambda i, j: (i, 0))
    g_specs = [pl.BlockSpec((tm, tn), functools.partial(lambda i, j, b: (i, gate_blk + b * (d // tn) + j), b=b))
               for b in range(N_BRANCH)]
    return pl.pallas_call(
        _merge_kernel, grid=(n // tm, d // tn),
        in_specs=[y_spec, y_spec, y_spec, *g_specs, pl.BlockSpec((N_BRANCH, mw, tn), lambda i, j: (0, 0, j))],
        out_specs=pl.BlockSpec((tm, tn), lambda i, j: (i, j)),
        out_shape=jax.ShapeDtypeStruct((n, d), BF16),
        compiler_params=_cparams("parallel", "parallel"), name="merge",
    )(ya, yb, yn, p, p, p, w_branch)


MOE_TM = 512


def _moe_dispatch(route, n_exp):
    n = route.shape[0]
    n_slots = n * TOP_K
    n_tiles = (n_slots + n_exp * (MOE_TM - 1)) // MOE_TM + 1
    i32 = jnp.int32
    experts = jnp.arange(n_exp, dtype=i32)
    flat = route[:, :TOP_K].astype(i32).reshape(n_slots)
    order = jnp.argsort(flat, stable=True).astype(i32)
    rank = jnp.argsort(order).astype(i32)
    onehot = (flat[:, None] == experts[None, :]).astype(i32)
    counts = jnp.sum(onehot, axis=0)
    padded = ((counts + MOE_TM - 1) // MOE_TM) * MOE_TM
    pad_end = jnp.cumsum(padded)
    pad_start = pad_end - padded
    start = jnp.cumsum(counts) - counts
    pos_of_slot = jnp.sum(onehot * (pad_start - start)[None, :], axis=1) + rank
    n_used = (pad_end[-1] // MOE_TM).astype(i32)
    tile_start = jnp.minimum(jnp.arange(n_tiles, dtype=i32), n_used - 1) * MOE_TM
    tile_expert = jnp.minimum(jnp.sum((tile_start[:, None] >= pad_end[None, :]).astype(i32), axis=1), n_exp - 1)
    tile_hot = (tile_expert[:, None] == experts[None, :]).astype(i32)
    row = jnp.arange(n_tiles, dtype=i32)[:, None] * MOE_TM + jnp.arange(MOE_TM, dtype=i32)[None, :]
    off = row - jnp.sum(tile_hot * pad_start[None, :], axis=1)[:, None]
    valid = (off < jnp.sum(tile_hot * counts[None, :], axis=1)[:, None]) & (row < n_used * MOE_TM)
    src = jnp.clip(jnp.sum(tile_hot * start[None, :], axis=1)[:, None] + off, 0, n_slots - 1)
    tok_of_pos = jnp.where(valid, jnp.take(order // TOP_K, src, mode="clip"), row % n).reshape(-1)
    return tok_of_pos, pos_of_slot.reshape(n, TOP_K), tile_expert, n_used.reshape(1)


def _moe_kernel(te_ref, nu_ref, x_ref, w1_ref, b1_ref, w2_ref, b2_ref, o_ref, w1_scr, w2_scr):
    t = pl.program_id(0)
    live = t < nu_ref[0]
    new_expert = (t == 0) | (te_ref[t] != te_ref[jnp.maximum(t - 1, 0)])

    @pl.when(live & new_expert)
    def _():
        w1_scr[...] = w1_ref[...].astype(BF16)
        w2_scr[...] = w2_ref[...].astype(BF16)

    @pl.when(live)
    def _():
        hu = jnp.dot(x_ref[...], w1_scr[...], preferred_element_type=F32) + b1_ref[0]
        f = hu.shape[1] // 2
        g = jnp.minimum(hu[:, :f], SWIGLU_LIMIT)
        u = jnp.clip(hu[:, f:], -SWIGLU_LIMIT, SWIGLU_LIMIT)
        act = g * jax.nn.sigmoid(SWIGLU_ALPHA * g) * (u + 1.0)
        y = jnp.dot(act.astype(BF16), w2_scr[...], preferred_element_type=F32) + b2_ref[0]
        o_ref[...] = y.astype(o_ref.dtype)

    @pl.when(jnp.logical_not(live))
    def _():
        o_ref[...] = jnp.zeros_like(o_ref)


def _moe(xs, tile_expert, n_used, w1, b1, w2, b2, layer):
    p, d = xs.shape
    _, n_exp, _, f2 = w1.shape
    grid_spec = pltpu.PrefetchScalarGridSpec(
        num_scalar_prefetch=2, grid=(p // MOE_TM,),
        in_specs=[pl.BlockSpec((MOE_TM, d), lambda t, te, nu: (t, 0)),
                  pl.BlockSpec((None, None, d, f2), lambda t, te, nu: (layer, te[t], 0, 0)),
                  pl.BlockSpec((1, 1, f2), lambda t, te, nu: (te[t], 0, 0)),
                  pl.BlockSpec((None, None, f2 // 2, d), lambda t, te, nu: (layer, te[t], 0, 0)),
                  pl.BlockSpec((1, 1, d), lambda t, te, nu: (te[t], 0, 0))],
        out_specs=pl.BlockSpec((MOE_TM, d), lambda t, te, nu: (t, 0)),
        scratch_shapes=[pltpu.VMEM((d, f2), BF16), pltpu.VMEM((f2 // 2, d), BF16)])
    return pl.pallas_call(
        _moe_kernel, grid_spec=grid_spec,
        out_shape=jax.ShapeDtypeStruct((p, d), BF16),
        compiler_params=_cparams("arbitrary"), name="moe",
    )(tile_expert, n_used, xs, w1, b1.reshape(n_exp, 1, f2), w2, b2.reshape(n_exp, 1, d))


def _dot_nt(a, b, precision=None):
    return lax.dot_general(a, b, (((1,), (1,)), ((), ())), preferred_element_type=F32, precision=precision)


def _dot_tn(a, b, precision=None):
    return lax.dot_general(a, b, (((0,), (0,)), ((), ())), preferred_element_type=F32, precision=precision)


def _scan_chunk(d, i, n_chunks, n_ctx_chunks):
    rev = jnp.where(i < n_ctx_chunks, n_ctx_chunks - 1 - i, n_chunks - 1 + n_ctx_chunks - i)
    return jnp.where(d == 0, i, rev)


def _head_sum_mats():
    ch = jnp.arange(MIX_W) // RWKV_HD
    e = (ch[:, None] == jnp.arange(LANES)[None, :]).astype(BF16)
    return e, e.T


def _dot_split(x, m_bf16):
    hi = x.astype(BF16)
    lo = (x - hi.astype(F32)).astype(BF16)
    return (jnp.dot(hi, m_bf16, preferred_element_type=F32) + jnp.dot(lo, m_bf16, preferred_element_type=F32))


def _rwkv_prep_kernel(*refs, tiles_per_batch):
    pieces, rest = [refs[3 * j:3 * j + 3] for j in range(4)], refs[12:]
    (shift_ref, w0_ref, wb_ref, a0_ref, ab_ref, gb_ref, kk_ref, ka_ref, rk_ref, e_ref, et_ref,
     r_out, v_out, kk_out, g_out, bonus_out, lw_out, b_out, kd_out) = rest
    i = pl.program_id(1)
    rows = lax.broadcasted_iota(jnp.int32, (ROW_TILE, 1), 0)
    has_prev = jnp.where(i > 1, 1.0, 0.0)
    has_next = jnp.where((i > 0) & (i < tiles_per_batch - 1), 1.0, 0.0)

    def token_shift(piece, col0):
        p_ref, pp_ref, pn_ref = piece
        x = p_ref[0]
        w = x.shape[1]
        xp = jnp.where(rows == 0, pp_ref[0, 7:8, :] * has_prev, pltpu.roll(x, 1, 0))
        xn = jnp.where(rows == ROW_TILE - 1, pn_ref[0, 0:1, :] * has_next, pltpu.roll(x, ROW_TILE - 1, 0))
        sh = shift_ref[:, col0:col0 + w]
        return xp * sh[0:1, :] + x * sh[1:2, :] + xn * sh[2:3, :]

    m = MIX_W
    r, k, v = (token_shift(pieces[j], j * m) for j in range(3))
    lora = token_shift(pieces[3], 3 * m)
    wl = lora[:, :RWKV_DECAY_RANK]
    al = lora[:, RWKV_DECAY_RANK:RWKV_DECAY_RANK + RWKV_AAA_RANK]
    gl = lora[:, RWKV_DECAY_RANK + RWKV_AAA_RANK:]
    dot = lambda x, w: jnp.dot(x.astype(BF16), w.astype(BF16), preferred_element_type=F32)
    e, et = e_ref[...], et_ref[...]
    g_out[0] = dot(jax.nn.sigmoid(gl), gb_ref[...])
    kk0 = k * kk_ref[...]
    nrm = jnp.maximum(jnp.sqrt(_dot_split(kk0 * kk0, e)), 1e-12)
    kk = kk0 * _dot_split(1.0 / nrm, et)
    tw = jnp.tanh(wl)
    bsum = None
    for d in range(2):
        a = jax.nn.sigmoid(a0_ref[d:d + 1, :] + dot(al, ab_ref[d]))
        kd = k * (1.0 + (a - 1.0) * ka_ref[...])
        lw = (-math.exp(-0.5)) * jax.nn.sigmoid(w0_ref[d:d + 1, :] + dot(tw, wb_ref[d]))
        lw_out[d, 0] = lw
        b_out[d, 0] = (kk * a).astype(b_out.dtype)
        kd_out[d, 0] = kd.astype(kd_out.dtype)
        t = _dot_split(r * kd * rk_ref[...], e)
        bsum = t if bsum is None else bsum + t
    r_out[0] = r.astype(r_out.dtype)
    v_out[0] = v.astype(v_out.dtype)
    kk_out[0] = kk.astype(kk_out.dtype)
    bonus_out[0] = _dot_split(bsum, et) * v


def _rwkv_prep(p3, shift, w0, wb, a0, ab, gb, k_k, k_a, r_k, ctx_len):
    b, tt, _ = p3.shape
    tpb = tt // ROW_TILE
    assert ctx_len == ROW_TILE
    e, et = _head_sum_mats()
    hb = ROW_TILE // 8
    full = lambda shape: pl.BlockSpec(shape, lambda bb, i: (0,) * len(shape))
    row1 = pl.BlockSpec((1, ROW_TILE, MIX_W), lambda bb, i: (bb, i, 0))
    row2 = pl.BlockSpec((2, 1, ROW_TILE, MIX_W), lambda bb, i: (0, bb, i, 0))
    s1 = jax.ShapeDtypeStruct((b, tt, MIX_W), F32)
    s2 = jax.ShapeDtypeStruct((2, b, tt, MIX_W), F32)
    h1 = jax.ShapeDtypeStruct((b, tt, MIX_W), BF16)
    h2 = jax.ShapeDtypeStruct((2, b, tt, MIX_W), BF16)
    vec = lambda a: a.reshape(1, MIX_W)

    def piece(width, col_block):
        return [pl.BlockSpec((1, ROW_TILE, width), lambda bb, i: (bb, i, col_block)),
                pl.BlockSpec((1, 8, width), lambda bb, i: (bb, jnp.maximum(i * hb - 1, 0), col_block)),
                pl.BlockSpec((1, 8, width), lambda bb, i: (bb, jnp.minimum((i + 1) * hb, tpb * hb - 1), col_block))]

    pieces = [s for j in range(3) for s in piece(MIX_W, OFF_RWKV // MIX_W + j)] + piece(RWKV_LORA_W, OFF_RWKV_LORA // RWKV_LORA_W)
    return pl.pallas_call(
        functools.partial(_rwkv_prep_kernel, tiles_per_batch=tpb), grid=(b, tpb),
        in_specs=pieces + [full((3, RWKV_SHIFT_W)), full((2, MIX_W)), full((2, RWKV_DECAY_RANK, MIX_W)), full((2, MIX_W)),
                           full((2, RWKV_AAA_RANK, MIX_W)), full((RWKV_GATE_RANK, MIX_W)),
                           full((1, MIX_W)), full((1, MIX_W)), full((1, MIX_W)), full((MIX_W, LANES)), full((LANES, MIX_W))],
        out_specs=[row1, row1, row1, row1, row1, row2, row2, row2],
        out_shape=[h1, h1, h1, s1, s1, s2, h2, h2],
        compiler_params=_cparams("parallel", "parallel"), name="rwkv_prep",
    )(*([p3] * 12), shift, w0, wb, a0, ab, gb, vec(k_k), vec(k_a), vec(r_k), e, et)


def _rwkv_scan_kernel(r0_ref, r1_ref, v0_ref, v1_ref, kk0_ref, kk1_ref, lw0_ref, lw1_ref, b0_ref, b1_ref,
                      kd0_ref, kd1_ref, y0_ref, y1_ref, s_scr):
    i = pl.program_id(0)
    c = RWKV_CHUNK
    n_b = r0_ref.shape[0]

    @pl.when(i == 0)
    def _():
        s_scr[...] = jnp.zeros_like(s_scr)

    bf = lambda x: x.astype(BF16)
    mm = lambda x, y: jnp.dot(bf(x), bf(y), preferred_element_type=F32)

    def cumsum_dot(t_bf16, x):
        hi = bf(x)
        r1 = x - hi.astype(F32)
        mid = bf(r1)
        lo = bf(r1 - mid.astype(F32))
        return (jnp.dot(t_bf16, hi, preferred_element_type=F32) + jnp.dot(t_bf16, mid, preferred_element_type=F32)
                + jnp.dot(t_bf16, lo, preferred_element_type=F32))

    ti = lax.broadcasted_iota(jnp.int32, (c, c), 0)
    si = lax.broadcasted_iota(jnp.int32, (c, c), 1)
    ri = lax.broadcasted_iota(jnp.int32, (LANES, LANES), 0)
    ci = lax.broadcasted_iota(jnp.int32, (LANES, LANES), 1)
    same = (ri // c) == (ci // c)
    lane = lax.broadcasted_iota(jnp.int32, (1, LANES), 1)
    m0 = (lane < RWKV_HD).astype(F32)
    m1 = 1.0 - m0
    stack = lambda x: jnp.concatenate([x * m0, x * m1], axis=0)
    unstack = lambda x: x[:c] + x[c:]
    pairs = range(RWKV_HEADS // 2)
    sls = [slice(p * LANES, (p + 1) * LANES) for p in pairs]

    tri, strict, incl = [], [], []
    for sgn in (1, -1):
        tri.append(jnp.where((ti - si) * sgn >= 0, 1.0, 0.0).astype(BF16))
        strict.append(same & ((ri - ci) * sgn > 0))
        incl.append(same & ((ri - ci) * sgn >= 0))

    refs = ((r0_ref, v0_ref, kk0_ref, lw0_ref, b0_ref, kd0_ref, y0_ref),
            (r1_ref, v1_ref, kk1_ref, lw1_ref, b1_ref, kd1_ref, y1_ref))
    streams = [(d, bi) for d in range(2) for bi in range(n_b)]
    a_t, b_t, k_t, r_t, v_all, g_tot = [], [], [], [], [], []
    for d, bi in streams:
        r_ref, v_ref, kk_ref, lw_ref, b_ref, kd_ref, _ = refs[d]
        lw = lw_ref[0, bi]
        cum = cumsum_dot(tri[d], lw)
        e_neg = jnp.exp(-cum)
        a_t.append(-kk_ref[bi].astype(F32) * jnp.exp(cum - lw))
        b_t.append(b_ref[0, bi].astype(F32) * e_neg)
        k_t.append(kd_ref[0, bi].astype(F32) * e_neg)
        r_t.append(r_ref[bi].astype(F32) * jnp.exp(cum))
        v_all.append(v_ref[bi])
        g_tot.append(jnp.exp(jnp.sum(lw, axis=0, keepdims=True)))

    keys = [(s, p) for s in range(len(streams)) for p in pairs]
    dirs = [streams[s][0] for s, _ in keys]
    a_st = [bf(stack(a_t[s][:, sls[p]])) for s, p in keys]
    r_st = [bf(stack(r_t[s][:, sls[p]])) for s, p in keys]
    v_st = [bf(stack(v_all[s][:, sls[p]])) for s, p in keys]
    b_p = [bf(b_t[s][:, sls[p]]) for s, p in keys]
    k_p = [bf(k_t[s][:, sls[p]]) for s, p in keys]
    ks = range(len(keys))
    gm = [_dot_nt(jnp.concatenate([a_st[j], r_st[j]], axis=0),
                  jnp.concatenate([b_p[j], b_p[j], k_p[j], k_p[j]], axis=0)) for j in ks]
    m_ak = [bf(jnp.where(strict[dirs[j]], gm[j][:LANES, LANES:], 0.0)) for j in ks]
    x = [jnp.concatenate([a_st[j].astype(F32), jnp.dot(m_ak[j], v_st[j], preferred_element_type=F32)], axis=1)
         for j in ks]
    pw = [jnp.where(strict[dirs[j]], gm[j][:LANES, :LANES], 0.0) for j in ks]
    for step in range(6):
        if step:
            pw = [mm(pw[j], pw[j]) for j in ks]
        x = [x[j] + mm(pw[j], x[j]) for j in ks]
    s_bd = [s_scr[streams[s][0], streams[s][1], p] for s, p in keys]
    s_bf = [bf(t) for t in s_bd]
    u_st = [_dot_nt(bf(x[j][:, :LANES]), s_bf[j]) + x[j][:, LANES:] for j in ks]
    m_rb = [bf(jnp.where(incl[dirs[j]], gm[j][LANES:, :LANES], 0.0)) for j in ks]
    m_rk = [bf(jnp.where(incl[dirs[j]], gm[j][LANES:, LANES:], 0.0)) for j in ks]
    for j, (s, p) in enumerate(keys):
        y_st = (_dot_nt(r_st[j], s_bf[j]) + jnp.dot(m_rb[j], bf(u_st[j]), preferred_element_type=F32)
                + jnp.dot(m_rk[j], v_st[j], preferred_element_type=F32))
        refs[streams[s][0]][6][streams[s][1], :, sls[p]] = unstack(y_st)
    for j, (s, p) in enumerate(keys):
        uv = jnp.concatenate([bf(unstack(u_st[j])), bf(v_all[s][:, sls[p]])], axis=0)
        bk = jnp.concatenate([b_p[j], k_p[j]], axis=0)
        upd = _dot_tn(uv, bk)
        s_scr[streams[s][0], streams[s][1], p] = g_tot[s][:, sls[p]] * (s_bd[j] + jnp.where(same, upd, 0.0))


def _rwkv_scan(r, v, kk, lw, bb, kd, ctx_len):
    b, tt, _ = r.shape
    c = RWKV_CHUNK
    n = tt // c
    nc = ctx_len // c
    shared = [pl.BlockSpec((b, c, MIX_W), functools.partial(lambda i, d: (0, _scan_chunk(d, i, n, nc), 0), d=d))
              for d in range(2)]
    per_dir = [pl.BlockSpec((1, b, c, MIX_W), functools.partial(lambda i, d: (d, 0, _scan_chunk(d, i, n, nc), 0), d=d))
               for d in range(2)]
    out = jax.ShapeDtypeStruct((b, tt, MIX_W), F32)
    return pl.pallas_call(
        _rwkv_scan_kernel, grid=(n,),
        in_specs=[*shared, *shared, *shared, *per_dir, *per_dir, *per_dir], out_specs=shared, out_shape=[out, out],
        scratch_shapes=[pltpu.VMEM((2, b, RWKV_HEADS // 2, LANES, LANES), F32)],
        compiler_params=_cparams("arbitrary"), name="rwkv_scan",
    )(r, r, v, v, kk, kk, lw, lw, bb, bb, kd, kd)


def _rwkv_finish_kernel(y0_ref, y1_ref, bonus_ref, g_ref, w_ref, b_ref, e_ref, et_ref, o_ref):
    e, et = e_ref[...], et_ref[...]
    y = y0_ref[0] + y1_ref[0]
    mu = _dot_split(_dot_split(y, e), et) * (1.0 / RWKV_HD)
    yc = y - mu
    var = _dot_split(_dot_split(yc * yc, e), et) * (1.0 / RWKV_HD)
    yn = yc * lax.rsqrt(var + RWKV_GN_EPS) * w_ref[...] + b_ref[...]
    o_ref[0] = ((yn + bonus_ref[0]) * g_ref[0]).astype(o_ref.dtype)


def _rwkv_finish(y0, y1, bonus, g, gn_w, gn_b):
    b, tt, _ = y0.shape
    e, et = _head_sum_mats()
    row = pl.BlockSpec((1, ROW_TILE, MIX_W), lambda bb, i: (bb, i, 0))
    full = lambda shape: pl.BlockSpec(shape, lambda bb, i: (0,) * len(shape))
    return pl.pallas_call(
        _rwkv_finish_kernel, grid=(b, tt // ROW_TILE),
        in_specs=[row, row, row, row,
                  full((1, MIX_W)), full((1, MIX_W)), full((MIX_W, LANES)), full((LANES, MIX_W))],
        out_specs=row, out_shape=jax.ShapeDtypeStruct((b, tt, MIX_W), BF16),
        compiler_params=_cparams("parallel", "parallel"), name="rwkv_finish",
    )(y0, y1, bonus, g, gn_w.reshape(1, MIX_W), gn_b.reshape(1, MIX_W), e, et)


def _rope_swap(x):
    lane = lax.broadcasted_iota(jnp.int32, x.shape, 1)
    return jnp.where((lane % 64) < 32, pltpu.roll(x, 96, 1), pltpu.roll(x, 32, 1))


def _rope_tables(t, ctx_len):
    pos = jnp.arange(t, dtype=jnp.int32)
    nf = RET_QK_HD // 4
    inv = ROPE_BASE ** (-jnp.arange(nf, dtype=F32) / nf)
    ang_r = (pos // GRID_W).astype(F32)[:, None] * inv[None, :]
    ang_c = (pos % GRID_W).astype(F32)[:, None] * inv[None, :]
    cos = jnp.concatenate([jnp.cos(ang_r)] * 2 + [jnp.cos(ang_c)] * 2, axis=-1)
    sin = jnp.concatenate([-jnp.sin(ang_r), jnp.sin(ang_r), -jnp.sin(ang_c), jnp.sin(ang_c)], axis=-1)
    cos = jnp.concatenate([jnp.ones((ctx_len, RET_QK_HD), F32), cos], axis=0)
    sin = jnp.concatenate([jnp.zeros((ctx_len, RET_QK_HD), F32), sin], axis=0)
    return cos, sin


def _ret_kernel(dec_ref, q0_ref, q1_ref, k0_ref, k1_ref, v0_ref, v1_ref, cos0_ref, cos1_ref, sin0_ref, sin1_ref,
                o0_ref, o1_ref, r_scr):
    i = pl.program_id(0)
    c = RET_CHUNK
    n_b = q0_ref.shape[0]

    @pl.when(i == 0)
    def _():
        r_scr[...] = jnp.zeros_like(r_scr)

    bf = lambda x: x.astype(BF16)
    ii = lax.broadcasted_iota(jnp.int32, (c, c), 0).astype(F32)
    jj = lax.broadcasted_iota(jnp.int32, (c, c), 1).astype(F32)
    heads = range(RET_HEADS)
    qs = [slice(h * RET_QK_HD, (h + 1) * RET_QK_HD) for h in heads]
    vs = [slice(h * RET_V_HD, (h + 1) * RET_V_HD) for h in heads]
    refs = ((q0_ref, k0_ref, v0_ref, cos0_ref, sin0_ref, o0_ref), (q1_ref, k1_ref, v1_ref, cos1_ref, sin1_ref, o1_ref))
    dmat, q_dec, k_dec, c_dec = {}, {}, {}, {}
    for d in range(2):
        pos = ii if d == 0 else c - 1.0 - ii
        diff = (ii - jj) if d == 0 else (jj - ii)
        for h in heads:
            lg = jnp.log(jax.nn.sigmoid(jnp.full((c, c), dec_ref[d, h], F32)))
            dmat[d, h] = jnp.where(diff >= 0, jnp.exp(lg * jnp.maximum(diff, 0.0)), 0.0)
            q_dec[d, h] = jnp.exp(lg * (pos + 1.0))
            k_dec[d, h] = jnp.exp(lg * (c - 1.0 - pos))
            c_dec[d, h] = jnp.exp(lg[:1, :1] * float(c))
    keys = [(d, bi, h) for d in range(2) for bi in range(n_b) for h in heads]
    q, k, v = {}, {}, {}
    for d, bi, h in keys:
        q_ref, k_ref, v_ref, cos_ref, sin_ref, _ = refs[d]
        cos, sin = cos_ref[...], sin_ref[...]
        qq = q_ref[bi, :, qs[h]]
        kk = k_ref[bi, :, qs[h]]
        q[d, bi, h] = qq * cos + _rope_swap(qq) * sin
        k[d, bi, h] = (kk * cos + _rope_swap(kk) * sin) * (RET_QK_HD ** -0.5)
        v[d, bi, h] = bf(v_ref[bi, :, vs[h]])
    r = {key: r_scr[key] for key in keys}
    s = {(d, bi, h): _dot_nt(bf(q[d, bi, h]), bf(k[d, bi, h])) * dmat[d, h] for d, bi, h in keys}
    for d, bi, h in keys:
        o = jnp.dot(bf(s[d, bi, h]), v[d, bi, h], preferred_element_type=F32)
        o = o + jnp.dot(bf(q[d, bi, h] * q_dec[d, h]), bf(r[d, bi, h]), preferred_element_type=F32)
        refs[d][5][bi, :, vs[h]] = o
    for d, bi, h in keys:
        r_scr[d, bi, h] = r[d, bi, h] * c_dec[d, h] + _dot_tn(bf(k[d, bi, h] * k_dec[d, h]), v[d, bi, h])


def _retention(p3, cos, sin, decay, ctx_len):
    b, tt, _ = p3.shape
    c = RET_CHUNK
    n = tt // c
    nc = ctx_len // c
    qb, kb, vb = OFF_RET_Q // RET_QK_W, OFF_RET_K // RET_QK_W, OFF_RET_V // MIX_W

    def per_dir(make):
        return [make(functools.partial(lambda i, dec, d: _scan_chunk(d, i, n, nc), d=d)) for d in range(2)]

    col = lambda width, cb: per_dir(lambda tok: pl.BlockSpec((b, c, width), lambda i, dec: (0, tok(i, dec), cb)))
    tab = per_dir(lambda tok: pl.BlockSpec((c, RET_QK_HD), lambda i, dec: (tok(i, dec), 0)))
    out = jax.ShapeDtypeStruct((b, tt, MIX_W), F32)
    grid_spec = pltpu.PrefetchScalarGridSpec(
        num_scalar_prefetch=1, grid=(n,),
        in_specs=[*col(RET_QK_W, qb), *col(RET_QK_W, kb), *col(MIX_W, vb), *tab, *tab],
        out_specs=col(MIX_W, 0),
        scratch_shapes=[pltpu.VMEM((2, b, RET_HEADS, RET_QK_HD, RET_V_HD), F32)])
    return pl.pallas_call(
        _ret_kernel, grid_spec=grid_spec, out_shape=[out, out],
        compiler_params=_cparams("arbitrary"), name="retention",
    )(decay, p3, p3, p3, p3, p3, p3, cos, cos, sin, sin)


def _ret_finish_kernel(o0_ref, o1_ref, g_ref, w_ref, b_ref, y_ref):
    for h in range(RET_HEADS):
        sl = slice(h * RET_V_HD, (h + 1) * RET_V_HD)
        o = o0_ref[0, :, sl] + o1_ref[0, :, sl]
        y = _norm(o, RET_GN_EPS) * w_ref[:, sl] + b_ref[:, sl]
        g = g_ref[0, :, sl]
        y_ref[0, :, sl] = (y * (g * jax.nn.sigmoid(g))).astype(y_ref.dtype)


def _ret_finish(o0, o1, p3, gn_w, gn_b):
    b, tt, _ = o0.shape
    gb = OFF_RET_G // MIX_W
    row = pl.BlockSpec((1, ROW_TILE, MIX_W), lambda bb, i: (bb, i, 0))
    vec = pl.BlockSpec((1, MIX_W), lambda bb, i: (0, 0))
    return pl.pallas_call(
        _ret_finish_kernel, grid=(b, tt // ROW_TILE),
        in_specs=[row, row, pl.BlockSpec((1, ROW_TILE, MIX_W), lambda bb, i: (bb, i, gb)), vec, vec],
        out_specs=row, out_shape=jax.ShapeDtypeStruct((b, tt, MIX_W), BF16),
        compiler_params=_cparams("parallel", "parallel"), name="ret_finish",
    )(o0, o1, p3, gn_w.reshape(1, MIX_W), gn_b.reshape(1, MIX_W))


NA_BLOCK_ROWS = 4
NA_BLOCK_TOK = NA_BLOCK_ROWS * GRID_W
NA_WIN_TOK = NA_WIN_ROWS * GRID_W


def _softmax_pv(s_parts, v_parts):
    m = None
    for s in s_parts:
        mm = jnp.max(s, axis=1, keepdims=True)
        m = mm if m is None else jnp.maximum(m, mm)
    den, acc = None, None
    for s, v in zip(s_parts, v_parts):
        e = jnp.exp(s - m)
        dd = jnp.sum(e, axis=1, keepdims=True)
        pv = jnp.dot(e.astype(BF16), v, preferred_element_type=F32)
        den = dd if den is None else den + dd
        acc = pv if acc is None else acc + pv
    return acc / den


def _na_kernel(q_ref, kp_ref, kc_ref, kn_ref, vp_ref, vc_ref, vn_ref, kx_ref, vx_ref, bias_ref, o_ref,
               k_scr, v_scr, *, n_blocks):
    rb = pl.program_id(1)
    n_b = q_ref.shape[0]
    batch = range(n_b)
    lane = lax.broadcasted_iota(jnp.int32, (1, LANES), 1)
    head_masks = [(lane < NA_HD).astype(F32), (lane >= NA_HD).astype(F32)]
    heads = range(2)
    kx = [kx_ref[bi].astype(BF16) for bi in batch]
    vx = [vx_ref[bi].astype(BF16) for bi in batch]
    scale = NA_HD ** -0.5

    @pl.when(rb == 0)
    def _():
        for bi in batch:
            q = q_ref[bi] * scale
            out = jnp.zeros((NA_BLOCK_TOK, LANES), F32)
            for hm in head_masks:
                s = _dot_nt((q * hm).astype(BF16), kx[bi])
                out = out + _softmax_pv([s], [vx[bi]]) * hm
            o_ref[bi] = out.astype(o_ref.dtype)

    @pl.when(rb > 0)
    def _():
        t = NA_BLOCK_TOK
        for bi in batch:
            k_scr[bi, 0:t, :] = kp_ref[bi].astype(BF16)
            k_scr[bi, t:2 * t, :] = kc_ref[bi].astype(BF16)
            k_scr[bi, 2 * t:3 * t, :] = kn_ref[bi].astype(BF16)
            v_scr[bi, 0:t, :] = vp_ref[bi].astype(BF16)
            v_scr[bi, t:2 * t, :] = vc_ref[bi].astype(BF16)
            v_scr[bi, 2 * t:3 * t, :] = vn_ref[bi].astype(BF16)
        first = rb == 1
        last = rb == n_blocks
        rows = range(NA_BLOCK_ROWS)
        rs = [slice(j * GRID_W, (j + 1) * GRID_W) for j in rows]
        starts, oi = [], []
        for j in rows:
            off = jnp.where(last, 0, jnp.where(first, NA_BLOCK_ROWS, j))
            oi.append(jnp.where(last, NA_BLOCK_ROWS + j, jnp.where(first, j, NA_WIN_ROWS // 2)))
            starts.append(pl.multiple_of(off * GRID_W, GRID_W))
        kw = [[k_scr[bi, pl.ds(starts[j], NA_WIN_TOK), :] for j in rows] for bi in batch]
        vw = [[v_scr[bi, pl.ds(starts[j], NA_WIN_TOK), :] for j in rows] for bi in batch]
        bias = [[bias_ref[oi[j], h] for h in heads] for j in rows]
        qh = [[(q_ref[bi] * scale * hm).astype(BF16) for hm in head_masks] for bi in batch]
        bh = [(bi, h) for bi in batch for h in heads]
        bjh = [(bi, j, h) for bi in batch for j in rows for h in heads]
        s_ctx = {(bi, h): _dot_nt(qh[bi][h], kx[bi]) for bi, h in bh}
        s_loc = {(bi, j, h): _dot_nt(qh[bi][h][rs[j]], kw[bi][j]) + bias[j][h] for bi, j, h in bjh}
        m_ctx = {k: jnp.max(s_ctx[k], axis=1, keepdims=True) for k in bh}
        m = {(bi, j, h): jnp.maximum(jnp.max(s_loc[bi, j, h], axis=1, keepdims=True), m_ctx[bi, h][rs[j]])
             for bi, j, h in bjh}
        m_all = {(bi, h): jnp.concatenate([m[bi, j, h] for j in rows], axis=0) for bi, h in bh}
        e_ctx = {k: jnp.exp(s_ctx[k] - m_all[k]) for k in bh}
        e_loc = {k: jnp.exp(s_loc[k] - m[k]) for k in bjh}
        d_ctx = {k: jnp.sum(e_ctx[k], axis=1, keepdims=True) for k in bh}
        pv_ctx = {(bi, h): jnp.dot(e_ctx[bi, h].astype(BF16), vx[bi], preferred_element_type=F32) for bi, h in bh}
        for bi in batch:
            for j in rows:
                out = None
                for h in heads:
                    den = jnp.sum(e_loc[bi, j, h], axis=1, keepdims=True) + d_ctx[bi, h][rs[j]]
                    pv = jnp.dot(e_loc[bi, j, h].astype(BF16), vw[bi][j], preferred_element_type=F32) + pv_ctx[bi, h][rs[j]]
                    term = pv / den * head_masks[h]
                    out = term if out is None else out + term
                o_ref[bi, rs[j], :] = out.astype(o_ref.dtype)


def _na_bias_table(rpb):
    col = jnp.arange(GRID_W)
    cs = jnp.clip(col - NA_WIN_COLS // 2, 0, GRID_W - NA_WIN_COLS)
    col_ok = (col[None, :] >= cs[:, None]) & (col[None, :] < cs[:, None] + NA_WIN_COLS)
    col_idx = jnp.clip(col[None, :] - col[:, None] + (NA_WIN_COLS - 1), 0, 2 * NA_WIN_COLS - 2)
    win = jnp.arange(NA_WIN_ROWS)
    row_idx = win[None, :] - win[:, None] + (NA_WIN_ROWS - 1)
    row_hot = (row_idx[:, :, None] == jnp.arange(2 * NA_WIN_ROWS - 1)).astype(F32)
    col_hot = (col_idx[:, :, None] == jnp.arange(2 * NA_WIN_COLS - 1)).astype(F32)
    bias = jnp.einsum("ora,hab,qcb->ohqrc", row_hot, rpb.astype(F32), col_hot, precision=HI)
    bias = jnp.where(col_ok[None, None, :, None, :], bias, -1e30)
    return bias.reshape(NA_WIN_ROWS, NA_HEADS, GRID_W, NA_WIN_TOK)


def _na(p3, bias_tab, ctx_len):
    b, tt, _ = p3.shape
    assert ctx_len == NA_BLOCK_TOK
    t = tt - ctx_len
    n_rows = t // GRID_W
    assert n_rows % NA_BLOCK_ROWS == 0 and n_rows >= NA_WIN_ROWS
    nb = n_rows // NA_BLOCK_ROWS
    qb, kb, vb = OFF_NA_Q // LANES, OFF_NA_K // LANES, OFF_NA_V // LANES
    blk = (b, NA_BLOCK_TOK, LANES)

    def spec(colb, shift):
        return pl.BlockSpec(blk, lambda pr, rb: (0, jnp.clip(rb + shift, 1, nb), colb + pr))

    return pl.pallas_call(
        functools.partial(_na_kernel, n_blocks=nb), grid=(NA_HEADS // 2, nb + 1),
        in_specs=[pl.BlockSpec(blk, lambda pr, rb: (0, rb, qb + pr)),
                  spec(kb, -1), spec(kb, 0), spec(kb, 1), spec(vb, -1), spec(vb, 0), spec(vb, 1),
                  pl.BlockSpec(blk, lambda pr, rb: (0, 0, kb + pr)),
                  pl.BlockSpec(blk, lambda pr, rb: (0, 0, vb + pr)),
                  pl.BlockSpec((NA_WIN_ROWS, 2, GRID_W, NA_WIN_TOK), lambda pr, rb: (0, pr, 0, 0))],
        out_specs=pl.BlockSpec(blk, lambda pr, rb: (0, rb, pr)),
        out_shape=jax.ShapeDtypeStruct((b, tt, MIX_W), BF16),
        scratch_shapes=[pltpu.VMEM((b, 3 * NA_BLOCK_TOK, LANES), BF16), pltpu.VMEM((b, 3 * NA_BLOCK_TOK, LANES), BF16)],
        compiler_params=_cparams("parallel", "parallel"), name="na",
    )(p3, p3, p3, p3, p3, p3, p3, p3, p3, bias_tab)


def _build_mod(gate, shift, scale, b):
    rows = jnp.stack([gate, shift, scale], axis=1)
    lat = rows[:b]
    ctx = jnp.broadcast_to(rows[b][None], lat.shape)
    mod = jnp.stack([ctx, lat], axis=1)
    return jnp.pad(mod, ((0, 0), (0, 0), (0, 5), (0, 0)))


def kernel(x, c, ctx, c_ctx, ada_w, ada_b, w_in, rwkv_shift, rwkv_w0, rwkv_wB, rwkv_a0, rwkv_aB, rwkv_gB, rwkv_kk, rwkv_ka, rwkv_rk, rwkv_gn_w, rwkv_gn_b, ret_decay, ret_gn_w, ret_gn_b, na_rpb, w_branch, w_out, ln1_w, ln1_b, ln2_w, ln2_b, router_w, router_b, exp_w1, exp_b1, exp_w2, exp_b2):
    b, t, d = x.shape
    ctx_len = ctx.shape[1]
    tt = ctx_len + t
    n = b * tt
    tpb = tt // ROW_TILE
    depth = ada_w.shape[0]
    alpha = (2 * depth) ** 0.25
    assert d == D_MODEL and b + 1 <= 8 and ctx_len == ROW_TILE and t % ROW_TILE == 0

    xa = jnp.concatenate([ctx, x], axis=1).reshape(n, d)
    cond = jnp.concatenate([c, c_ctx[None], jnp.zeros((8 - b - 1, d), F32)], axis=0)
    ada = _adaln(cond.T, ada_w, ada_b, b + 1).reshape(depth, 8, 6, d)
    cos, sin = _rope_tables(t, ctx_len)
    n_exp = router_w.shape[-1]
    rw_pad = jnp.pad(router_w, ((0, 0), (0, 0), (0, LANES - n_exp)))
    rb_pad = jnp.pad(router_b, ((0, 0), (0, LANES - n_exp)), constant_values=-1e30)
    zero = jnp.zeros((8, d), F32)

    h = _modulate(xa, _build_mod(zero, ada[0, :, 0], ada[0, :, 1], b), tpb)
    for l in range(depth):
        p = _in_proj(h, w_in, l)
        p3 = p.reshape(b, tt, D_IN_PROJ)

        r, v, kk, g, bonus, lw, bb, kd = _rwkv_prep(p3, rwkv_shift[l], rwkv_w0[l], rwkv_wB[l], rwkv_a0[l], rwkv_aB[l],
                                                    rwkv_gB[l], rwkv_kk[l], rwkv_ka[l], rwkv_rk[l], ctx_len)
        y_fwd, y_bwd = _rwkv_scan(r, v, kk, lw, bb, kd, ctx_len)
        ya = _rwkv_finish(y_fwd, y_bwd, bonus, g, rwkv_gn_w[l], rwkv_gn_b[l])

        o_fwd, o_bwd = _retention(p3, cos, sin, ret_decay[l], ctx_len)
        yb = _ret_finish(o_fwd, o_bwd, p3, ret_gn_w[l], ret_gn_b[l])

        yn = _na(p3, _na_bias_table(na_rpb[l]), ctx_len)

        merged = _merge(ya.reshape(n, MIX_W), yb.reshape(n, MIX_W), yn.reshape(n, MIX_W), p, w_branch[l].astype(BF16))
        y = _matmul(merged, w_out[l].astype(BF16), 512)

        mod_f = _build_mod(ada[l, :, 2], ada[l, :, 3], ada[l, :, 4], b)
        x1, h2, route = _resid_ln(xa, y, mod_f, ln1_w[l], ln1_b[l], tpb, alpha, rw_pad[l], rb_pad[l][None])
        tok_of_pos, pos_of_slot, tile_expert, n_used = _moe_dispatch(route, n_exp)
        xs = jnp.take(h2, tok_of_pos, axis=0, mode="clip")
        ys = _moe(xs, tile_expert, n_used, exp_w1, exp_b1[l], exp_w2, exp_b2[l], l)
        f4 = jnp.take(ys, pos_of_slot.T.reshape(-1), axis=0, mode="clip").reshape(TOP_K, n, d)
        nxt = min(l + 1, depth - 1)
        mod_a = _build_mod(ada[l, :, 5], ada[nxt, :, 0], ada[nxt, :, 1], b)
        xa, h = _resid_ln(x1, f4, mod_a, ln2_w[l], ln2_b[l], tpb, alpha, route=route)
    return xa.reshape(b, tt, d)[:, ctx_len:]
```

```python
import functools
import math

import jax
import jax.numpy as jnp
from jax import lax
from jax.experimental import pallas as pl
from jax.experimental.pallas import tpu as pltpu

F32 = jnp.float32
BF16 = jnp.bfloat16
HI = lax.Precision.HIGHEST

D_MODEL = 2048
GRID_W = 64
MIX_W = D_MODEL // 2
N_BRANCH = 3

RWKV_HD = 64
RWKV_HEADS = MIX_W // RWKV_HD
RWKV_DECAY_RANK = 64
RWKV_AAA_RANK = 64
RWKV_GATE_RANK = 128
RWKV_SHIFT_W = 3 * MIX_W + RWKV_DECAY_RANK + RWKV_AAA_RANK + RWKV_GATE_RANK
RWKV_GN_EPS = 64e-5
RWKV_CHUNK = 64

RET_HEADS = 4
RET_V_HD = MIX_W // RET_HEADS
RET_QK_HD = RET_V_HD // 2
RET_QK_W = RET_HEADS * RET_QK_HD
RET_CHUNK = 128
RET_GN_EPS = 1e-6

NA_HD = 64
NA_HEADS = MIX_W // NA_HD
NA_WIN_ROWS = 8
NA_WIN_COLS = 16

N_EXPERTS = 32
TOP_K = 4
D_EXPERT = D_MODEL // 4
SWIGLU_LIMIT = 7.0
SWIGLU_ALPHA = 1.702

ROPE_BASE = 10000.0
LN_EPS = 1e-6

IN_PROJ_TN = 256
D_IN_PROJ = RWKV_SHIFT_W + 2 * RET_QK_W + 5 * MIX_W + N_BRANCH * D_MODEL
IN_PROJ_ROT = RWKV_SHIFT_W // IN_PROJ_TN
OFF_RET_Q = 0
OFF_RET_K = OFF_RET_Q + RET_QK_W
OFF_RET_V = OFF_RET_K + RET_QK_W
OFF_RET_G = OFF_RET_V + MIX_W
OFF_NA_Q = OFF_RET_G + MIX_W
OFF_NA_K = OFF_NA_Q + MIX_W
OFF_NA_V = OFF_NA_K + MIX_W
OFF_GATE = OFF_NA_V + MIX_W
OFF_RWKV = OFF_GATE + N_BRANCH * D_MODEL
OFF_RWKV_LORA = OFF_RWKV + 3 * MIX_W
RWKV_LORA_W = RWKV_SHIFT_W - 3 * MIX_W

ROW_TILE = 256
LANES = 128
VMEM_LIMIT = 56 * 1024 * 1024


def _cparams(*sem):
    return pltpu.CompilerParams(dimension_semantics=sem, vmem_limit_bytes=VMEM_LIMIT)


def _adaln_kernel(ct_ref, w_ref, b_ref, o_ref, *, n_cond):
    ct = ct_ref[...]
    st = ct * jax.nn.sigmoid(ct)
    w = w_ref[0]
    o_ref[0] = jnp.zeros(o_ref.shape[1:], F32)
    for m in range(n_cond):
        o_ref[0, m:m + 1, :] = jnp.sum(st[:, m:m + 1] * w, axis=0, keepdims=True) + b_ref[0]


def _adaln(cond_t, ada_w, ada_b, n_cond):
    n_layers, d, d6 = ada_w.shape
    tn = 512
    return pl.pallas_call(
        functools.partial(_adaln_kernel, n_cond=n_cond),
        grid=(n_layers, d6 // tn),
        in_specs=[pl.BlockSpec((d, 8), lambda l, j: (0, 0)),
                  pl.BlockSpec((1, d, tn), lambda l, j: (l, 0, j)),
                  pl.BlockSpec((1, 1, tn), lambda l, j: (l, 0, j))],
        out_specs=pl.BlockSpec((1, 8, tn), lambda l, j: (l, 0, j)),
        out_shape=jax.ShapeDtypeStruct((n_layers, 8, d6), F32),
        compiler_params=_cparams("parallel", "parallel"),
        name="adaln",
    )(cond_t, ada_w, ada_b.reshape(n_layers, 1, d6))


def _norm(x, eps):
    mu = jnp.mean(x, axis=-1, keepdims=True)
    xc = x - mu
    var = jnp.mean(xc * xc, axis=-1, keepdims=True)
    return xc * lax.rsqrt(var + eps)


def _top4_route(logits):
    lane = lax.broadcasted_iota(jnp.int32, logits.shape, 1).astype(F32)
    l = logits
    idxs, vals = [], []
    for k in range(TOP_K):
        m = jnp.max(l, axis=1, keepdims=True)
        idx = jnp.min(jnp.where(l == m, lane, float(LANES)), axis=1, keepdims=True)
        l = jnp.where(lane == idx, -jnp.inf, l)
        idxs.append(idx)
        vals.append(m)
    es = [jnp.exp(v - vals[0]) for v in vals]
    den = es[0] + es[1] + es[2] + es[3]
    out = jnp.zeros(logits.shape, F32)
    for k in range(TOP_K):
        out = jnp.where(lane == float(k), idxs[k], out)
        out = jnp.where(lane == float(TOP_K + k), es[k] / den, out)
    return out


def _mod_kernel(x_ref, mod_ref, h_ref):
    md = mod_ref[0, 0]
    h = _norm(x_ref[...], LN_EPS) * (1.0 + md[2:3, :]) + md[1:2, :]
    h_ref[...] = h.astype(BF16)


def _resid_kernel(*refs, alpha, router, combine):
    refs = list(refs)
    x_ref, y_ref, mod_ref, lnw_ref, lnb_ref = refs[:5]
    rest = refs[5:]
    if combine:
        route_ref, rest = rest[0], rest[1:]
    if router:
        rw_ref, rb_ref, rest = rest[0], rest[1], rest[2:]
    xo_ref, h_ref = rest[:2]
    md = mod_ref[0, 0]
    if combine:
        route = route_ref[...]
        y = None
        for k in range(TOP_K):
            t = route[:, TOP_K + k:TOP_K + k + 1] * y_ref[k].astype(F32)
            y = t if y is None else y + t
    else:
        y = y_ref[...]
    z = alpha * x_ref[...] + md[0:1, :] * y
    x1 = _norm(z, LN_EPS) * lnw_ref[...] + lnb_ref[...]
    xo_ref[...] = x1
    h = _norm(x1, LN_EPS) * (1.0 + md[2:3, :]) + md[1:2, :]
    h_ref[...] = h.astype(BF16)
    if router:
        h_hi = h.astype(BF16)
        h_lo = (h - h_hi.astype(F32)).astype(BF16)
        dotf = functools.partial(jnp.dot, preferred_element_type=F32)
        logits = dotf(h_hi, rw_ref[0]) + dotf(h_lo, rw_ref[0]) + dotf(h_hi, rw_ref[1]) + rb_ref[...]
        rest[2][...] = _top4_route(logits)


def _row_specs(n_rows, tiles_per_batch, d):
    row = pl.BlockSpec((ROW_TILE, d), lambda i: (i, 0))
    mod = pl.BlockSpec((1, 1, 8, d), lambda i: (i // tiles_per_batch, jnp.minimum(i % tiles_per_batch, 1), 0, 0))
    vec = pl.BlockSpec((1, d), lambda i: (0, 0))
    return row, mod, vec


def _modulate(x, mod, tiles_per_batch):
    n, d = x.shape
    row, modspec, _ = _row_specs(n, tiles_per_batch, d)
    return pl.pallas_call(
        _mod_kernel, grid=(n // ROW_TILE,),
        in_specs=[row, modspec], out_specs=row,
        out_shape=jax.ShapeDtypeStruct((n, d), BF16),
        compiler_params=_cparams("parallel"), name="modulate",
    )(x, mod)


def _resid_ln(x, y, mod, lnw, lnb, tiles_per_batch, alpha, router_w=None, router_b=None, route=None):
    n, d = x.shape
    row, modspec, vec = _row_specs(n, tiles_per_batch, d)
    lanes = pl.BlockSpec((ROW_TILE, LANES), lambda i: (i, 0))
    router = router_w is not None
    combine = route is not None
    y_spec = pl.BlockSpec((TOP_K, ROW_TILE, d), lambda i: (0, i, 0)) if combine else row
    in_specs = [row, y_spec, modspec, vec, vec]
    args = [x, y, mod, lnw.reshape(1, d), lnb.reshape(1, d)]
    out_specs = [row, row]
    out_shape = [jax.ShapeDtypeStruct((n, d), F32), jax.ShapeDtypeStruct((n, d), BF16)]
    if combine:
        in_specs.append(lanes)
        args.append(route)
    if router:
        in_specs += [pl.BlockSpec((2, d, LANES), lambda i: (0, 0, 0)), pl.BlockSpec((1, LANES), lambda i: (0, 0))]
        args += [router_w, router_b]
        out_specs.append(lanes)
        out_shape.append(jax.ShapeDtypeStruct((n, LANES), F32))
    return pl.pallas_call(
        functools.partial(_resid_kernel, alpha=alpha, router=router, combine=combine),
        grid=(n // ROW_TILE,), in_specs=in_specs, out_specs=out_specs, out_shape=out_shape,
        compiler_params=_cparams("parallel"), name="resid_ln",
    )(*args)


def _matmul_kernel(x_ref, w_ref, o_ref):
    o_ref[...] = jnp.dot(x_ref[...], w_ref[...], preferred_element_type=F32).astype(o_ref.dtype)


def _pick_tm(n, cap):
    tm = cap
    while n % tm:
        tm -= 16
    return tm


def _matmul(x, w, tn, out_dtype=F32, tm_cap=1536):
    n, k = x.shape
    _, m = w.shape
    tm = _pick_tm(n, tm_cap)
    return pl.pallas_call(
        _matmul_kernel, grid=(n // tm, m // tn),
        in_specs=[pl.BlockSpec((tm, k), lambda i, j: (i, 0)), pl.BlockSpec((k, tn), lambda i, j: (0, j))],
        out_specs=pl.BlockSpec((tm, tn), lambda i, j: (i, j)),
        out_shape=jax.ShapeDtypeStruct((n, m), out_dtype),
        compiler_params=_cparams("parallel", "parallel"), name="matmul",
    )(x, w)


def _in_proj_kernel(x_ref, w_ref, o_ref):
    o_ref[...] = jnp.dot(x_ref[...], w_ref[...].astype(BF16), preferred_element_type=F32)


def _in_proj(x, w, layer):
    n, k = x.shape
    tm = _pick_tm(n, 2816)
    nblk = D_IN_PROJ // IN_PROJ_TN
    return pl.pallas_call(
        _in_proj_kernel, grid=(n // tm, nblk),
        in_specs=[pl.BlockSpec((tm, k), lambda i, j: (i, 0)),
                  pl.BlockSpec((None, k, IN_PROJ_TN), lambda i, j: (layer, 0, (j + IN_PROJ_ROT) % nblk))],
        out_specs=pl.BlockSpec((tm, IN_PROJ_TN), lambda i, j: (i, j)),
        out_shape=jax.ShapeDtypeStruct((n, D_IN_PROJ), F32),
        compiler_params=_cparams("parallel", "parallel"), name="in_proj",
    )(x, w)


def _merge_kernel(ya_ref, yb_ref, yn_ref, g0_ref, g1_ref, g2_ref, wb_ref, o_ref):
    acc = None
    for i, (y_ref, g_ref) in enumerate(((ya_ref, g0_ref), (yb_ref, g1_ref), (yn_ref, g2_ref))):
        z = jnp.dot(y_ref[...], wb_ref[i], preferred_element_type=F32)
        t = jax.nn.sigmoid(g_ref[...]) * z
        acc = t if acc is None else acc + t
    o_ref[...] = acc.astype(o_ref.dtype)


def _merge(ya, yb, yn, p, w_branch):
    n, mw = ya.shape
    d = w_branch.shape[-1]
    tn = 512
    tm = _pick_tm(n, 768)
    gate_blk = OFF_GATE // tn
    y_spec = pl.BlockSpec((tm, mw), lambda i, j: (i, 0))
    g_specs = [pl.BlockSpec((tm, tn), functools.partial(lambda i, j, b: (i, gate_blk + b * (d // tn) + j), b=b))
               for b in range(N_BRANCH)]
    return pl.pallas_call(
        _merge_kernel, grid=(n // tm, d // tn),
        in_specs=[y_spec, y_spec, y_spec, *g_specs, pl.BlockSpec((N_BRANCH, mw, tn), lambda i, j: (0, 0, j))],
        out_specs=pl.BlockSpec((tm, tn), lambda i, j: (i, j)),
        out_shape=jax.ShapeDtypeStruct((n, d), BF16),
        compiler_params=_cparams("parallel", "parallel"), name="merge",
    )(ya, yb, yn, p, p, p, w_branch)


MOE_TM = 512


def _moe_dispatch(route, n_exp):
    n = route.shape[0]
    n_slots = n * TOP_K
    n_tiles = (n_slots + n_exp * (MOE_TM - 1)) // MOE_TM + 1
    i32 = jnp.int32
    experts = jnp.arange(n_exp, dtype=i32)
    flat = route[:, :TOP_K].astype(i32).reshape(n_slots)
    order = jnp.argsort(flat, stable=True).astype(i32)
    rank = jnp.argsort(order).astype(i32)
    onehot = (flat[:, None] == experts[None, :]).astype(i32)
    counts = jnp.sum(onehot, axis=0)
    padded = ((counts + MOE_TM - 1) // MOE_TM) * MOE_TM
    pad_end = jnp.cumsum(padded)
    pad_start = pad_end - padded
    start = jnp.cumsum(counts) - counts
    pos_of_slot = jnp.sum(onehot * (pad_start - start)[None, :], axis=1) + rank
    n_used = (pad_end[-1] // MOE_TM).astype(i32)
    tile_start = jnp.minimum(jnp.arange(n_tiles, dtype=i32), n_used - 1) * MOE_TM
    tile_expert = jnp.minimum(jnp.sum((tile_start[:, None] >= pad_end[None, :]).astype(i32), axis=1), n_exp - 1)
    tile_hot = (tile_expert[:, None] == experts[None, :]).astype(i32)
    row = jnp.arange(n_tiles, dtype=i32)[:, None] * MOE_TM + jnp.arange(MOE_TM, dtype=i32)[None, :]
    off = row - jnp.sum(tile_hot * pad_start[None, :], axis=1)[:, None]
    valid = (off < jnp.sum(tile_hot * counts[None, :], axis=1)[:, None]) & (row < n_used * MOE_TM)
    src = jnp.clip(jnp.sum(tile_hot * start[None, :], axis=1)[:, None] + off, 0, n_slots - 1)
    tok_of_pos = jnp.where(valid, jnp.take(order // TOP_K, src, mode="clip"), row % n).reshape(-1)
    return tok_of_pos, pos_of_slot.reshape(n, TOP_K), tile_expert, n_used.reshape(1)


def _moe_kernel(te_ref, nu_ref, x_ref, w1_ref, b1_ref, w2_ref, b2_ref, o_ref, w1_scr, w2_scr):
    t = pl.program_id(0)
    live = t < nu_ref[0]
    new_expert = (t == 0) | (te_ref[t] != te_ref[jnp.maximum(t - 1, 0)])

    @pl.when(live & new_expert)
    def _():
        w1_scr[...] = w1_ref[...].astype(BF16)
        w2_scr[...] = w2_ref[...].astype(BF16)

    @pl.when(live)
    def _():
        hu = jnp.dot(x_ref[...], w1_scr[...], preferred_element_type=F32) + b1_ref[0]
        f = hu.shape[1] // 2
        g = jnp.minimum(hu[:, :f], SWIGLU_LIMIT)
        u = jnp.clip(hu[:, f:], -SWIGLU_LIMIT, SWIGLU_LIMIT)
        act = g * jax.nn.sigmoid(SWIGLU_ALPHA * g) * (u + 1.0)
        y = jnp.dot(act.astype(BF16), w2_scr[...], preferred_element_type=F32) + b2_ref[0]
        o_ref[...] = y.astype(o_ref.dtype)

    @pl.when(jnp.logical_not(live))
    def _():
        o_ref[...] = jnp.zeros_like(o_ref)


def _moe(xs, tile_expert, n_used, w1, b1, w2, b2, layer):
    p, d = xs.shape
    _, n_exp, _, f2 = w1.shape
    grid_spec = pltpu.PrefetchScalarGridSpec(
        num_scalar_prefetch=2, grid=(p // MOE_TM,),
        in_specs=[pl.BlockSpec((MOE_TM, d), lambda t, te, nu: (t, 0)),
                  pl.BlockSpec((None, None, d, f2), lambda t, te, nu: (layer, te[t], 0, 0)),
                  pl.BlockSpec((1, 1, f2), lambda t, te, nu: (te[t], 0, 0)),
                  pl.BlockSpec((None, None, f2 // 2, d), lambda t, te, nu: (layer, te[t], 0, 0)),
                  pl.BlockSpec((1, 1, d), lambda t, te, nu: (te[t], 0, 0))],
        out_specs=pl.BlockSpec((MOE_TM, d), lambda t, te, nu: (t, 0)),
        scratch_shapes=[pltpu.VMEM((d, f2), BF16), pltpu.VMEM((f2 // 2, d), BF16)])
    return pl.pallas_call(
        _moe_kernel, grid_spec=grid_spec,
        out_shape=jax.ShapeDtypeStruct((p, d), BF16),
        compiler_params=_cparams("arbitrary"), name="moe",
    )(tile_expert, n_used, xs, w1, b1.reshape(n_exp, 1, f2), w2, b2.reshape(n_exp, 1, d))


def _dot_nt(a, b, precision=None):
    return lax.dot_general(a, b, (((1,), (1,)), ((), ())), preferred_element_type=F32, precision=precision)


def _dot_tn(a, b, precision=None):
    return lax.dot_general(a, b, (((0,), (0,)), ((), ())), preferred_element_type=F32, precision=precision)


def _scan_chunk(d, i, n_chunks, n_ctx_chunks):
    rev = jnp.where(i < n_ctx_chunks, n_ctx_chunks - 1 - i, n_chunks - 1 + n_ctx_chunks - i)
    return jnp.where(d == 0, i, rev)


def _head_sum_mats():
    ch = jnp.arange(MIX_W) // RWKV_HD
    e = (ch[:, None] == jnp.arange(LANES)[None, :]).astype(BF16)
    return e, e.T


def _dot_split(x, m_bf16):
    hi = x.astype(BF16)
    lo = (x - hi.astype(F32)).astype(BF16)
    return (jnp.dot(hi, m_bf16, preferred_element_type=F32) + jnp.dot(lo, m_bf16, preferred_element_type=F32))


def _rwkv_prep_kernel(*refs, tiles_per_batch):
    pieces, rest = [refs[3 * j:3 * j + 3] for j in range(4)], refs[12:]
    (shift_ref, w0_ref, wb_ref, a0_ref, ab_ref, gb_ref, kk_ref, ka_ref, rk_ref, e_ref, et_ref,
     r_out, v_out, kk_out, g_out, bonus_out, lw_out, b_out, kd_out) = rest
    i = pl.program_id(1)
    rows = lax.broadcasted_iota(jnp.int32, (ROW_TILE, 1), 0)
    has_prev = jnp.where(i > 1, 1.0, 0.0)
    has_next = jnp.where((i > 0) & (i < tiles_per_batch - 1), 1.0, 0.0)

    def token_shift(piece, col0):
        p_ref, pp_ref, pn_ref = piece
        x = p_ref[0]
        w = x.shape[1]
        xp = jnp.where(rows == 0, pp_ref[0, 7:8, :] * has_prev, pltpu.roll(x, 1, 0))
        xn = jnp.where(rows == ROW_TILE - 1, pn_ref[0, 0:1, :] * has_next, pltpu.roll(x, ROW_TILE - 1, 0))
        sh = shift_ref[:, col0:col0 + w]
        return xp * sh[0:1, :] + x * sh[1:2, :] + xn * sh[2:3, :]

    m = MIX_W
    r, k, v = (token_shift(pieces[j], j * m) for j in range(3))
    lora = token_shift(pieces[3], 3 * m)
    wl = lora[:, :RWKV_DECAY_RANK]
    al = lora[:, RWKV_DECAY_RANK:RWKV_DECAY_RANK + RWKV_AAA_RANK]
    gl = lora[:, RWKV_DECAY_RANK + RWKV_AAA_RANK:]
    dot = lambda x, w: jnp.dot(x.astype(BF16), w.astype(BF16), preferred_element_type=F32)
    e, et = e_ref[...], et_ref[...]
    g_out[0] = dot(jax.nn.sigmoid(gl), gb_ref[...]).astype(g_out.dtype)
    kk0 = k * kk_ref[...]
    nrm = jnp.maximum(jnp.sqrt(_dot_split(kk0 * kk0, e)), 1e-12)
    kk = kk0 * _dot_split(1.0 / nrm, et)
    tw = jnp.tanh(wl)
    bsum = None
    for d in range(2):
        a = jax.nn.sigmoid(a0_ref[d:d + 1, :] + dot(al, ab_ref[d]))
        kd = k * (1.0 + (a - 1.0) * ka_ref[...])
        lw = (-math.exp(-0.5)) * jax.nn.sigmoid(w0_ref[d:d + 1, :] + dot(tw, wb_ref[d]))
        lw_out[d, 0] = lw
        b_out[d, 0] = (kk * a).astype(b_out.dtype)
        kd_out[d, 0] = kd.astype(kd_out.dtype)
        t = _dot_split(r * kd * rk_ref[...], e)
        bsum = t if bsum is None else bsum + t
    r_out[0] = r.astype(r_out.dtype)
    v_out[0] = v.astype(v_out.dtype)
    kk_out[0] = kk.astype(kk_out.dtype)
    bonus_out[0] = (_dot_split(bsum, et) * v).astype(bonus_out.dtype)


def _rwkv_prep(p3, shift, w0, wb, a0, ab, gb, k_k, k_a, r_k, ctx_len):
    b, tt, _ = p3.shape
    tpb = tt // ROW_TILE
    assert ctx_len == ROW_TILE
    e, et = _head_sum_mats()
    hb = ROW_TILE // 8
    full = lambda shape: pl.BlockSpec(shape, lambda bb, i: (0,) * len(shape))
    row1 = pl.BlockSpec((1, ROW_TILE, MIX_W), lambda bb, i: (bb, i, 0))
    row2 = pl.BlockSpec((2, 1, ROW_TILE, MIX_W), lambda bb, i: (0, bb, i, 0))
    s2 = jax.ShapeDtypeStruct((2, b, tt, MIX_W), F32)
    h1 = jax.ShapeDtypeStruct((b, tt, MIX_W), BF16)
    h2 = jax.ShapeDtypeStruct((2, b, tt, MIX_W), BF16)
    vec = lambda a: a.reshape(1, MIX_W)

    def piece(width, col_block):
        return [pl.BlockSpec((1, ROW_TILE, width), lambda bb, i: (bb, i, col_block)),
                pl.BlockSpec((1, 8, width), lambda bb, i: (bb, jnp.maximum(i * hb - 1, 0), col_block)),
                pl.BlockSpec((1, 8, width), lambda bb, i: (bb, jnp.minimum((i + 1) * hb, tpb * hb - 1), col_block))]

    pieces = [s for j in range(3) for s in piece(MIX_W, OFF_RWKV // MIX_W + j)] + piece(RWKV_LORA_W, OFF_RWKV_LORA // RWKV_LORA_W)
    return pl.pallas_call(
        functools.partial(_rwkv_prep_kernel, tiles_per_batch=tpb), grid=(b, tpb),
        in_specs=pieces + [full((3, RWKV_SHIFT_W)), full((2, MIX_W)), full((2, RWKV_DECAY_RANK, MIX_W)), full((2, MIX_W)),
                           full((2, RWKV_AAA_RANK, MIX_W)), full((RWKV_GATE_RANK, MIX_W)),
                           full((1, MIX_W)), full((1, MIX_W)), full((1, MIX_W)), full((MIX_W, LANES)), full((LANES, MIX_W))],
        out_specs=[row1, row1, row1, row1, row1, row2, row2, row2],
        out_shape=[h1, h1, h1, h1, h1, s2, h2, h2],
        compiler_params=_cparams("parallel", "parallel"), name="rwkv_prep",
    )(*([p3] * 12), shift, w0, wb, a0, ab, gb, vec(k_k), vec(k_a), vec(r_k), e, et)


def _rwkv_scan_kernel(r0_ref, r1_ref, v0_ref, v1_ref, kk0_ref, kk1_ref, lw0_ref, lw1_ref, b0_ref, b1_ref,
                      kd0_ref, kd1_ref, y0_ref, y1_ref, s_scr):
    i = pl.program_id(0)
    c = RWKV_CHUNK
    n_b = r0_ref.shape[0]

    @pl.when(i == 0)
    def _():
        s_scr[...] = jnp.zeros_like(s_scr)

    bf = lambda x: x.astype(BF16)
    mm = lambda x, y: jnp.dot(bf(x), bf(y), preferred_element_type=F32)

    def cumsum_dot(t_bf16, x):
        hi = bf(x)
        r1 = x - hi.astype(F32)
        mid = bf(r1)
        lo = bf(r1 - mid.astype(F32))
        return (jnp.dot(t_bf16, hi, preferred_element_type=F32) + jnp.dot(t_bf16, mid, preferred_element_type=F32)
                + jnp.dot(t_bf16, lo, preferred_element_type=F32))

    ti = lax.broadcasted_iota(jnp.int32, (c, c), 0)
    si = lax.broadcasted_iota(jnp.int32, (c, c), 1)
    ri = lax.broadcasted_iota(jnp.int32, (LANES, LANES), 0)
    ci = lax.broadcasted_iota(jnp.int32, (LANES, LANES), 1)
    same = (ri // c) == (ci // c)
    lane = lax.broadcasted_iota(jnp.int32, (1, LANES), 1)
    m0 = (lane < RWKV_HD).astype(F32)
    m1 = 1.0 - m0
    stack = lambda x: jnp.concatenate([x * m0, x * m1], axis=0)
    unstack = lambda x: x[:c] + x[c:]
    pairs = range(RWKV_HEADS // 2)
    sls = [slice(p * LANES, (p + 1) * LANES) for p in pairs]

    tri, strict, incl = [], [], []
    for sgn in (1, -1):
        tri.append(jnp.where((ti - si) * sgn >= 0, 1.0, 0.0).astype(BF16))
        strict.append(same & ((ri - ci) * sgn > 0))
        incl.append(same & ((ri - ci) * sgn >= 0))

    refs = ((r0_ref, v0_ref, kk0_ref, lw0_ref, b0_ref, kd0_ref, y0_ref),
            (r1_ref, v1_ref, kk1_ref, lw1_ref, b1_ref, kd1_ref, y1_ref))
    streams = [(d, bi) for d in range(2) for bi in range(n_b)]
    a_t, b_t, k_t, r_t, v_all, g_tot = [], [], [], [], [], []
    for d, bi in streams:
        r_ref, v_ref, kk_ref, lw_ref, b_ref, kd_ref, _ = refs[d]
        lw = lw_ref[0, bi]
        cum = cumsum_dot(tri[d], lw)
        e_neg = jnp.exp(-cum)
        a_t.append(-kk_ref[bi].astype(F32) * jnp.exp(cum - lw))
        b_t.append(b_ref[0, bi].astype(F32) * e_neg)
        k_t.append(kd_ref[0, bi].astype(F32) * e_neg)
        r_t.append(r_ref[bi].astype(F32) * jnp.exp(cum))
        v_all.append(v_ref[bi])
        g_tot.append(jnp.exp(jnp.sum(lw, axis=0, keepdims=True)))

    keys = [(s, p) for s in range(len(streams)) for p in pairs]
    dirs = [streams[s][0] for s, _ in keys]
    a_st = [bf(stack(a_t[s][:, sls[p]])) for s, p in keys]
    r_st = [bf(stack(r_t[s][:, sls[p]])) for s, p in keys]
    v_st = [bf(stack(v_all[s][:, sls[p]])) for s, p in keys]
    b_p = [bf(b_t[s][:, sls[p]]) for s, p in keys]
    k_p = [bf(k_t[s][:, sls[p]]) for s, p in keys]
    ks = range(len(keys))
    gm = [_dot_nt(jnp.concatenate([a_st[j], r_st[j]], axis=0),
                  jnp.concatenate([b_p[j], b_p[j], k_p[j], k_p[j]], axis=0)) for j in ks]
    m_ak = [bf(jnp.where(strict[dirs[j]], gm[j][:LANES, LANES:], 0.0)) for j in ks]
    x = [jnp.concatenate([a_st[j].astype(F32), jnp.dot(m_ak[j], v_st[j], preferred_element_type=F32)], axis=1)
         for j in ks]
    pw = [jnp.where(strict[dirs[j]], gm[j][:LANES, :LANES], 0.0) for j in ks]
    for step in range(6):
        if step:
            pw = [mm(pw[j], pw[j]) for j in ks]
        x = [x[j] + mm(pw[j], x[j]) for j in ks]
    s_bd = [s_scr[streams[s][0], streams[s][1], p] for s, p in keys]
    s_bf = [bf(t) for t in s_bd]
    u_st = [_dot_nt(bf(x[j][:, :LANES]), s_bf[j]) + x[j][:, LANES:] for j in ks]
    m_rb = [bf(jnp.where(incl[dirs[j]], gm[j][LANES:, :LANES], 0.0)) for j in ks]
    m_rk = [bf(jnp.where(incl[dirs[j]], gm[j][LANES:, LANES:], 0.0)) for j in ks]
    for j, (s, p) in enumerate(keys):
        y_st = (_dot_nt(r_st[j], s_bf[j]) + jnp.dot(m_rb[j], bf(u_st[j]), preferred_element_type=F32)
                + jnp.dot(m_rk[j], v_st[j], preferred_element_type=F32))
        refs[streams[s][0]][6][streams[s][1], :, sls[p]] = unstack(y_st).astype(y0_ref.dtype)
    for j, (s, p) in enumerate(keys):
        uv = jnp.concatenate([bf(unstack(u_st[j])), bf(v_all[s][:, sls[p]])], axis=0)
        bk = jnp.concatenate([b_p[j], k_p[j]], axis=0)
        upd = _dot_tn(uv, bk)
        s_scr[streams[s][0], streams[s][1], p] = g_tot[s][:, sls[p]] * (s_bd[j] + jnp.where(same, upd, 0.0))


def _rwkv_scan(r, v, kk, lw, bb, kd, ctx_len):
    b, tt, _ = r.shape
    c = RWKV_CHUNK
    n = tt // c
    nc = ctx_len // c
    shared = [pl.BlockSpec((b, c, MIX_W), functools.partial(lambda i, d: (0, _scan_chunk(d, i, n, nc), 0), d=d))
              for d in range(2)]
    per_dir = [pl.BlockSpec((1, b, c, MIX_W), functools.partial(lambda i, d: (d, 0, _scan_chunk(d, i, n, nc), 0), d=d))
               for d in range(2)]
    out = jax.ShapeDtypeStruct((b, tt, MIX_W), BF16)
    return pl.pallas_call(
        _rwkv_scan_kernel, grid=(n,),
        in_specs=[*shared, *shared, *shared, *per_dir, *per_dir, *per_dir], out_specs=shared, out_shape=[out, out],
        scratch_shapes=[pltpu.VMEM((2, b, RWKV_HEADS // 2, LANES, LANES), F32)],
        compiler_params=_cparams("arbitrary"), name="rwkv_scan",
    )(r, r, v, v, kk, kk, lw, lw, bb, bb, kd, kd)


def _rwkv_finish_kernel(y0_ref, y1_ref, bonus_ref, g_ref, w_ref, b_ref, e_ref, et_ref, o_ref):
    e, et = e_ref[...], et_ref[...]
    y = y0_ref[0].astype(F32) + y1_ref[0].astype(F32)
    mu = _dot_split(_dot_split(y, e), et) * (1.0 / RWKV_HD)
    yc = y - mu
    var = _dot_split(_dot_split(yc * yc, e), et) * (1.0 / RWKV_HD)
    yn = yc * lax.rsqrt(var + RWKV_GN_EPS) * w_ref[...] + b_ref[...]
    o_ref[0] = ((yn + bonus_ref[0].astype(F32)) * g_ref[0].astype(F32)).astype(o_ref.dtype)


def _rwkv_finish(y0, y1, bonus, g, gn_w, gn_b):
    b, tt, _ = y0.shape
    e, et = _head_sum_mats()
    row = pl.BlockSpec((1, ROW_TILE, MIX_W), lambda bb, i: (bb, i, 0))
    full = lambda shape: pl.BlockSpec(shape, lambda bb, i: (0,) * len(shape))
    return pl.pallas_call(
        _rwkv_finish_kernel, grid=(b, tt // ROW_TILE),
        in_specs=[row, row, row, row,
                  full((1, MIX_W)), full((1, MIX_W)), full((MIX_W, LANES)), full((LANES, MIX_W))],
        out_specs=row, out_shape=jax.ShapeDtypeStruct((b, tt, MIX_W), BF16),
        compiler_params=_cparams("parallel", "parallel"), name="rwkv_finish",
    )(y0, y1, bonus, g, gn_w.reshape(1, MIX_W), gn_b.reshape(1, MIX_W), e, et)


def _rope_swap(x):
    lane = lax.broadcasted_iota(jnp.int32, x.shape, 1)
    return jnp.where((lane % 64) < 32, pltpu.roll(x, 96, 1), pltpu.roll(x, 32, 1))


def _rope_tables(t, ctx_len):
    pos = jnp.arange(t, dtype=jnp.int32)
    nf = RET_QK_HD // 4
    inv = ROPE_BASE ** (-jnp.arange(nf, dtype=F32) / nf)
    ang_r = (pos // GRID_W).astype(F32)[:, None] * inv[None, :]
    ang_c = (pos % GRID_W).astype(F32)[:, None] * inv[None, :]
    cos = jnp.concatenate([jnp.cos(ang_r)] * 2 + [jnp.cos(ang_c)] * 2, axis=-1)
    sin = jnp.concatenate([-jnp.sin(ang_r), jnp.sin(ang_r), -jnp.sin(ang_c), jnp.sin(ang_c)], axis=-1)
    cos = jnp.concatenate([jnp.ones((ctx_len, RET_QK_HD), F32), cos], axis=0)
    sin = jnp.concatenate([jnp.zeros((ctx_len, RET_QK_HD), F32), sin], axis=0)
    return cos, sin


def _ret_kernel(dec_ref, q0_ref, q1_ref, k0_ref, k1_ref, v0_ref, v1_ref, cos0_ref, cos1_ref, sin0_ref, sin1_ref,
                o0_ref, o1_ref, r_scr):
    i = pl.program_id(0)
    c = RET_CHUNK
    n_b = q0_ref.shape[0]

    @pl.when(i == 0)
    def _():
        r_scr[...] = jnp.zeros_like(r_scr)

    bf = lambda x: x.astype(BF16)
    ii = lax.broadcasted_iota(jnp.int32, (c, c), 0).astype(F32)
    jj = lax.broadcasted_iota(jnp.int32, (c, c), 1).astype(F32)
    heads = range(RET_HEADS)
    qs = [slice(h * RET_QK_HD, (h + 1) * RET_QK_HD) for h in heads]
    vs = [slice(h * RET_V_HD, (h + 1) * RET_V_HD) for h in heads]
    refs = ((q0_ref, k0_ref, v0_ref, cos0_ref, sin0_ref, o0_ref), (q1_ref, k1_ref, v1_ref, cos1_ref, sin1_ref, o1_ref))
    dmat, q_dec, k_dec, c_dec = {}, {}, {}, {}
    for d in range(2):
        pos = ii if d == 0 else c - 1.0 - ii
        diff = (ii - jj) if d == 0 else (jj - ii)
        for h in heads:
            lg = jnp.log(jax.nn.sigmoid(jnp.full((c, c), dec_ref[d, h], F32)))
            dmat[d, h] = jnp.where(diff >= 0, jnp.exp(lg * jnp.maximum(diff, 0.0)), 0.0)
            q_dec[d, h] = jnp.exp(lg * (pos + 1.0))
            k_dec[d, h] = jnp.exp(lg * (c - 1.0 - pos))
            c_dec[d, h] = jnp.exp(lg[:1, :1] * float(c))
    keys = [(d, bi, h) for d in range(2) for bi in range(n_b) for h in heads]
    q, k, v = {}, {}, {}
    for d, bi, h in keys:
        q_ref, k_ref, v_ref, cos_ref, sin_ref, _ = refs[d]
        cos, sin = cos_ref[...], sin_ref[...]
        qq = q_ref[bi, :, qs[h]]
        kk = k_ref[bi, :, qs[h]]
        q[d, bi, h] = qq * cos + _rope_swap(qq) * sin
        k[d, bi, h] = (kk * cos + _rope_swap(kk) * sin) * (RET_QK_HD ** -0.5)
        v[d, bi, h] = bf(v_ref[bi, :, vs[h]])
    r = {key: r_scr[key] for key in keys}
    s = {(d, bi, h): _dot_nt(bf(q[d, bi, h]), bf(k[d, bi, h])) * dmat[d, h] for d, bi, h in keys}
    for d, bi, h in keys:
        o = jnp.dot(bf(s[d, bi, h]), v[d, bi, h], preferred_element_type=F32)
        o = o + jnp.dot(bf(q[d, bi, h] * q_dec[d, h]), bf(r[d, bi, h]), preferred_element_type=F32)
        refs[d][5][bi, :, vs[h]] = o.astype(o0_ref.dtype)
    for d, bi, h in keys:
        r_scr[d, bi, h] = r[d, bi, h] * c_dec[d, h] + _dot_tn(bf(k[d, bi, h] * k_dec[d, h]), v[d, bi, h])


def _retention(p3, cos, sin, decay, ctx_len):
    b, tt, _ = p3.shape
    c = RET_CHUNK
    n = tt // c
    nc = ctx_len // c
    qb, kb, vb = OFF_RET_Q // RET_QK_W, OFF_RET_K // RET_QK_W, OFF_RET_V // MIX_W

    def per_dir(make):
        return [make(functools.partial(lambda i, dec, d: _scan_chunk(d, i, n, nc), d=d)) for d in range(2)]

    col = lambda width, cb: per_dir(lambda tok: pl.BlockSpec((b, c, width), lambda i, dec: (0, tok(i, dec), cb)))
    tab = per_dir(lambda tok: pl.BlockSpec((c, RET_QK_HD), lambda i, dec: (tok(i, dec), 0)))
    out = jax.ShapeDtypeStruct((b, tt, MIX_W), BF16)
    grid_spec = pltpu.PrefetchScalarGridSpec(
        num_scalar_prefetch=1, grid=(n,),
        in_specs=[*col(RET_QK_W, qb), *col(RET_QK_W, kb), *col(MIX_W, vb), *tab, *tab],
        out_specs=col(MIX_W, 0),
        scratch_shapes=[pltpu.VMEM((2, b, RET_HEADS, RET_QK_HD, RET_V_HD), F32)])
    return pl.pallas_call(
        _ret_kernel, grid_spec=grid_spec, out_shape=[out, out],
        compiler_params=_cparams("arbitrary"), name="retention",
    )(decay, p3, p3, p3, p3, p3, p3, cos, cos, sin, sin)


def _ret_finish_kernel(o0_ref, o1_ref, g_ref, w_ref, b_ref, y_ref):
    for h in range(RET_HEADS):
        sl = slice(h * RET_V_HD, (h + 1) * RET_V_HD)
        o = o0_ref[0, :, sl].astype(F32) + o1_ref[0, :, sl].astype(F32)
        y = _norm(o, RET_GN_EPS) * w_ref[:, sl] + b_ref[:, sl]
        g = g_ref[0, :, sl]
        y_ref[0, :, sl] = (y * (g * jax.nn.sigmoid(g))).astype(y_ref.dtype)


def _ret_finish(o0, o1, p3, gn_w, gn_b):
    b, tt, _ = o0.shape
    gb = OFF_RET_G // MIX_W
    row = pl.BlockSpec((1, ROW_TILE, MIX_W), lambda bb, i: (bb, i, 0))
    vec = pl.BlockSpec((1, MIX_W), lambda bb, i: (0, 0))
    return pl.pallas_call(
        _ret_finish_kernel, grid=(b, tt // ROW_TILE),
        in_specs=[row, row, pl.BlockSpec((1, ROW_TILE, MIX_W), lambda bb, i: (bb, i, gb)), vec, vec],
        out_specs=row, out_shape=jax.ShapeDtypeStruct((b, tt, MIX_W), BF16),
        compiler_params=_cparams("parallel", "parallel"), name="ret_finish",
    )(o0, o1, p3, gn_w.reshape(1, MIX_W), gn_b.reshape(1, MIX_W))


NA_BLOCK_ROWS = 4
NA_BLOCK_TOK = NA_BLOCK_ROWS * GRID_W
NA_WIN_TOK = NA_WIN_ROWS * GRID_W


def _softmax_pv(s_parts, v_parts):
    m = None
    for s in s_parts:
        mm = jnp.max(s, axis=1, keepdims=True)
        m = mm if m is None else jnp.maximum(m, mm)
    den, acc = None, None
    for s, v in zip(s_parts, v_parts):
        e = jnp.exp(s - m)
        dd = jnp.sum(e, axis=1, keepdims=True)
        pv = jnp.dot(e.astype(BF16), v, preferred_element_type=F32)
        den = dd if den is None else den + dd
        acc = pv if acc is None else acc + pv
    return acc / den


def _na_kernel(q_ref, kp_ref, kc_ref, kn_ref, vp_ref, vc_ref, vn_ref, kx_ref, vx_ref, bias_ref, o_ref,
               k_scr, v_scr, *, n_blocks):
    rb = pl.program_id(1)
    n_b = q_ref.shape[0]
    batch = range(n_b)
    lane = lax.broadcasted_iota(jnp.int32, (1, LANES), 1)
    head_masks = [(lane < NA_HD).astype(F32), (lane >= NA_HD).astype(F32)]
    heads = range(2)
    kx = [kx_ref[bi].astype(BF16) for bi in batch]
    vx = [vx_ref[bi].astype(BF16) for bi in batch]
    scale = NA_HD ** -0.5

    @pl.when(rb == 0)
    def _():
        for bi in batch:
            q = q_ref[bi] * scale
            out = jnp.zeros((NA_BLOCK_TOK, LANES), F32)
            for hm in head_masks:
                s = _dot_nt((q * hm).astype(BF16), kx[bi])
                out = out + _softmax_pv([s], [vx[bi]]) * hm
            o_ref[bi] = out.astype(o_ref.dtype)

    @pl.when(rb > 0)
    def _():
        t = NA_BLOCK_TOK
        for bi in batch:
            k_scr[bi, 0:t, :] = kp_ref[bi].astype(BF16)
            k_scr[bi, t:2 * t, :] = kc_ref[bi].astype(BF16)
            k_scr[bi, 2 * t:3 * t, :] = kn_ref[bi].astype(BF16)
            v_scr[bi, 0:t, :] = vp_ref[bi].astype(BF16)
            v_scr[bi, t:2 * t, :] = vc_ref[bi].astype(BF16)
            v_scr[bi, 2 * t:3 * t, :] = vn_ref[bi].astype(BF16)
        first = rb == 1
        last = rb == n_blocks
        rows = range(NA_BLOCK_ROWS)
        rs = [slice(j * GRID_W, (j + 1) * GRID_W) for j in rows]
        starts, oi = [], []
        for j in rows:
            off = jnp.where(last, 0, jnp.where(first, NA_BLOCK_ROWS, j))
            oi.append(jnp.where(last, NA_BLOCK_ROWS + j, jnp.where(first, j, NA_WIN_ROWS // 2)))
            starts.append(pl.multiple_of(off * GRID_W, GRID_W))
        kw = [[k_scr[bi, pl.ds(starts[j], NA_WIN_TOK), :] for j in rows] for bi in batch]
        vw = [[v_scr[bi, pl.ds(starts[j], NA_WIN_TOK), :] for j in rows] for bi in batch]
        bias = [[bias_ref[oi[j], h] for h in heads] for j in rows]
        qh = [[(q_ref[bi] * scale * hm).astype(BF16) for hm in head_masks] for bi in batch]
        bh = [(bi, h) for bi in batch for h in heads]
        bjh = [(bi, j, h) for bi in batch for j in rows for h in heads]
        s_ctx = {(bi, h): _dot_nt(qh[bi][h], kx[bi]) for bi, h in bh}
        s_loc = {(bi, j, h): _dot_nt(qh[bi][h][rs[j]], kw[bi][j]) + bias[j][h] for bi, j, h in bjh}
        m_ctx = {k: jnp.max(s_ctx[k], axis=1, keepdims=True) for k in bh}
        m = {(bi, j, h): jnp.maximum(jnp.max(s_loc[bi, j, h], axis=1, keepdims=True), m_ctx[bi, h][rs[j]])
             for bi, j, h in bjh}
        m_all = {(bi, h): jnp.concatenate([m[bi, j, h] for j in rows], axis=0) for bi, h in bh}
        e_ctx = {k: jnp.exp(s_ctx[k] - m_all[k]) for k in bh}
        e_loc = {k: jnp.exp(s_loc[k] - m[k]) for k in bjh}
        d_ctx = {k: jnp.sum(e_ctx[k], axis=1, keepdims=True) for k in bh}
        pv_ctx = {(bi, h): jnp.dot(e_ctx[bi, h].astype(BF16), vx[bi], preferred_element_type=F32) for bi, h in bh}
        for bi in batch:
            for j in rows:
                out = None
                for h in heads:
                    den = jnp.sum(e_loc[bi, j, h], axis=1, keepdims=True) + d_ctx[bi, h][rs[j]]
                    pv = jnp.dot(e_loc[bi, j, h].astype(BF16), vw[bi][j], preferred_element_type=F32) + pv_ctx[bi, h][rs[j]]
                    term = pv / den * head_masks[h]
                    out = term if out is None else out + term
                o_ref[bi, rs[j], :] = out.astype(o_ref.dtype)


def _na_bias_table(rpb):
    col = jnp.arange(GRID_W)
    cs = jnp.clip(col - NA_WIN_COLS // 2, 0, GRID_W - NA_WIN_COLS)
    col_ok = (col[None, :] >= cs[:, None]) & (col[None, :] < cs[:, None] + NA_WIN_COLS)
    col_idx = jnp.clip(col[None, :] - col[:, None] + (NA_WIN_COLS - 1), 0, 2 * NA_WIN_COLS - 2)
    win = jnp.arange(NA_WIN_ROWS)
    row_idx = win[None, :] - win[:, None] + (NA_WIN_ROWS - 1)
    row_hot = (row_idx[:, :, None] == jnp.arange(2 * NA_WIN_ROWS - 1)).astype(F32)
    col_hot = (col_idx[:, :, None] == jnp.arange(2 * NA_WIN_COLS - 1)).astype(F32)
    bias = jnp.einsum("ora,hab,qcb->ohqrc", row_hot, rpb.astype(F32), col_hot, precision=HI)
    bias = jnp.where(col_ok[None, None, :, None, :], bias, -1e30)
    return bias.reshape(NA_WIN_ROWS, NA_HEADS, GRID_W, NA_WIN_TOK)


def _na(p3, bias_tab, ctx_len):
    b, tt, _ = p3.shape
    assert ctx_len == NA_BLOCK_TOK
    t = tt - ctx_len
    n_rows = t // GRID_W
    assert n_rows % NA_BLOCK_ROWS == 0 and n_rows >= NA_WIN_ROWS
    nb = n_rows // NA_BLOCK_ROWS
    qb, kb, vb = OFF_NA_Q // LANES, OFF_NA_K // LANES, OFF_NA_V // LANES
    blk = (b, NA_BLOCK_TOK, LANES)

    def spec(colb, shift):
        return pl.BlockSpec(blk, lambda pr, rb: (0, jnp.clip(rb + shift, 1, nb), colb + pr))

    return pl.pallas_call(
        functools.partial(_na_kernel, n_blocks=nb), grid=(NA_HEADS // 2, nb + 1),
        in_specs=[pl.BlockSpec(blk, lambda pr, rb: (0, rb, qb + pr)),
                  spec(kb, -1), spec(kb, 0), spec(kb, 1), spec(vb, -1), spec(vb, 0), spec(vb, 1),
                  pl.BlockSpec(blk, lambda pr, rb: (0, 0, kb + pr)),
                  pl.BlockSpec(blk, lambda pr, rb: (0, 0, vb + pr)),
                  pl.BlockSpec((NA_WIN_ROWS, 2, GRID_W, NA_WIN_TOK), lambda pr, rb: (0, pr, 0, 0))],
        out_specs=pl.BlockSpec(blk, lambda pr, rb: (0, rb, pr)),
        out_shape=jax.ShapeDtypeStruct((b, tt, MIX_W), BF16),
        scratch_shapes=[pltpu.VMEM((b, 3 * NA_BLOCK_TOK, LANES), BF16), pltpu.VMEM((b, 3 * NA_BLOCK_TOK, LANES), BF16)],
        compiler_params=_cparams("parallel", "parallel"), name="na",
    )(p3, p3, p3, p3, p3, p3, p3, p3, p3, bias_tab)


def _build_mod(gate, shift, scale, b):
    rows = jnp.stack([gate, shift, scale], axis=1)
    lat = rows[:b]
    ctx = jnp.broadcast_to(rows[b][None], lat.shape)
    mod = jnp.stack([ctx, lat], axis=1)
    return jnp.pad(mod, ((0, 0), (0, 0), (0, 5), (0, 0)))


def kernel(x, c, ctx, c_ctx, ada_w, ada_b, w_in, rwkv_shift, rwkv_w0, rwkv_wB, rwkv_a0, rwkv_aB, rwkv_gB, rwkv_kk, rwkv_ka, rwkv_rk, rwkv_gn_w, rwkv_gn_b, ret_decay, ret_gn_w, ret_gn_b, na_rpb, w_branch, w_out, ln1_w, ln1_b, ln2_w, ln2_b, router_w, router_b, exp_w1, exp_b1, exp_w2, exp_b2):
    b, t, d = x.shape
    ctx_len = ctx.shape[1]
    tt = ctx_len + t
    n = b * tt
    tpb = tt // ROW_TILE
    depth = ada_w.shape[0]
    alpha = (2 * depth) ** 0.25
    assert d == D_MODEL and b + 1 <= 8 and ctx_len == ROW_TILE and t % ROW_TILE == 0

    xa = jnp.concatenate([ctx, x], axis=1).reshape(n, d)
    cond = jnp.concatenate([c, c_ctx[None], jnp.zeros((8 - b - 1, d), F32)], axis=0)
    ada = _adaln(cond.T, ada_w, ada_b, b + 1).reshape(depth, 8, 6, d)
    cos, sin = _rope_tables(t, ctx_len)
    n_exp = router_w.shape[-1]
    rw_pad = jnp.pad(router_w, ((0, 0), (0, 0), (0, LANES - n_exp)))
    rw_trunc = lax.bitcast_convert_type(lax.bitcast_convert_type(rw_pad, jnp.uint32) & jnp.uint32(0xFFFF0000), F32)
    rw_split = jnp.stack([rw_trunc.astype(BF16), (rw_pad - rw_trunc).astype(BF16)], axis=1)
    rb_pad = jnp.pad(router_b, ((0, 0), (0, LANES - n_exp)), constant_values=-1e30)
    zero = jnp.zeros((8, d), F32)

    h = _modulate(xa, _build_mod(zero, ada[0, :, 0], ada[0, :, 1], b), tpb)
    for l in range(depth):
        p = _in_proj(h, w_in, l)
        p3 = p.reshape(b, tt, D_IN_PROJ)

        r, v, kk, g, bonus, lw, bb, kd = _rwkv_prep(p3, rwkv_shift[l], rwkv_w0[l], rwkv_wB[l], rwkv_a0[l], rwkv_aB[l],
                                                    rwkv_gB[l], rwkv_kk[l], rwkv_ka[l], rwkv_rk[l], ctx_len)
        y_fwd, y_bwd = _rwkv_scan(r, v, kk, lw, bb, kd, ctx_len)
        ya = _rwkv_finish(y_fwd, y_bwd, bonus, g, rwkv_gn_w[l], rwkv_gn_b[l])

        o_fwd, o_bwd = _retention(p3, cos, sin, ret_decay[l], ctx_len)
        yb = _ret_finish(o_fwd, o_bwd, p3, ret_gn_w[l], ret_gn_b[l])

        yn = _na(p3, _na_bias_table(na_rpb[l]), ctx_len)

        merged = _merge(ya.reshape(n, MIX_W), yb.reshape(n, MIX_W), yn.reshape(n, MIX_W), p, w_branch[l].astype(BF16))
        y = _matmul(merged, w_out[l].astype(BF16), 512)

        mod_f = _build_mod(ada[l, :, 2], ada[l, :, 3], ada[l, :, 4], b)
        x1, h2, route = _resid_ln(xa, y, mod_f, ln1_w[l], ln1_b[l], tpb, alpha, rw_split[l], rb_pad[l][None])
        tok_of_pos, pos_of_slot, tile_expert, n_used = _moe_dispatch(route, n_exp)
        xs = jnp.take(h2, tok_of_pos, axis=0, mode="clip")
        ys = _moe(xs, tile_expert, n_used, exp_w1, exp_b1[l], exp_w2, exp_b2[l], l)
        f4 = jnp.take(ys, pos_of_slot.T.reshape(-1), axis=0, mode="clip").reshape(TOP_K, n, d)
        nxt = min(l + 1, depth - 1)
        mod_a = _build_mod(ada[l, :, 5], ada[nxt, :, 0], ada[nxt, :, 1], b)
        xa, h = _resid_ln(x1, f4, mod_a, ln2_w[l], ln2_b[l], tpb, alpha, route=route)
    return xa.reshape(b, tt, d)[:, ctx_len:]
```

```python
import functools
import math

import jax
import jax.numpy as jnp
from jax import lax
from jax.experimental import pallas as pl
from jax.experimental.pallas import tpu as pltpu

F32 = jnp.float32
BF16 = jnp.bfloat16
HI = lax.Precision.HIGHEST

D_MODEL = 2048
GRID_W = 64
MIX_W = D_MODEL // 2
N_BRANCH = 3

RWKV_HD = 64
RWKV_HEADS = MIX_W // RWKV_HD
RWKV_DECAY_RANK = 64
RWKV_AAA_RANK = 64
RWKV_GATE_RANK = 128
RWKV_SHIFT_W = 3 * MIX_W + RWKV_DECAY_RANK + RWKV_AAA_RANK + RWKV_GATE_RANK
RWKV_GN_EPS = 64e-5
RWKV_CHUNK = 64

RET_HEADS = 4
RET_V_HD = MIX_W // RET_HEADS
RET_QK_HD = RET_V_HD // 2
RET_QK_W = RET_HEADS * RET_QK_HD
RET_CHUNK = 128
RET_GN_EPS = 1e-6

NA_HD = 64
NA_HEADS = MIX_W // NA_HD
NA_WIN_ROWS = 8
NA_WIN_COLS = 16

N_EXPERTS = 32
TOP_K = 4
D_EXPERT = D_MODEL // 4
SWIGLU_LIMIT = 7.0
SWIGLU_ALPHA = 1.702

ROPE_BASE = 10000.0
LN_EPS = 1e-6

IN_PROJ_TN = 256
D_IN_PROJ = RWKV_SHIFT_W + 2 * RET_QK_W + 5 * MIX_W + N_BRANCH * D_MODEL
IN_PROJ_ROT = RWKV_SHIFT_W // IN_PROJ_TN
OFF_RET_Q = 0
OFF_RET_K = OFF_RET_Q + RET_QK_W
OFF_RET_V = OFF_RET_K + RET_QK_W
OFF_RET_G = OFF_RET_V + MIX_W
OFF_NA_Q = OFF_RET_G + MIX_W
OFF_NA_K = OFF_NA_Q + MIX_W
OFF_NA_V = OFF_NA_K + MIX_W
OFF_GATE = OFF_NA_V + MIX_W
OFF_RWKV = OFF_GATE + N_BRANCH * D_MODEL
OFF_RWKV_LORA = OFF_RWKV + 3 * MIX_W
RWKV_LORA_W = RWKV_SHIFT_W - 3 * MIX_W

ROW_TILE = 256
LANES = 128
VMEM_LIMIT = 56 * 1024 * 1024


def _cparams(*sem):
    return pltpu.CompilerParams(dimension_semantics=sem, vmem_limit_bytes=VMEM_LIMIT)


def _adaln_kernel(ct_ref, w_ref, b_ref, o_ref, *, n_cond):
    ct = ct_ref[...]
    st = ct * jax.nn.sigmoid(ct)
    w = w_ref[0]
    o_ref[0] = jnp.zeros(o_ref.shape[1:], F32)
    for m in range(n_cond):
        o_ref[0, m:m + 1, :] = jnp.sum(st[:, m:m + 1] * w, axis=0, keepdims=True) + b_ref[0]


def _adaln(cond_t, ada_w, ada_b, n_cond):
    n_layers, d, d6 = ada_w.shape
    tn = 512
    return pl.pallas_call(
        functools.partial(_adaln_kernel, n_cond=n_cond),
        grid=(n_layers, d6 // tn),
        in_specs=[pl.BlockSpec((d, 8), lambda l, j: (0, 0)),
                  pl.BlockSpec((1, d, tn), lambda l, j: (l, 0, j)),
                  pl.BlockSpec((1, 1, tn), lambda l, j: (l, 0, j))],
        out_specs=pl.BlockSpec((1, 8, tn), lambda l, j: (l, 0, j)),
        out_shape=jax.ShapeDtypeStruct((n_layers, 8, d6), F32),
        compiler_params=_cparams("parallel", "parallel"),
        name="adaln",
    )(cond_t, ada_w, ada_b.reshape(n_layers, 1, d6))


def _norm(x, eps):
    mu = jnp.mean(x, axis=-1, keepdims=True)
    xc = x - mu
    var = jnp.mean(xc * xc, axis=-1, keepdims=True)
    return xc * lax.rsqrt(var + eps)


def _top4_route(logits):
    lane = lax.broadcasted_iota(jnp.int32, logits.shape, 1).astype(F32)
    l = logits
    idxs, vals = [], []
    for k in range(TOP_K):
        m = jnp.max(l, axis=1, keepdims=True)
        idx = jnp.min(jnp.where(l == m, lane, float(LANES)), axis=1, keepdims=True)
        l = jnp.where(lane == idx, -jnp.inf, l)
        idxs.append(idx)
        vals.append(m)
    es = [jnp.exp(v - vals[0]) for v in vals]
    den = es[0] + es[1] + es[2] + es[3]
    out = jnp.zeros(logits.shape, F32)
    for k in range(TOP_K):
        out = jnp.where(lane == float(k), idxs[k], out)
        out = jnp.where(lane == float(TOP_K + k), es[k] / den, out)
    return out


def _mod_kernel(x_ref, mod_ref, h_ref):
    md = mod_ref[0, 0]
    h = _norm(x_ref[...], LN_EPS) * (1.0 + md[2:3, :]) + md[1:2, :]
    h_ref[...] = h.astype(BF16)


def _resid_kernel(*refs, alpha, router, combine):
    refs = list(refs)
    x_ref, y_ref, mod_ref, lnw_ref, lnb_ref = refs[:5]
    rest = refs[5:]
    if combine:
        route_ref, rest = rest[0], rest[1:]
    if router:
        rw_ref, rb_ref, rest = rest[0], rest[1], rest[2:]
    xo_ref, h_ref = rest[:2]
    md = mod_ref[0, 0]
    if combine:
        route = route_ref[...]
        y = None
        for k in range(TOP_K):
            t = route[:, TOP_K + k:TOP_K + k + 1] * y_ref[k].astype(F32)
            y = t if y is None else y + t
    else:
        y = y_ref[...]
    z = alpha * x_ref[...] + md[0:1, :] * y
    x1 = _norm(z, LN_EPS) * lnw_ref[...] + lnb_ref[...]
    xo_ref[...] = x1
    h = _norm(x1, LN_EPS) * (1.0 + md[2:3, :]) + md[1:2, :]
    h_ref[...] = h.astype(BF16)
    if router:
        h_hi = h.astype(BF16)
        h_lo = (h - h_hi.astype(F32)).astype(BF16)
        dotf = functools.partial(jnp.dot, preferred_element_type=F32)
        logits = dotf(h_hi, rw_ref[0]) + dotf(h_lo, rw_ref[0]) + dotf(h_hi, rw_ref[1]) + rb_ref[...]
        rest[2][...] = _top4_route(logits)


def _row_specs(n_rows, tiles_per_batch, d):
    row = pl.BlockSpec((ROW_TILE, d), lambda i: (i, 0))
    mod = pl.BlockSpec((1, 1, 8, d), lambda i: (i // tiles_per_batch, jnp.minimum(i % tiles_per_batch, 1), 0, 0))
    vec = pl.BlockSpec((1, d), lambda i: (0, 0))
    return row, mod, vec


def _modulate(x, mod, tiles_per_batch):
    n, d = x.shape
    row, modspec, _ = _row_specs(n, tiles_per_batch, d)
    return pl.pallas_call(
        _mod_kernel, grid=(n // ROW_TILE,),
        in_specs=[row, modspec], out_specs=row,
        out_shape=jax.ShapeDtypeStruct((n, d), BF16),
        compiler_params=_cparams("parallel"), name="modulate",
    )(x, mod)


def _resid_ln(x, y, mod, lnw, lnb, tiles_per_batch, alpha, router_w=None, router_b=None, route=None):
    n, d = x.shape
    row, modspec, vec = _row_specs(n, tiles_per_batch, d)
    lanes = pl.BlockSpec((ROW_TILE, LANES), lambda i: (i, 0))
    router = router_w is not None
    combine = route is not None
    y_spec = pl.BlockSpec((TOP_K, ROW_TILE, d), lambda i: (0, i, 0)) if combine else row
    in_specs = [row, y_spec, modspec, vec, vec]
    args = [x, y, mod, lnw.reshape(1, d), lnb.reshape(1, d)]
    out_specs = [row, row]
    out_shape = [jax.ShapeDtypeStruct((n, d), F32), jax.ShapeDtypeStruct((n, d), BF16)]
    if combine:
        in_specs.append(lanes)
        args.append(route)
    if router:
        in_specs += [pl.BlockSpec((2, d, LANES), lambda i: (0, 0, 0)), pl.BlockSpec((1, LANES), lambda i: (0, 0))]
        args += [router_w, router_b]
        out_specs.append(lanes)
        out_shape.append(jax.ShapeDtypeStruct((n, LANES), F32))
    return pl.pallas_call(
        functools.partial(_resid_kernel, alpha=alpha, router=router, combine=combine),
        grid=(n // ROW_TILE,), in_specs=in_specs, out_specs=out_specs, out_shape=out_shape,
        compiler_params=_cparams("parallel"), name="resid_ln",
    )(*args)


def _matmul_kernel(x_ref, w_ref, o_ref):
    o_ref[...] = jnp.dot(x_ref[...], w_ref[...], preferred_element_type=F32).astype(o_ref.dtype)


def _pick_tm(n, cap):
    tm = cap
    while n % tm:
        tm -= 16
    return tm


def _matmul(x, w, tn, out_dtype=F32, tm_cap=1536):
    n, k = x.shape
    _, m = w.shape
    tm = _pick_tm(n, tm_cap)
    return pl.pallas_call(
        _matmul_kernel, grid=(n // tm, m // tn),
        in_specs=[pl.BlockSpec((tm, k), lambda i, j: (i, 0)), pl.BlockSpec((k, tn), lambda i, j: (0, j))],
        out_specs=pl.BlockSpec((tm, tn), lambda i, j: (i, j)),
        out_shape=jax.ShapeDtypeStruct((n, m), out_dtype),
        compiler_params=_cparams("parallel", "parallel"), name="matmul",
    )(x, w)


def _in_proj_kernel(x_ref, w_ref, o_ref):
    o_ref[...] = jnp.dot(x_ref[...], w_ref[...].astype(BF16), preferred_element_type=F32)


def _in_proj(x, w, layer):
    n, k = x.shape
    tm = _pick_tm(n, 2816)
    nblk = D_IN_PROJ // IN_PROJ_TN
    return pl.pallas_call(
        _in_proj_kernel, grid=(n // tm, nblk),
        in_specs=[pl.BlockSpec((tm, k), lambda i, j: (i, 0)),
                  pl.BlockSpec((None, k, IN_PROJ_TN), lambda i, j: (layer, 0, (j + IN_PROJ_ROT) % nblk))],
        out_specs=pl.BlockSpec((tm, IN_PROJ_TN), lambda i, j: (i, j)),
        out_shape=jax.ShapeDtypeStruct((n, D_IN_PROJ), F32),
        compiler_params=_cparams("parallel", "parallel"), name="in_proj",
    )(x, w)


def _merge_kernel(ya_ref, yb_ref, yn_ref, g0_ref, g1_ref, g2_ref, wb_ref, o_ref):
    acc = None
    for i, (y_ref, g_ref) in enumerate(((ya_ref, g0_ref), (yb_ref, g1_ref), (yn_ref, g2_ref))):
        z = jnp.dot(y_ref[...], wb_ref[i], preferred_element_type=F32)
        t = jax.nn.sigmoid(g_ref[...]) * z
        acc = t if acc is None else acc + t
    o_ref[...] = acc.astype(o_ref.dtype)


def _merge(ya, yb, yn, p, w_branch):
    n, mw = ya.shape
    d = w_branch.shape[-1]
    tn = 512
    tm = _pick_tm(n, 768)
    gate_blk = OFF_GATE // tn
    y_spec = pl.BlockSpec((tm, mw), lambda i, j: (i, 0))
    g_specs = [pl.BlockSpec((tm, tn), functools.partial(lambda i, j, b: (i, gate_blk + b * (d // tn) + j), b=b))
               for b in range(N_BRANCH)]
    return pl.pallas_call(
        _merge_kernel, grid=(n // tm, d // tn),
        in_specs=[y_spec, y_spec, y_spec, *g_specs, pl.BlockSpec((N_BRANCH, mw, tn), lambda i, j: (0, 0, j))],
        out_specs=pl.BlockSpec((tm, tn), lambda i, j: (i, j)),
        out_shape=jax.ShapeDtypeStruct((n, d), BF16),
        compiler_params=_cparams("parallel", "parallel"), name="merge",
    )(ya, yb, yn, p, p, p, w_branch)


MOE_TM = 512


def _moe_dispatch(route, n_exp):
    n = route.shape[0]
    n_slots = n * TOP_K
    n_tiles = (n_slots + n_exp * (MOE_TM - 1)) // MOE_TM + 1
    i32 = jnp.int32
    experts = jnp.arange(n_exp, dtype=i32)
    flat = route[:, :TOP_K].astype(i32).T.reshape(n_slots)
    order = jnp.argsort(flat, stable=True).astype(i32)
    rank = jnp.argsort(order).astype(i32)
    onehot = (flat[:, None] == experts[None, :]).astype(i32)
    counts = jnp.sum(onehot, axis=0)
    padded = ((counts + MOE_TM - 1) // MOE_TM) * MOE_TM
    pad_end = jnp.cumsum(padded)
    pad_start = pad_end - padded
    start = jnp.cumsum(counts) - counts
    pos_of_slot = jnp.sum(onehot * (pad_start - start)[None, :], axis=1) + rank
    n_used = (pad_end[-1] // MOE_TM).astype(i32)
    tile_start = jnp.minimum(jnp.arange(n_tiles, dtype=i32), n_used - 1) * MOE_TM
    tile_expert = jnp.minimum(jnp.sum((tile_start[:, None] >= pad_end[None, :]).astype(i32), axis=1), n_exp - 1)
    tile_hot = (tile_expert[:, None] == experts[None, :]).astype(i32)
    row = jnp.arange(n_tiles, dtype=i32)[:, None] * MOE_TM + jnp.arange(MOE_TM, dtype=i32)[None, :]
    off = row - jnp.sum(tile_hot * pad_start[None, :], axis=1)[:, None]
    valid = (off < jnp.sum(tile_hot * counts[None, :], axis=1)[:, None]) & (row < n_used * MOE_TM)
    src = jnp.clip(jnp.sum(tile_hot * start[None, :], axis=1)[:, None] + off, 0, n_slots - 1)
    tok_of_pos = jnp.where(valid, jnp.take(order % n, src, mode="clip"), row % n).reshape(-1)
    return tok_of_pos, pos_of_slot, tile_expert, n_used.reshape(1)


def _moe_kernel(te_ref, nu_ref, x_ref, w1_ref, b1_ref, w2_ref, b2_ref, o_ref, w1_scr, w2_scr):
    t = pl.program_id(0)
    live = t < nu_ref[0]
    new_expert = (t == 0) | (te_ref[t] != te_ref[jnp.maximum(t - 1, 0)])

    @pl.when(live & new_expert)
    def _():
        w1_scr[...] = w1_ref[...].astype(BF16)
        w2_scr[...] = w2_ref[...].astype(BF16)

    @pl.when(live)
    def _():
        hu = jnp.dot(x_ref[...], w1_scr[...], preferred_element_type=F32) + b1_ref[0]
        f = hu.shape[1] // 2
        g = jnp.minimum(hu[:, :f], SWIGLU_LIMIT)
        u = jnp.clip(hu[:, f:], -SWIGLU_LIMIT, SWIGLU_LIMIT)
        act = g * jax.nn.sigmoid(SWIGLU_ALPHA * g) * (u + 1.0)
        y = jnp.dot(act.astype(BF16), w2_scr[...], preferred_element_type=F32) + b2_ref[0]
        o_ref[...] = y.astype(o_ref.dtype)

    @pl.when(jnp.logical_not(live))
    def _():
        o_ref[...] = jnp.zeros_like(o_ref)


def _moe(xs, tile_expert, n_used, w1, b1, w2, b2, layer):
    p, d = xs.shape
    _, n_exp, _, f2 = w1.shape
    grid_spec = pltpu.PrefetchScalarGridSpec(
        num_scalar_prefetch=2, grid=(p // MOE_TM,),
        in_specs=[pl.BlockSpec((MOE_TM, d), lambda t, te, nu: (t, 0)),
                  pl.BlockSpec((None, None, d, f2), lambda t, te, nu: (layer, te[t], 0, 0)),
                  pl.BlockSpec((1, 1, f2), lambda t, te, nu: (te[t], 0, 0)),
                  pl.BlockSpec((None, None, f2 // 2, d), lambda t, te, nu: (layer, te[t], 0, 0)),
                  pl.BlockSpec((1, 1, d), lambda t, te, nu: (te[t], 0, 0))],
        out_specs=pl.BlockSpec((MOE_TM, d), lambda t, te, nu: (t, 0)),
        scratch_shapes=[pltpu.VMEM((d, f2), BF16), pltpu.VMEM((f2 // 2, d), BF16)])
    return pl.pallas_call(
        _moe_kernel, grid_spec=grid_spec,
        out_shape=jax.ShapeDtypeStruct((p, d), BF16),
        compiler_params=_cparams("arbitrary"), name="moe",
    )(tile_expert, n_used, xs, w1, b1.reshape(n_exp, 1, f2), w2, b2.reshape(n_exp, 1, d))


def _dot_nt(a, b, precision=None):
    return lax.dot_general(a, b, (((1,), (1,)), ((), ())), preferred_element_type=F32, precision=precision)


def _dot_tn(a, b, precision=None):
    return lax.dot_general(a, b, (((0,), (0,)), ((), ())), preferred_element_type=F32, precision=precision)


def _scan_chunk(d, i, n_chunks, n_ctx_chunks):
    rev = jnp.where(i < n_ctx_chunks, n_ctx_chunks - 1 - i, n_chunks - 1 + n_ctx_chunks - i)
    return jnp.where(d == 0, i, rev)


def _head_sum_mats():
    ch = jnp.arange(MIX_W) // RWKV_HD
    e = (ch[:, None] == jnp.arange(LANES)[None, :]).astype(BF16)
    return e, e.T


def _dot_split(x, m_bf16):
    hi = x.astype(BF16)
    lo = (x - hi.astype(F32)).astype(BF16)
    return (jnp.dot(hi, m_bf16, preferred_element_type=F32) + jnp.dot(lo, m_bf16, preferred_element_type=F32))


def _rwkv_prep_kernel(*refs, tiles_per_batch):
    pieces, rest = [refs[3 * j:3 * j + 3] for j in range(4)], refs[12:]
    (shift_ref, w0_ref, wb_ref, a0_ref, ab_ref, gb_ref, kk_ref, ka_ref, rk_ref, e_ref, et_ref,
     r_out, v_out, kk_out, g_out, bonus_out, lw_out, b_out, kd_out) = rest
    i = pl.program_id(1)
    rows = lax.broadcasted_iota(jnp.int32, (ROW_TILE, 1), 0)
    has_prev = jnp.where(i > 1, 1.0, 0.0)
    has_next = jnp.where((i > 0) & (i < tiles_per_batch - 1), 1.0, 0.0)

    def token_shift(piece, col0):
        p_ref, pp_ref, pn_ref = piece
        x = p_ref[0]
        w = x.shape[1]
        xp = jnp.where(rows == 0, pp_ref[0, 7:8, :] * has_prev, pltpu.roll(x, 1, 0))
        xn = jnp.where(rows == ROW_TILE - 1, pn_ref[0, 0:1, :] * has_next, pltpu.roll(x, ROW_TILE - 1, 0))
        sh = shift_ref[:, col0:col0 + w]
        return xp * sh[0:1, :] + x * sh[1:2, :] + xn * sh[2:3, :]

    m = MIX_W
    r, k, v = (token_shift(pieces[j], j * m) for j in range(3))
    lora = token_shift(pieces[3], 3 * m)
    wl = lora[:, :RWKV_DECAY_RANK]
    al = lora[:, RWKV_DECAY_RANK:RWKV_DECAY_RANK + RWKV_AAA_RANK]
    gl = lora[:, RWKV_DECAY_RANK + RWKV_AAA_RANK:]
    dot = lambda x, w: jnp.dot(x.astype(BF16), w.astype(BF16), preferred_element_type=F32)
    e, et = e_ref[...], et_ref[...]
    g_out[0] = dot(jax.nn.sigmoid(gl), gb_ref[...]).astype(g_out.dtype)
    kk0 = k * kk_ref[...]
    nrm = jnp.maximum(jnp.sqrt(_dot_split(kk0 * kk0, e)), 1e-12)
    kk = kk0 * _dot_split(1.0 / nrm, et)
    tw = jnp.tanh(wl)
    bsum = None
    for d in range(2):
        a = jax.nn.sigmoid(a0_ref[d:d + 1, :] + dot(al, ab_ref[d]))
        kd = k * (1.0 + (a - 1.0) * ka_ref[...])
        lw = (-math.exp(-0.5)) * jax.nn.sigmoid(w0_ref[d:d + 1, :] + dot(tw, wb_ref[d]))
        lw_out[d, 0] = lw
        b_out[d, 0] = (kk * a).astype(b_out.dtype)
        kd_out[d, 0] = kd.astype(kd_out.dtype)
        t = _dot_split(r * kd * rk_ref[...], e)
        bsum = t if bsum is None else bsum + t
    r_out[0] = r.astype(r_out.dtype)
    v_out[0] = v.astype(v_out.dtype)
    kk_out[0] = kk.astype(kk_out.dtype)
    bonus_out[0] = (_dot_split(bsum, et) * v).astype(bonus_out.dtype)


def _rwkv_prep(p3, shift, w0, wb, a0, ab, gb, k_k, k_a, r_k, ctx_len):
    b, tt, _ = p3.shape
    tpb = tt // ROW_TILE
    assert ctx_len == ROW_TILE
    e, et = _head_sum_mats()
    hb = ROW_TILE // 8
    full = lambda shape: pl.BlockSpec(shape, lambda bb, i: (0,) * len(shape))
    row1 = pl.BlockSpec((1, ROW_TILE, MIX_W), lambda bb, i: (bb, i, 0))
    row2 = pl.BlockSpec((2, 1, ROW_TILE, MIX_W), lambda bb, i: (0, bb, i, 0))
    s2 = jax.ShapeDtypeStruct((2, b, tt, MIX_W), F32)
    h1 = jax.ShapeDtypeStruct((b, tt, MIX_W), BF16)
    h2 = jax.ShapeDtypeStruct((2, b, tt, MIX_W), BF16)
    vec = lambda a: a.reshape(1, MIX_W)

    def piece(width, col_block):
        return [pl.BlockSpec((1, ROW_TILE, width), lambda bb, i: (bb, i, col_block)),
                pl.BlockSpec((1, 8, width), lambda bb, i: (bb, jnp.maximum(i * hb - 1, 0), col_block)),
                pl.BlockSpec((1, 8, width), lambda bb, i: (bb, jnp.minimum((i + 1) * hb, tpb * hb - 1), col_block))]

    pieces = [s for j in range(3) for s in piece(MIX_W, OFF_RWKV // MIX_W + j)] + piece(RWKV_LORA_W, OFF_RWKV_LORA // RWKV_LORA_W)
    return pl.pallas_call(
        functools.partial(_rwkv_prep_kernel, tiles_per_batch=tpb), grid=(b, tpb),
        in_specs=pieces + [full((3, RWKV_SHIFT_W)), full((2, MIX_W)), full((2, RWKV_DECAY_RANK, MIX_W)), full((2, MIX_W)),
                           full((2, RWKV_AAA_RANK, MIX_W)), full((RWKV_GATE_RANK, MIX_W)),
                           full((1, MIX_W)), full((1, MIX_W)), full((1, MIX_W)), full((MIX_W, LANES)), full((LANES, MIX_W))],
        out_specs=[row1, row1, row1, row1, row1, row2, row2, row2],
        out_shape=[h1, h1, h1, h1, h1, s2, h2, h2],
        compiler_params=_cparams("parallel", "parallel"), name="rwkv_prep",
    )(*([p3] * 12), shift, w0, wb, a0, ab, gb, vec(k_k), vec(k_a), vec(r_k), e, et)


def _rwkv_scan_kernel(r0_ref, r1_ref, v0_ref, v1_ref, kk0_ref, kk1_ref, lw0_ref, lw1_ref, b0_ref, b1_ref,
                      kd0_ref, kd1_ref, y0_ref, y1_ref, s_scr):
    i = pl.program_id(0)
    c = RWKV_CHUNK
    n_b = r0_ref.shape[0]

    @pl.when(i == 0)
    def _():
        s_scr[...] = jnp.zeros_like(s_scr)

    bf = lambda x: x.astype(BF16)
    mm = lambda x, y: jnp.dot(bf(x), bf(y), preferred_element_type=F32)

    def cumsum_dot(t_bf16, x):
        hi = bf(x)
        r1 = x - hi.astype(F32)
        mid = bf(r1)
        lo = bf(r1 - mid.astype(F32))
        return (jnp.dot(t_bf16, hi, preferred_element_type=F32) + jnp.dot(t_bf16, mid, preferred_element_type=F32)
                + jnp.dot(t_bf16, lo, preferred_element_type=F32))

    ti = lax.broadcasted_iota(jnp.int32, (c, c), 0)
    si = lax.broadcasted_iota(jnp.int32, (c, c), 1)
    ri = lax.broadcasted_iota(jnp.int32, (LANES, LANES), 0)
    ci = lax.broadcasted_iota(jnp.int32, (LANES, LANES), 1)
    same = (ri // c) == (ci // c)
    lane = lax.broadcasted_iota(jnp.int32, (1, LANES), 1)
    m0 = (lane < RWKV_HD).astype(F32)
    m1 = 1.0 - m0
    stack = lambda x: jnp.concatenate([x * m0, x * m1], axis=0)
    unstack = lambda x: x[:c] + x[c:]
    pairs = range(RWKV_HEADS // 2)
    sls = [slice(p * LANES, (p + 1) * LANES) for p in pairs]

    tri, strict, incl = [], [], []
    for sgn in (1, -1):
        tri.append(jnp.where((ti - si) * sgn >= 0, 1.0, 0.0).astype(BF16))
        strict.append(same & ((ri - ci) * sgn > 0))
        incl.append(same & ((ri - ci) * sgn >= 0))

    refs = ((r0_ref, v0_ref, kk0_ref, lw0_ref, b0_ref, kd0_ref, y0_ref),
            (r1_ref, v1_ref, kk1_ref, lw1_ref, b1_ref, kd1_ref, y1_ref))
    streams = [(d, bi) for d in range(2) for bi in range(n_b)]
    a_t, b_t, k_t, r_t, v_all, g_tot = [], [], [], [], [], []
    for d, bi in streams:
        r_ref, v_ref, kk_ref, lw_ref, b_ref, kd_ref, _ = refs[d]
        lw = lw_ref[0, bi]
        cum = cumsum_dot(tri[d], lw)
        e_neg = jnp.exp(-cum)
        a_t.append(-kk_ref[bi].astype(F32) * jnp.exp(cum - lw))
        b_t.append(b_ref[0, bi].astype(F32) * e_neg)
        k_t.append(kd_ref[0, bi].astype(F32) * e_neg)
        r_t.append(r_ref[bi].astype(F32) * jnp.exp(cum))
        v_all.append(v_ref[bi])
        g_tot.append(jnp.exp(jnp.sum(lw, axis=0, keepdims=True)))

    keys = [(s, p) for s in range(len(streams)) for p in pairs]
    dirs = [streams[s][0] for s, _ in keys]
    a_st = [bf(stack(a_t[s][:, sls[p]])) for s, p in keys]
    r_st = [bf(stack(r_t[s][:, sls[p]])) for s, p in keys]
    v_st = [bf(stack(v_all[s][:, sls[p]])) for s, p in keys]
    b_p = [bf(b_t[s][:, sls[p]]) for s, p in keys]
    k_p = [bf(k_t[s][:, sls[p]]) for s, p in keys]
    ks = range(len(keys))
    gm = [_dot_nt(jnp.concatenate([a_st[j], r_st[j]], axis=0),
                  jnp.concatenate([b_p[j], b_p[j], k_p[j], k_p[j]], axis=0)) for j in ks]
    m_ak = [bf(jnp.where(strict[dirs[j]], gm[j][:LANES, LANES:], 0.0)) for j in ks]
    x = [jnp.concatenate([a_st[j].astype(F32), jnp.dot(m_ak[j], v_st[j], preferred_element_type=F32)], axis=1)
         for j in ks]
    pw = [jnp.where(strict[dirs[j]], gm[j][:LANES, :LANES], 0.0) for j in ks]
    for step in range(6):
        if step:
            pw = [mm(pw[j], pw[j]) for j in ks]
        x = [x[j] + mm(pw[j], x[j]) for j in ks]
    s_bd = [s_scr[streams[s][0], streams[s][1], p] for s, p in keys]
    s_bf = [bf(t) for t in s_bd]
    u_st = [_dot_nt(bf(x[j][:, :LANES]), s_bf[j]) + x[j][:, LANES:] for j in ks]
    m_rb = [bf(jnp.where(incl[dirs[j]], gm[j][LANES:, :LANES], 0.0)) for j in ks]
    m_rk = [bf(jnp.where(incl[dirs[j]], gm[j][LANES:, LANES:], 0.0)) for j in ks]
    for j, (s, p) in enumerate(keys):
        y_st = (_dot_nt(r_st[j], s_bf[j]) + jnp.dot(m_rb[j], bf(u_st[j]), preferred_element_type=F32)
                + jnp.dot(m_rk[j], v_st[j], preferred_element_type=F32))
        refs[streams[s][0]][6][streams[s][1], :, sls[p]] = unstack(y_st).astype(y0_ref.dtype)
    for j, (s, p) in enumerate(keys):
        uv = jnp.concatenate([bf(unstack(u_st[j])), bf(v_all[s][:, sls[p]])], axis=0)
        bk = jnp.concatenate([b_p[j], k_p[j]], axis=0)
        upd = _dot_tn(uv, bk)
        s_scr[streams[s][0], streams[s][1], p] = g_tot[s][:, sls[p]] * (s_bd[j] + jnp.where(same, upd, 0.0))


def _rwkv_scan(r, v, kk, lw, bb, kd, ctx_len):
    b, tt, _ = r.shape
    c = RWKV_CHUNK
    n = tt // c
    nc = ctx_len // c
    shared = [pl.BlockSpec((b, c, MIX_W), functools.partial(lambda i, d: (0, _scan_chunk(d, i, n, nc), 0), d=d))
              for d in range(2)]
    per_dir = [pl.BlockSpec((1, b, c, MIX_W), functools.partial(lambda i, d: (d, 0, _scan_chunk(d, i, n, nc), 0), d=d))
               for d in range(2)]
    out = jax.ShapeDtypeStruct((b, tt, MIX_W), BF16)
    return pl.pallas_call(
        _rwkv_scan_kernel, grid=(n,),
        in_specs=[*shared, *shared, *shared, *per_dir, *per_dir, *per_dir], out_specs=shared, out_shape=[out, out],
        scratch_shapes=[pltpu.VMEM((2, b, RWKV_HEADS // 2, LANES, LANES), F32)],
        compiler_params=_cparams("arbitrary"), name="rwkv_scan",
    )(r, r, v, v, kk, kk, lw, lw, bb, bb, kd, kd)


def _rwkv_finish_kernel(y0_ref, y1_ref, bonus_ref, g_ref, w_ref, b_ref, e_ref, et_ref, o_ref):
    e, et = e_ref[...], et_ref[...]
    y = y0_ref[0].astype(F32) + y1_ref[0].astype(F32)
    mu = _dot_split(_dot_split(y, e), et) * (1.0 / RWKV_HD)
    yc = y - mu
    var = _dot_split(_dot_split(yc * yc, e), et) * (1.0 / RWKV_HD)
    yn = yc * lax.rsqrt(var + RWKV_GN_EPS) * w_ref[...] + b_ref[...]
    o_ref[0] = ((yn + bonus_ref[0].astype(F32)) * g_ref[0].astype(F32)).astype(o_ref.dtype)


def _rwkv_finish(y0, y1, bonus, g, gn_w, gn_b):
    b, tt, _ = y0.shape
    e, et = _head_sum_mats()
    row = pl.BlockSpec((1, ROW_TILE, MIX_W), lambda bb, i: (bb, i, 0))
    full = lambda shape: pl.BlockSpec(shape, lambda bb, i: (0,) * len(shape))
    return pl.pallas_call(
        _rwkv_finish_kernel, grid=(b, tt // ROW_TILE),
        in_specs=[row, row, row, row,
                  full((1, MIX_W)), full((1, MIX_W)), full((MIX_W, LANES)), full((LANES, MIX_W))],
        out_specs=row, out_shape=jax.ShapeDtypeStruct((b, tt, MIX_W), BF16),
        compiler_params=_cparams("parallel", "parallel"), name="rwkv_finish",
    )(y0, y1, bonus, g, gn_w.reshape(1, MIX_W), gn_b.reshape(1, MIX_W), e, et)


def _rope_swap(x):
    lane = lax.broadcasted_iota(jnp.int32, x.shape, 1)
    return jnp.where((lane % 64) < 32, pltpu.roll(x, 96, 1), pltpu.roll(x, 32, 1))


def _rope_tables(t, ctx_len):
    pos = jnp.arange(t, dtype=jnp.int32)
    nf = RET_QK_HD // 4
    inv = ROPE_BASE ** (-jnp.arange(nf, dtype=F32) / nf)
    ang_r = (pos // GRID_W).astype(F32)[:, None] * inv[None, :]
    ang_c = (pos % GRID_W).astype(F32)[:, None] * inv[None, :]
    cos = jnp.concatenate([jnp.cos(ang_r)] * 2 + [jnp.cos(ang_c)] * 2, axis=-1)
    sin = jnp.concatenate([-jnp.sin(ang_r), jnp.sin(ang_r), -jnp.sin(ang_c), jnp.sin(ang_c)], axis=-1)
    cos = jnp.concatenate([jnp.ones((ctx_len, RET_QK_HD), F32), cos], axis=0)
    sin = jnp.concatenate([jnp.zeros((ctx_len, RET_QK_HD), F32), sin], axis=0)
    return cos, sin


def _ret_kernel(dec_ref, q0_ref, q1_ref, k0_ref, k1_ref, v0_ref, v1_ref, cos0_ref, cos1_ref, sin0_ref, sin1_ref,
                o0_ref, o1_ref, r_scr):
    i = pl.program_id(0)
    c = RET_CHUNK
    n_b = q0_ref.shape[0]

    @pl.when(i == 0)
    def _():
        r_scr[...] = jnp.zeros_like(r_scr)

    bf = lambda x: x.astype(BF16)
    ii = lax.broadcasted_iota(jnp.int32, (c, c), 0).astype(F32)
    jj = lax.broadcasted_iota(jnp.int32, (c, c), 1).astype(F32)
    heads = range(RET_HEADS)
    qs = [slice(h * RET_QK_HD, (h + 1) * RET_QK_HD) for h in heads]
    vs = [slice(h * RET_V_HD, (h + 1) * RET_V_HD) for h in heads]
    refs = ((q0_ref, k0_ref, v0_ref, cos0_ref, sin0_ref, o0_ref), (q1_ref, k1_ref, v1_ref, cos1_ref, sin1_ref, o1_ref))
    dmat, q_dec, k_dec, c_dec = {}, {}, {}, {}
    for d in range(2):
        pos = ii if d == 0 else c - 1.0 - ii
        diff = (ii - jj) if d == 0 else (jj - ii)
        for h in heads:
            lg = jnp.log(jax.nn.sigmoid(jnp.full((c, c), dec_ref[d, h], F32)))
            dmat[d, h] = jnp.where(diff >= 0, jnp.exp(lg * jnp.maximum(diff, 0.0)), 0.0)
            q_dec[d, h] = jnp.exp(lg * (pos + 1.0))
            k_dec[d, h] = jnp.exp(lg * (c - 1.0 - pos))
            c_dec[d, h] = jnp.exp(lg[:1, :1] * float(c))
    keys = [(d, bi, h) for d in range(2) for bi in range(n_b) for h in heads]
    q, k, v = {}, {}, {}
    for d, bi, h in keys:
        q_ref, k_ref, v_ref, cos_ref, sin_ref, _ = refs[d]
        cos, sin = cos_ref[...], sin_ref[...]
        qq = q_ref[bi, :, qs[h]]
        kk = k_ref[bi, :, qs[h]]
        q[d, bi, h] = qq * cos + _rope_swap(qq) * sin
        k[d, bi, h] = (kk * cos + _rope_swap(kk) * sin) * (RET_QK_HD ** -0.5)
        v[d, bi, h] = bf(v_ref[bi, :, vs[h]])
    r = {key: r_scr[key] for key in keys}
    s = {(d, bi, h): _dot_nt(bf(q[d, bi, h]), bf(k[d, bi, h])) * dmat[d, h] for d, bi, h in keys}
    for d, bi, h in keys:
        o = jnp.dot(bf(s[d, bi, h]), v[d, bi, h], preferred_element_type=F32)
        o = o + jnp.dot(bf(q[d, bi, h] * q_dec[d, h]), bf(r[d, bi, h]), preferred_element_type=F32)
        refs[d][5][bi, :, vs[h]] = o.astype(o0_ref.dtype)
    for d, bi, h in keys:
        r_scr[d, bi, h] = r[d, bi, h] * c_dec[d, h] + _dot_tn(bf(k[d, bi, h] * k_dec[d, h]), v[d, bi, h])


def _retention(p3, cos, sin, decay, ctx_len):
    b, tt, _ = p3.shape
    c = RET_CHUNK
    n = tt // c
    nc = ctx_len // c
    qb, kb, vb = OFF_RET_Q // RET_QK_W, OFF_RET_K // RET_QK_W, OFF_RET_V // MIX_W

    def per_dir(make):
        return [make(functools.partial(lambda i, dec, d: _scan_chunk(d, i, n, nc), d=d)) for d in range(2)]

    col = lambda width, cb: per_dir(lambda tok: pl.BlockSpec((b, c, width), lambda i, dec: (0, tok(i, dec), cb)))
    tab = per_dir(lambda tok: pl.BlockSpec((c, RET_QK_HD), lambda i, dec: (tok(i, dec), 0)))
    out = jax.ShapeDtypeStruct((b, tt, MIX_W), BF16)
    grid_spec = pltpu.PrefetchScalarGridSpec(
        num_scalar_prefetch=1, grid=(n,),
        in_specs=[*col(RET_QK_W, qb), *col(RET_QK_W, kb), *col(MIX_W, vb), *tab, *tab],
        out_specs=col(MIX_W, 0),
        scratch_shapes=[pltpu.VMEM((2, b, RET_HEADS, RET_QK_HD, RET_V_HD), F32)])
    return pl.pallas_call(
        _ret_kernel, grid_spec=grid_spec, out_shape=[out, out],
        compiler_params=_cparams("arbitrary"), name="retention",
    )(decay, p3, p3, p3, p3, p3, p3, cos, cos, sin, sin)


def _ret_finish_kernel(o0_ref, o1_ref, g_ref, w_ref, b_ref, y_ref):
    for h in range(RET_HEADS):
        sl = slice(h * RET_V_HD, (h + 1) * RET_V_HD)
        o = o0_ref[0, :, sl].astype(F32) + o1_ref[0, :, sl].astype(F32)
        y = _norm(o, RET_GN_EPS) * w_ref[:, sl] + b_ref[:, sl]
        g = g_ref[0, :, sl]
        y_ref[0, :, sl] = (y * (g * jax.nn.sigmoid(g))).astype(y_ref.dtype)


def _ret_finish(o0, o1, p3, gn_w, gn_b):
    b, tt, _ = o0.shape
    gb = OFF_RET_G // MIX_W
    row = pl.BlockSpec((1, ROW_TILE, MIX_W), lambda bb, i: (bb, i, 0))
    vec = pl.BlockSpec((1, MIX_W), lambda bb, i: (0, 0))
    return pl.pallas_call(
        _ret_finish_kernel, grid=(b, tt // ROW_TILE),
        in_specs=[row, row, pl.BlockSpec((1, ROW_TILE, MIX_W), lambda bb, i: (bb, i, gb)), vec, vec],
        out_specs=row, out_shape=jax.ShapeDtypeStruct((b, tt, MIX_W), BF16),
        compiler_params=_cparams("parallel", "parallel"), name="ret_finish",
    )(o0, o1, p3, gn_w.reshape(1, MIX_W), gn_b.reshape(1, MIX_W))


NA_BLOCK_ROWS = 4
NA_PAIRS = 2
NA_BLOCK_TOK = NA_BLOCK_ROWS * GRID_W
NA_WIN_TOK = NA_WIN_ROWS * GRID_W


def _softmax_pv(s_parts, v_parts):
    m = None
    for s in s_parts:
        mm = jnp.max(s, axis=1, keepdims=True)
        m = mm if m is None else jnp.maximum(m, mm)
    den, acc = None, None
    for s, v in zip(s_parts, v_parts):
        e = jnp.exp(s - m)
        dd = jnp.sum(e, axis=1, keepdims=True)
        pv = jnp.dot(e.astype(BF16), v, preferred_element_type=F32)
        den = dd if den is None else den + dd
        acc = pv if acc is None else acc + pv
    return acc / den


def _na_kernel(q_ref, kp_ref, kc_ref, kn_ref, vp_ref, vc_ref, vn_ref, kx_ref, vx_ref, bias_ref, o_ref,
               k_scr, v_scr, *, n_blocks):
    rb = pl.program_id(1)
    n_b = q_ref.shape[0]
    units = [(bi, pp) for bi in range(n_b) for pp in range(NA_PAIRS)]
    ls = [slice(pp * LANES, (pp + 1) * LANES) for pp in range(NA_PAIRS)]
    lane = lax.broadcasted_iota(jnp.int32, (1, LANES), 1)
    head_masks = [(lane < NA_HD).astype(F32), (lane >= NA_HD).astype(F32)]
    heads = range(2)
    kx = {(bi, pp): kx_ref[bi, :, ls[pp]].astype(BF16) for bi, pp in units}
    vx = {(bi, pp): vx_ref[bi, :, ls[pp]].astype(BF16) for bi, pp in units}
    scale = NA_HD ** -0.5

    @pl.when(rb == 0)
    def _():
        for bi, pp in units:
            q = q_ref[bi, :, ls[pp]] * scale
            out = jnp.zeros((NA_BLOCK_TOK, LANES), F32)
            for hm in head_masks:
                s = _dot_nt((q * hm).astype(BF16), kx[bi, pp])
                out = out + _softmax_pv([s], [vx[bi, pp]]) * hm
            o_ref[bi, :, ls[pp]] = out.astype(o_ref.dtype)

    @pl.when(rb > 0)
    def _():
        t = NA_BLOCK_TOK
        for bi in range(n_b):
            k_scr[bi, 0:t, :] = kp_ref[bi].astype(BF16)
            k_scr[bi, t:2 * t, :] = kc_ref[bi].astype(BF16)
            k_scr[bi, 2 * t:3 * t, :] = kn_ref[bi].astype(BF16)
            v_scr[bi, 0:t, :] = vp_ref[bi].astype(BF16)
            v_scr[bi, t:2 * t, :] = vc_ref[bi].astype(BF16)
            v_scr[bi, 2 * t:3 * t, :] = vn_ref[bi].astype(BF16)
        first = rb == 1
        last = rb == n_blocks
        rows = range(NA_BLOCK_ROWS)
        rs = [slice(j * GRID_W, (j + 1) * GRID_W) for j in rows]
        starts, oi = [], []
        for j in rows:
            off = jnp.where(last, 0, jnp.where(first, NA_BLOCK_ROWS, j))
            oi.append(jnp.where(last, NA_BLOCK_ROWS + j, jnp.where(first, j, NA_WIN_ROWS // 2)))
            starts.append(pl.multiple_of(off * GRID_W, GRID_W))
        kw = {(bi, pp, j): k_scr[bi, pl.ds(starts[j], NA_WIN_TOK), ls[pp]] for bi, pp in units for j in rows}
        vw = {(bi, pp, j): v_scr[bi, pl.ds(starts[j], NA_WIN_TOK), ls[pp]] for bi, pp in units for j in rows}
        bias = {(pp, j, h): bias_ref[oi[j], 2 * pp + h] for pp in range(NA_PAIRS) for j in rows for h in heads}
        qh = {(bi, pp, h): (q_ref[bi, :, ls[pp]] * scale * head_masks[h]).astype(BF16) for bi, pp in units for h in heads}
        uh = [(bi, pp, h) for bi, pp in units for h in heads]
        ujh = [(bi, pp, j, h) for bi, pp in units for j in rows for h in heads]
        s_ctx = {(bi, pp, h): _dot_nt(qh[bi, pp, h], kx[bi, pp]) for bi, pp, h in uh}
        s_loc = {(bi, pp, j, h): _dot_nt(qh[bi, pp, h][rs[j]], kw[bi, pp, j]) + bias[pp, j, h] for bi, pp, j, h in ujh}
        m_ctx = {k: jnp.max(s_ctx[k], axis=1, keepdims=True) for k in uh}
        m = {(bi, pp, j, h): jnp.maximum(jnp.max(s_loc[bi, pp, j, h], axis=1, keepdims=True), m_ctx[bi, pp, h][rs[j]])
             for bi, pp, j, h in ujh}
        m_all = {(bi, pp, h): jnp.concatenate([m[bi, pp, j, h] for j in rows], axis=0) for bi, pp, h in uh}
        e_ctx = {k: jnp.exp(s_ctx[k] - m_all[k]) for k in uh}
        e_loc = {k: jnp.exp(s_loc[k] - m[k]) for k in ujh}
        d_ctx = {k: jnp.sum(e_ctx[k], axis=1, keepdims=True) for k in uh}
        pv_ctx = {(bi, pp, h): jnp.dot(e_ctx[bi, pp, h].astype(BF16), vx[bi, pp], preferred_element_type=F32)
                  for bi, pp, h in uh}
        for bi, pp in units:
            for j in rows:
                out = None
                for h in heads:
                    den = jnp.sum(e_loc[bi, pp, j, h], axis=1, keepdims=True) + d_ctx[bi, pp, h][rs[j]]
                    pv = (jnp.dot(e_loc[bi, pp, j, h].astype(BF16), vw[bi, pp, j], preferred_element_type=F32)
                          + pv_ctx[bi, pp, h][rs[j]])
                    term = pv / den * head_masks[h]
                    out = term if out is None else out + term
                o_ref[bi, rs[j], ls[pp]] = out.astype(o_ref.dtype)


def _na_bias_table(rpb):
    col = jnp.arange(GRID_W)
    cs = jnp.clip(col - NA_WIN_COLS // 2, 0, GRID_W - NA_WIN_COLS)
    col_ok = (col[None, :] >= cs[:, None]) & (col[None, :] < cs[:, None] + NA_WIN_COLS)
    col_idx = jnp.clip(col[None, :] - col[:, None] + (NA_WIN_COLS - 1), 0, 2 * NA_WIN_COLS - 2)
    win = jnp.arange(NA_WIN_ROWS)
    row_idx = win[None, :] - win[:, None] + (NA_WIN_ROWS - 1)
    row_hot = (row_idx[:, :, None] == jnp.arange(2 * NA_WIN_ROWS - 1)).astype(F32)
    col_hot = (col_idx[:, :, None] == jnp.arange(2 * NA_WIN_COLS - 1)).astype(F32)
    bias = jnp.einsum("ora,hab,qcb->ohqrc", row_hot, rpb.astype(F32), col_hot, precision=HI)
    bias = jnp.where(col_ok[None, None, :, None, :], bias, -1e30)
    return bias.reshape(NA_WIN_ROWS, NA_HEADS, GRID_W, NA_WIN_TOK)


def _na(p3, bias_tab, ctx_len):
    b, tt, _ = p3.shape
    assert ctx_len == NA_BLOCK_TOK
    t = tt - ctx_len
    n_rows = t // GRID_W
    assert n_rows % NA_BLOCK_ROWS == 0 and n_rows >= NA_WIN_ROWS
    nb = n_rows // NA_BLOCK_ROWS
    w = NA_PAIRS * LANES
    qb, kb, vb = OFF_NA_Q // w, OFF_NA_K // w, OFF_NA_V // w
    blk = (b, NA_BLOCK_TOK, w)

    def spec(colb, shift):
        return pl.BlockSpec(blk, lambda pr, rb: (0, jnp.clip(rb + shift, 1, nb), colb + pr))

    return pl.pallas_call(
        functools.partial(_na_kernel, n_blocks=nb), grid=(NA_HEADS // (2 * NA_PAIRS), nb + 1),
        in_specs=[pl.BlockSpec(blk, lambda pr, rb: (0, rb, qb + pr)),
                  spec(kb, -1), spec(kb, 0), spec(kb, 1), spec(vb, -1), spec(vb, 0), spec(vb, 1),
                  pl.BlockSpec(blk, lambda pr, rb: (0, 0, kb + pr)),
                  pl.BlockSpec(blk, lambda pr, rb: (0, 0, vb + pr)),
                  pl.BlockSpec((NA_WIN_ROWS, 2 * NA_PAIRS, GRID_W, NA_WIN_TOK), lambda pr, rb: (0, pr, 0, 0))],
        out_specs=pl.BlockSpec(blk, lambda pr, rb: (0, rb, pr)),
        out_shape=jax.ShapeDtypeStruct((b, tt, MIX_W), BF16),
        scratch_shapes=[pltpu.VMEM((b, 3 * NA_BLOCK_TOK, w), BF16), pltpu.VMEM((b, 3 * NA_BLOCK_TOK, w), BF16)],
        compiler_params=_cparams("parallel", "parallel"), name="na",
    )(p3, p3, p3, p3, p3, p3, p3, p3, p3, bias_tab)


def _build_mod(gate, shift, scale, b):
    rows = jnp.stack([gate, shift, scale], axis=1)
    lat = rows[:b]
    ctx = jnp.broadcast_to(rows[b][None], lat.shape)
    mod = jnp.stack([ctx, lat], axis=1)
    return jnp.pad(mod, ((0, 0), (0, 0), (0, 5), (0, 0)))


def kernel(x, c, ctx, c_ctx, ada_w, ada_b, w_in, rwkv_shift, rwkv_w0, rwkv_wB, rwkv_a0, rwkv_aB, rwkv_gB, rwkv_kk, rwkv_ka, rwkv_rk, rwkv_gn_w, rwkv_gn_b, ret_decay, ret_gn_w, ret_gn_b, na_rpb, w_branch, w_out, ln1_w, ln1_b, ln2_w, ln2_b, router_w, router_b, exp_w1, exp_b1, exp_w2, exp_b2):
    b, t, d = x.shape
    ctx_len = ctx.shape[1]
    tt = ctx_len + t
    n = b * tt
    tpb = tt // ROW_TILE
    depth = ada_w.shape[0]
    alpha = (2 * depth) ** 0.25
    assert d == D_MODEL and b + 1 <= 8 and ctx_len == ROW_TILE and t % ROW_TILE == 0

    xa = jnp.concatenate([ctx, x], axis=1).reshape(n, d)
    cond = jnp.concatenate([c, c_ctx[None], jnp.zeros((8 - b - 1, d), F32)], axis=0)
    ada = _adaln(cond.T, ada_w, ada_b, b + 1).reshape(depth, 8, 6, d)
    cos, sin = _rope_tables(t, ctx_len)
    n_exp = router_w.shape[-1]
    rw_pad = jnp.pad(router_w, ((0, 0), (0, 0), (0, LANES - n_exp)))
    rw_trunc = lax.bitcast_convert_type(lax.bitcast_convert_type(rw_pad, jnp.uint32) & jnp.uint32(0xFFFF0000), F32)
    rw_split = jnp.stack([rw_trunc.astype(BF16), (rw_pad - rw_trunc).astype(BF16)], axis=1)
    rb_pad = jnp.pad(router_b, ((0, 0), (0, LANES - n_exp)), constant_values=-1e30)
    zero = jnp.zeros((8, d), F32)

    h = _modulate(xa, _build_mod(zero, ada[0, :, 0], ada[0, :, 1], b), tpb)
    for l in range(depth):
        p = _in_proj(h, w_in, l)
        p3 = p.reshape(b, tt, D_IN_PROJ)

        r, v, kk, g, bonus, lw, bb, kd = _rwkv_prep(p3, rwkv_shift[l], rwkv_w0[l], rwkv_wB[l], rwkv_a0[l], rwkv_aB[l],
                                                    rwkv_gB[l], rwkv_kk[l], rwkv_ka[l], rwkv_rk[l], ctx_len)
        y_fwd, y_bwd = _rwkv_scan(r, v, kk, lw, bb, kd, ctx_len)
        ya = _rwkv_finish(y_fwd, y_bwd, bonus, g, rwkv_gn_w[l], rwkv_gn_b[l])

        o_fwd, o_bwd = _retention(p3, cos, sin, ret_decay[l], ctx_len)
        yb = _ret_finish(o_fwd, o_bwd, p3, ret_gn_w[l], ret_gn_b[l])

        yn = _na(p3, _na_bias_table(na_rpb[l]), ctx_len)

        merged = _merge(ya.reshape(n, MIX_W), yb.reshape(n, MIX_W), yn.reshape(n, MIX_W), p, w_branch[l].astype(BF16))
        y = _matmul(merged, w_out[l].astype(BF16), 512)

        mod_f = _build_mod(ada[l, :, 2], ada[l, :, 3], ada[l, :, 4], b)
        x1, h2, route = _resid_ln(xa, y, mod_f, ln1_w[l], ln1_b[l], tpb, alpha, rw_split[l], rb_pad[l][None])
        tok_of_pos, pos_of_slot, tile_expert, n_used = _moe_dispatch(route, n_exp)
        xs = jnp.take(h2, tok_of_pos, axis=0, mode="clip")
        ys = _moe(xs, tile_expert, n_used, exp_w1, exp_b1[l], exp_w2, exp_b2[l], l)
        f4 = jnp.take(ys, pos_of_slot, axis=0, mode="clip").reshape(TOP_K, n, d)
        nxt = min(l + 1, depth - 1)
        mod_a = _build_mod(ada[l, :, 5], ada[nxt, :, 0], ada[nxt, :, 1], b)
        xa, h = _resid_ln(x1, f4, mod_a, ln2_w[l], ln2_b[l], tpb, alpha, route=route)
    return xa.reshape(b, tt, d)[:, ctx_len:]
```
